```python
import math
import jax
import jax.numpy as jnp
from jax import lax
import numpy as np

D_MODEL = 1024
BATCH = 16
SEQ = 2048
DEPTH = 2
DEC_BATCH = 128
DEC_SEQ = 1
PAST_LEN = 16384
PAGE_SIZE = 128

EPS = 1e-5
N_MAMBA_LAYERS = (DEPTH + 1) // 2
N_ATTN_LAYERS = DEPTH // 2

SSM_EXPAND = 2
D_INNER = SSM_EXPAND * D_MODEL
SSM_HEAD_DIM = 64
N_SSM_HEADS = D_INNER // SSM_HEAD_DIM
D_STATE = 128
N_BC_GROUPS = 8
HEADS_PER_BC = N_SSM_HEADS // N_BC_GROUPS
CONV_W = 4
CONV_DIM = D_INNER + 2 * N_BC_GROUPS * D_STATE
D_IN_PROJ = D_INNER + CONV_DIM + N_SSM_HEADS
SSD_CHUNK = 128
NORM_GROUP = D_INNER // N_BC_GROUPS

N_Q_HEADS = 16
N_KV_HEADS = 4
Q_PER_KV = N_Q_HEADS // N_KV_HEADS
HEAD_DIM = 64
QKV_DIM = (N_Q_HEADS + 2 * N_KV_HEADS) * HEAD_DIM
WINDOW = 128
ATTN_BLOCK = WINDOW

N_EXPERT_GROUPS = 4
EXPERTS_PER_GROUP = 8
N_EXPERTS = N_EXPERT_GROUPS * EXPERTS_PER_GROUP
TOP_K = 2
D_EXPERT = 512
MOE_BLOCK = 128

kernel_name = 'hybrid_ssd_swa_hmoe_step'

F32 = jnp.float32


def rmsnorm(x, w):
    xf = x.astype(F32)
    xf = xf * lax.rsqrt(jnp.mean(xf * xf, axis=-1, keepdims=True) + EPS)
    return (xf * w.astype(F32)).astype(x.dtype)


def gated_rmsnorm(y, z, w):
    g = y.astype(F32) * jax.nn.silu(z.astype(F32))
    gg = g.reshape(g.shape[:-1] + (N_BC_GROUPS, NORM_GROUP))
    gg = gg * lax.rsqrt(jnp.mean(gg * gg, axis=-1, keepdims=True) + EPS)
    return (gg.reshape(g.shape) * w.astype(F32)).astype(z.dtype)


def causal_dwconv(xbc, prev, w, b):
    L = xbc.shape[1]
    xpad = jnp.concatenate([prev.astype(xbc.dtype), xbc], axis=1)
    out = sum((xpad[:, k:k + L] * w[k] for k in range(CONV_W)), b)
    return jax.nn.silu(out), xpad[:, L:]


def ssd_scan(x, dt, a, bm, cm, s0):
    bsz, L = x.shape[:2]
    cs = min(SSD_CHUNK, L)
    nc = -(-L // cs)
    pad = nc * cs - L
    x = x.astype(F32)
    bm = bm.astype(F32)
    cm = cm.astype(F32)
    if pad:
        x = jnp.pad(x, ((0, 0), (0, pad), (0, 0), (0, 0)))
        dt = jnp.pad(dt, ((0, 0), (0, pad), (0, 0)))
        bm = jnp.pad(bm, ((0, 0), (0, pad), (0, 0), (0, 0)))
        cm = jnp.pad(cm, ((0, 0), (0, pad), (0, 0), (0, 0)))
    xr = x.reshape(bsz, nc, cs, N_BC_GROUPS, HEADS_PER_BC, SSM_HEAD_DIM)
    dtr = dt.reshape(bsz, nc, cs, N_BC_GROUPS, HEADS_PER_BC)
    br = bm.reshape(bsz, nc, cs, N_BC_GROUPS, D_STATE)
    cr = cm.reshape(bsz, nc, cs, N_BC_GROUPS, D_STATE)
    acum = jnp.cumsum(dtr * a.reshape(N_BC_GROUPS, HEADS_PER_BC), axis=2)
    xdt = xr * dtr[..., None]
    causal = jnp.tril(jnp.ones((cs, cs), dtype=bool))
    seg = acum[:, :, :, None] - acum[:, :, None, :]
    decay_ls = jnp.exp(jnp.where(causal[:, :, None, None], seg, -jnp.inf))
    cb = jnp.einsum('bclgn,bcsgn->bclsg', cr, br)
    y_diag = jnp.einsum('bclsgk,bcsgkp->bclgkp', cb[..., None] * decay_ls, xdt)
    to_end = jnp.exp(acum[:, :, -1:] - acum)
    chunk_states = jnp.einsum('bclgn,bclgkp->bcgkpn', br, xdt * to_end[..., None])
    chunk_decay = jnp.exp(acum[:, :, -1])

    def step(s, inp):
        st, dec = inp
        return s * dec[..., None, None] + st, s

    s_init = s0.astype(F32).reshape(bsz, N_BC_GROUPS, HEADS_PER_BC, SSM_HEAD_DIM, D_STATE)
    s_last, s_in = lax.scan(step, s_init, (jnp.moveaxis(chunk_states, 1, 0),
                                           jnp.moveaxis(chunk_decay, 1, 0)))
    s_in = jnp.moveaxis(s_in, 0, 1)
    y_off = jnp.einsum('bclgn,bcgkpn->bclgkp', cr, s_in) * jnp.exp(acum)[..., None]
    y = (y_diag + y_off).reshape(bsz, nc * cs, N_SSM_HEADS, SSM_HEAD_DIM)[:, :L]
    return y, s_last.reshape(bsz, N_SSM_HEADS, SSM_HEAD_DIM, D_STATE)


def mamba_mixer(h, conv_prev, ssm_prev, w_in, conv_w, conv_b, dt_bias, a_log, d_skip, norm_w, w_out):
    bsz, L, _ = h.shape
    proj = h @ w_in
    z = proj[..., :D_INNER]
    xbc, conv_new = causal_dwconv(proj[..., D_INNER:D_INNER + CONV_DIM], conv_prev, conv_w, conv_b)
    dt = jax.nn.softplus(proj[..., D_INNER + CONV_DIM:].astype(F32) + dt_bias.astype(F32))
    nbc = N_BC_GROUPS * D_STATE
    xs = xbc[..., :D_INNER].reshape(bsz, L, N_SSM_HEADS, SSM_HEAD_DIM)
    bm = xbc[..., D_INNER:D_INNER + nbc].reshape(bsz, L, N_BC_GROUPS, D_STATE)
    cm = xbc[..., D_INNER + nbc:].reshape(bsz, L, N_BC_GROUPS, D_STATE)
    a = -jnp.exp(a_log.astype(F32))
    y, ssm_new = ssd_scan(xs, dt, a, bm, cm, ssm_prev)
    y = y + xs.astype(F32) * d_skip.astype(F32)[:, None]
    y = gated_rmsnorm(y.reshape(bsz, L, D_INNER), z, norm_w)
    return y @ w_out, conv_new, ssm_new


def sink_softmax(s, sink):
    sink = sink.astype(F32)
    m = jnp.maximum(jnp.max(s, axis=-1, keepdims=True), sink)
    p = jnp.exp(s - m)
    return p / (jnp.sum(p, axis=-1, keepdims=True) + jnp.exp(sink - m))


def split_qkv(h, w_qkv, b_qkv):
    bsz, L, _ = h.shape
    qkv = h @ w_qkv + b_qkv
    nq = N_Q_HEADS * HEAD_DIM
    nk = N_KV_HEADS * HEAD_DIM
    q = qkv[..., :nq].reshape(bsz, L, N_KV_HEADS, Q_PER_KV, HEAD_DIM)
    k = qkv[..., nq:nq + nk].reshape(bsz, L, N_KV_HEADS, HEAD_DIM)
    v = qkv[..., nq + nk:].reshape(bsz, L, N_KV_HEADS, HEAD_DIM)
    return q, k, v


def attn_prompt(h, w_qkv, b_qkv, sinks, w_o, b_o):
    bsz, L, _ = h.shape
    q, k, v = split_qkv(h, w_qkv, b_qkv)
    nb = L // ATTN_BLOCK
    qb = q.reshape(bsz, nb, ATTN_BLOCK, N_KV_HEADS, Q_PER_KV, HEAD_DIM)

    def band(t):
        tb = t.reshape(bsz, nb, ATTN_BLOCK, N_KV_HEADS, HEAD_DIM)
        prev = jnp.pad(tb, ((0, 0), (1, 0), (0, 0), (0, 0), (0, 0)))[:, :nb]
        return jnp.concatenate([prev, tb], axis=2)

    kk, vv = band(k), band(v)
    s = jnp.einsum('bnqhgd,bnkhd->bnhgqk', qb, kk).astype(F32) * (HEAD_DIM ** -0.5)
    qi = jnp.arange(ATTN_BLOCK)[:, None] + ATTN_BLOCK
    ki = jnp.arange(2 * ATTN_BLOCK)[None, :]
    diff = qi - ki
    band_ok = (diff >= 0) & (diff <= WINDOW)
    has_prev = (jnp.arange(nb) > 0)[:, None, None] | (ki >= ATTN_BLOCK)[None]
    ok = band_ok[None] & has_prev
    s = jnp.where(ok[None, :, None, None], s, -jnp.inf)
    p = sink_softmax(s, sinks.reshape(N_KV_HEADS, Q_PER_KV)[None, None, :, :, None, None])
    o = jnp.einsum('bnhgqk,bnkhd->bnqhgd', p.astype(vv.dtype), vv)
    o = o.reshape(bsz, L, N_Q_HEADS * HEAD_DIM)
    keep = min(WINDOW, L)
    return o @ w_o + b_o, k[:, L - keep:], v[:, L - keep:]


def attn_sample(h, k_buf, v_buf, w_qkv, b_qkv, sinks, w_o, b_o):
    bsz, S, _ = h.shape
    q, k, v = split_qkv(h, w_qkv, b_qkv)
    wb = k_buf.shape[1]
    kk = jnp.concatenate([k_buf.astype(k.dtype), k], axis=1)
    vv = jnp.concatenate([v_buf.astype(v.dtype), v], axis=1)
    s = jnp.einsum('bqhgd,bkhd->bhgqk', q, kk).astype(F32) * (HEAD_DIM ** -0.5)
    diff = (wb + jnp.arange(S))[:, None] - jnp.arange(wb + S)[None, :]
    ok = (diff >= 0) & (diff <= WINDOW)
    s = jnp.where(ok, s, -jnp.inf)
    p = sink_softmax(s, sinks.reshape(N_KV_HEADS, Q_PER_KV)[None, :, :, None, None])
    o = jnp.einsum('bhgqk,bkhd->bqhgd', p.astype(vv.dtype), vv).reshape(bsz, S, N_Q_HEADS * HEAD_DIM)
    return o @ w_o + b_o, kk[:, S:], vv[:, S:]


def expert_dispatch(x, eid, gates, w_gate, w_up, w_down):
    T, D = x.shape
    n_assign = T * TOP_K
    e = eid.reshape(-1)
    g = gates.reshape(-1)
    tok = jnp.repeat(jnp.arange(T, dtype=jnp.int32), TOP_K)
    order = jnp.argsort(e)
    e_s = e[order]
    counts = jnp.bincount(e, length=N_EXPERTS)
    padded = (counts + MOE_BLOCK - 1) // MOE_BLOCK * MOE_BLOCK
    start = jnp.cumsum(counts) - counts
    pend = jnp.cumsum(padded)
    pstart = pend - padded
    dest = pstart[e_s] + jnp.arange(n_assign, dtype=jnp.int32) - start[e_s]
    n_blocks = -(-(n_assign + N_EXPERTS * (MOE_BLOCK - 1)) // MOE_BLOCK)
    n_slots = n_blocks * MOE_BLOCK
    slot_tok = jnp.full((n_slots,), T, jnp.int32).at[dest].set(tok[order])
    slot_gate = jnp.zeros((n_slots,), F32).at[dest].set(g[order])
    blk_exp = jnp.minimum(jnp.searchsorted(pend, jnp.arange(n_blocks) * MOE_BLOCK, side='right'),
                          N_EXPERTS - 1).astype(jnp.int32)
    x_pad = jnp.concatenate([x, jnp.zeros((1, D), x.dtype)], axis=0)
    xb = x_pad[slot_tok].reshape(n_blocks, MOE_BLOCK, D)

    def run_block(args):
        xi, ei = args
        hid = jax.nn.silu(xi @ w_gate[ei]) * (xi @ w_up[ei])
        return hid @ w_down[ei]

    yb = lax.map(run_block, (xb, blk_exp)).reshape(n_slots, D)
    y = jnp.zeros((T + 1, D), F32).at[slot_tok].add(yb.astype(F32) * slot_gate[:, None])
    return y[:T].astype(x.dtype)


def hier_moe(h, w_group, b_group, w_expert, b_expert, w_gate, w_up, w_down):
    bsz, L, D = h.shape
    x = h.reshape(-1, D)
    T = x.shape[0]
    gl = (x @ w_group).astype(F32) + b_group.astype(F32)
    pg, gi = lax.top_k(jax.nn.softmax(gl, axis=-1), 1)
    el = ((x @ w_expert).astype(F32) + b_expert.astype(F32)).reshape(T, N_EXPERT_GROUPS, EXPERTS_PER_GROUP)
    el_sel = jnp.take_along_axis(el, gi[:, :, None], axis=1)[:, 0]
    tp, ti = lax.top_k(jax.nn.softmax(el_sel, axis=-1), TOP_K)
    gates = pg * tp / jnp.sum(tp, axis=-1, keepdims=True)
    eid = gi * EXPERTS_PER_GROUP + ti
    return expert_dispatch(x, eid, gates, w_gate, w_up, w_down).reshape(bsz, L, D)


def setup_inputs(seed: int = 0) -> dict:
    key = jax.random.key(seed)
    ks = iter(jax.random.split(key, 32))

    def nrm(shape, scale):
        return scale * jax.random.normal(next(ks), shape, F32)

    nm, na = N_MAMBA_LAYERS, N_ATTN_LAYERS
    win_buf = min(WINDOW, PAST_LEN)
    inp = {}
    inp['x_prompt'] = nrm((BATCH, SEQ, D_MODEL), 1.0)
    inp['x_sample'] = nrm((DEC_BATCH, DEC_SEQ, D_MODEL), 1.0)
    inp['state_ssm'] = nrm((nm, DEC_BATCH, N_SSM_HEADS, SSM_HEAD_DIM, D_STATE), 0.1)
    inp['state_conv'] = nrm((nm, DEC_BATCH, CONV_W - 1, CONV_DIM), 1.0)
    inp['cache_k_win'] = nrm((na, DEC_BATCH, win_buf, N_KV_HEADS, HEAD_DIM), 1.0)
    inp['cache_v_win'] = nrm((na, DEC_BATCH, win_buf, N_KV_HEADS, HEAD_DIM), 1.0)
    inp['mamba_w_in'] = nrm((nm, D_MODEL, D_IN_PROJ), D_MODEL ** -0.5)
    inp['mamba_conv_w'] = nrm((nm, CONV_W, CONV_DIM), CONV_W ** -0.5)
    inp['mamba_conv_b'] = nrm((nm, CONV_DIM), 0.02)
    dt = jnp.exp(jax.random.uniform(next(ks), (nm, N_SSM_HEADS), F32,
                                    minval=math.log(1e-3), maxval=math.log(1e-1)))
    inp['mamba_dt_bias'] = dt + jnp.log(-jnp.expm1(-dt))
    inp['mamba_a_log'] = jnp.log(jax.random.uniform(next(ks), (nm, N_SSM_HEADS), F32, minval=1.0, maxval=16.0))
    inp['mamba_d'] = 1.0 + nrm((nm, N_SSM_HEADS), 0.1)
    inp['mamba_norm_w'] = 1.0 + nrm((nm, D_INNER), 0.02)
    inp['mamba_w_out'] = nrm((nm, D_INNER, D_MODEL), D_INNER ** -0.5)
    inp['attn_w_qkv'] = nrm((na, D_MODEL, QKV_DIM), D_MODEL ** -0.5)
    inp['attn_b_qkv'] = nrm((na, QKV_DIM), 0.02)
    inp['attn_sinks'] = nrm((na, N_Q_HEADS), 0.5)
    inp['attn_w_o'] = nrm((na, N_Q_HEADS * HEAD_DIM, D_MODEL), (N_Q_HEADS * HEAD_DIM) ** -0.5)
    inp['attn_b_o'] = nrm((na, D_MODEL), 0.02)
    inp['norm_mix'] = 1.0 + nrm((DEPTH, D_MODEL), 0.02)
    inp['norm_ffn'] = 1.0 + nrm((DEPTH, D_MODEL), 0.02)
    inp['router_w_group'] = nrm((DEPTH, D_MODEL, N_EXPERT_GROUPS), D_MODEL ** -0.5)
    inp['router_b_group'] = nrm((DEPTH, N_EXPERT_GROUPS), 0.01)
    inp['router_w_expert'] = nrm((DEPTH, D_MODEL, N_EXPERTS), D_MODEL ** -0.5)
    inp['router_b_expert'] = nrm((DEPTH, N_EXPERTS), 0.01)
    inp['expert_w_gate'] = nrm((DEPTH, N_EXPERTS, D_MODEL, D_EXPERT), D_MODEL ** -0.5)
    inp['expert_w_up'] = nrm((DEPTH, N_EXPERTS, D_MODEL, D_EXPERT), D_MODEL ** -0.5)
    inp['expert_w_down'] = nrm((DEPTH, N_EXPERTS, D_EXPERT, D_MODEL), D_EXPERT ** -0.5)
    inp['norm_final'] = 1.0 + nrm((D_MODEL,), 0.02)
    return inp


def reference(x_prompt, x_sample, state_ssm, state_conv, cache_k_win, cache_v_win,
              mamba_w_in, mamba_conv_w, mamba_conv_b, mamba_dt_bias, mamba_a_log, mamba_d,
              mamba_norm_w, mamba_w_out, attn_w_qkv, attn_b_qkv, attn_sinks, attn_w_o, attn_b_o,
              norm_mix, norm_ffn, router_w_group, router_b_group, router_w_expert, router_b_expert,
              expert_w_gate, expert_w_up, expert_w_down, norm_final):
    xp, xs = x_prompt, x_sample
    bp = xp.shape[0]
    ssm_p, conv_p, kp_l, vp_l = [], [], [], []
    ssm_s, conv_s, ks_l, vs_l = [], [], [], []
    for i in range(DEPTH):
        j = i // 2
        hp = rmsnorm(xp, norm_mix[i])
        hs = rmsnorm(xs, norm_mix[i])
        if i % 2 == 0:
            mw = (mamba_w_in[j], mamba_conv_w[j], mamba_conv_b[j], mamba_dt_bias[j],
                  mamba_a_log[j], mamba_d[j], mamba_norm_w[j], mamba_w_out[j])
            conv0 = jnp.zeros((bp, CONV_W - 1, CONV_DIM), xp.dtype)
            ssm0 = jnp.zeros((bp, N_SSM_HEADS, SSM_HEAD_DIM, D_STATE), F32)
            op, cp, sp = mamba_mixer(hp, conv0, ssm0, *mw)
            os_, cs_, ss_ = mamba_mixer(hs, state_conv[j], state_ssm[j], *mw)
            ssm_p.append(sp)
            conv_p.append(cp)
            ssm_s.append(ss_)
            conv_s.append(cs_)
        else:
            aw = (attn_w_qkv[j], attn_b_qkv[j], attn_sinks[j], attn_w_o[j], attn_b_o[j])
            op, kp, vp = attn_prompt(hp, *aw)
            os_, ks_, vs_ = attn_sample(hs, cache_k_win[j], cache_v_win[j], *aw)
            kp_l.append(kp)
            vp_l.append(vp)
            ks_l.append(ks_)
            vs_l.append(vs_)
        xp = xp + op
        xs = xs + os_
        moe_w = (router_w_group[i], router_b_group[i], router_w_expert[i], router_b_expert[i],
                 expert_w_gate[i], expert_w_up[i], expert_w_down[i])
        xp = xp + hier_moe(rmsnorm(xp, norm_ffn[i]), *moe_w)
        xs = xs + hier_moe(rmsnorm(xs, norm_ffn[i]), *moe_w)
    y_prompt = rmsnorm(xp, norm_final)
    y_sample = rmsnorm(xs, norm_final)
    return (y_prompt, y_sample,
            jnp.stack(ssm_p), jnp.stack(conv_p), jnp.stack(kp_l), jnp.stack(vp_l),
            jnp.stack(ssm_s), jnp.stack(conv_s), jnp.stack(ks_l), jnp.stack(vs_l))
```

```python
import functools
import math

import jax
import jax.numpy as jnp
from jax import lax
from jax.experimental import pallas as pl
from jax.experimental.pallas import tpu as pltpu

F32 = jnp.float32
BF16 = jnp.bfloat16
I32 = jnp.int32

EPS = 1e-5
LANES = 128
VMEM_LIMIT = 56 * 1024 * 1024

SSM_HEAD_DIM = 64
D_STATE = 128
N_BC_GROUPS = 8
CONV_W = 4
SSD_CHUNK = 128
N_Q_HEADS = 16
N_KV_HEADS = 4
HEAD_DIM = 64
WINDOW = 128
N_EXPERT_GROUPS = 4
EXPERTS_PER_GROUP = 8
N_EXPERTS = N_EXPERT_GROUPS * EXPERTS_PER_GROUP
TOP_K = 2
MOE_BLOCK = 256


def _cparams(sem):
    return pltpu.CompilerParams(dimension_semantics=sem, vmem_limit_bytes=VMEM_LIMIT)


def _full(shape):
    n = len(shape)
    return pl.BlockSpec(shape, lambda *_: (0,) * n)


def _rms(x, w):
    return x * lax.rsqrt(jnp.mean(x * x, axis=-1, keepdims=True) + EPS) * w


def _silu(x):
    return x / (1.0 + jnp.exp(-x))


def _softplus(x):
    return jnp.maximum(x, 0.0) + jnp.log(1.0 + jnp.exp(-jnp.abs(x)))


def _bdot(a, b):
    return jnp.dot(a.astype(BF16), b.astype(BF16), preferred_element_type=F32)


def _bdot_nt(a, b):
    return lax.dot_general(a.astype(BF16), b.astype(BF16), (((1,), (1,)), ((), ())),
                           preferred_element_type=F32)


def _bdot_tn(a, b):
    return lax.dot_general(a.astype(BF16), b.astype(BF16), (((0,), (0,)), ((), ())),
                           preferred_element_type=F32)


def _fdot(a, b):
    return jnp.dot(a, b, preferred_element_type=F32, precision=lax.Precision.HIGHEST)


def _mamba_prompt_body(x_ref, nw_ref, win_ref, cw_ref, cb_ref, dtb_ref, alog_ref, dsk_ref, gnw_ref,
                       wout_ref, out_ref, conv_ref, ssm_ref, h_buf, xbc_buf, xc_buf, st_buf, y_buf,
                       *, d_inner, n_heads):
    ts = x_ref.shape[0]
    cs = SSD_CHUNK
    hp = SSM_HEAD_DIM
    nst = D_STATE
    gw = d_inner // N_BC_GROUPS
    hpg = n_heads // N_BC_GROUPS
    conv_dim = d_inner + 2 * N_BC_GROUPS * nst
    s = pl.program_id(1)

    @pl.when(s == 0)
    def _():
        xbc_buf[0:8, :] = jnp.zeros((8, conv_dim), F32)
        st_buf[...] = jnp.zeros_like(st_buf)

    h_buf[...] = _rms(x_ref[...], nw_ref[...]).astype(BF16)
    dtr = jnp.dot(h_buf[...], win_ref[:, d_inner + conv_dim:], preferred_element_type=F32)
    ct = 512
    for j in range(conv_dim // ct):
        cols = slice(j * ct, (j + 1) * ct)
        xbc_buf[8:8 + ts, cols] = jnp.dot(h_buf[...], win_ref[:, d_inner + j * ct:d_inner + (j + 1) * ct],
                                          preferred_element_type=F32)
        acc = cb_ref[:, cols] + cw_ref[3:4, cols] * xbc_buf[8:8 + ts, cols]
        acc = acc + cw_ref[2:3, cols] * xbc_buf[7:7 + ts, cols]
        acc = acc + cw_ref[1:2, cols] * xbc_buf[6:6 + ts, cols]
        acc = acc + cw_ref[0:1, cols] * xbc_buf[5:5 + ts, cols]
        xc_buf[:, cols] = _silu(acc)
    last3 = xbc_buf[5 + ts:8 + ts, :]
    xbc_buf[5:8, :] = last3
    conv_ref[0] = last3

    dt = _softplus(dtr + dtb_ref[...])
    da = dt * (-jnp.exp(alog_ref[...]))
    row = lax.broadcasted_iota(I32, (cs, cs), 0)
    col = lax.broadcasted_iota(I32, (cs, cs), 1)
    causal = row >= col
    tril = causal.astype(F32)
    lane = lax.broadcasted_iota(I32, (cs, LANES), 1)
    lo_half = lane < hp

    for c in range(ts // cs):
        rows = slice(c * cs, (c + 1) * cs)
        da_c = da[rows]
        dt_c = dt[rows]
        acum = _fdot(tril, da_c)
        acum_t = acum.T
        dt_t = dt_c.T
        a_last = acum[cs - 1:cs, :]
        to_end = jnp.exp(a_last - acum)
        w_all = dt_c * to_end
        ea = jnp.exp(acum)
        cd = jnp.exp(a_last)
        for g in range(N_BC_GROUPS):
            b_g = xc_buf[rows, d_inner + g * nst:d_inner + (g + 1) * nst]
            c_g = xc_buf[rows, d_inner + (N_BC_GROUPS + g) * nst:d_inner + (N_BC_GROUPS + g + 1) * nst]
            cb = _bdot_nt(c_g, b_g)
            xw_parts = []
            for pr in range(hpg // 2):
                h0 = g * hpg + 2 * pr
                lanes0 = slice(h0 * hp, h0 * hp + 2 * hp)
                x_pair = xc_buf[rows, lanes0]
                st_pair = st_buf[:, lanes0]
                y_pair = jnp.zeros((cs, 2 * hp), F32)
                w_pair = jnp.zeros((cs, 2 * hp), F32)
                for k in range(2):
                    hh = h0 + k
                    seg = acum[:, hh:hh + 1] - acum_t[hh:hh + 1, :]
                    dec = jnp.exp(jnp.where(causal, seg, -jnp.inf))
                    m = cb * dec * dt_t[hh:hh + 1, :]
                    sel = lo_half if k == 0 else jnp.logical_not(lo_half)
                    x_k = jnp.where(sel, x_pair, 0.0)
                    st_k = jnp.where(sel, st_pair, 0.0)
                    lhs = jnp.concatenate([m, c_g * ea[:, hh:hh + 1]], axis=1)
                    rhs = jnp.concatenate([x_k, st_k], axis=0)
                    y_pair = y_pair + _bdot(lhs, rhs)
                    w_pair = jnp.where(sel, w_all[:, hh:hh + 1], w_pair)
                y_buf[rows, lanes0] = y_pair + x_pair * dsk_ref[:, lanes0]
                xw_parts.append(x_pair * w_pair)
            xw = jnp.concatenate(xw_parts, axis=1)
            glanes = slice(g * gw, (g + 1) * gw)
            cd_parts = [jnp.broadcast_to(cd[:, g * hpg + k:g * hpg + k + 1], (1, hp)) for k in range(hpg)]
            cd_g = jnp.concatenate(cd_parts, axis=1)
            st_buf[:, glanes] = st_buf[:, glanes] * cd_g + _bdot_tn(b_g, xw)

    @pl.when(s == pl.num_programs(1) - 1)
    def _():
        for pr in range(n_heads // 2):
            t = st_buf[:, 2 * pr * hp:2 * (pr + 1) * hp].T
            ssm_ref[0, 2 * pr] = t[0:hp]
            ssm_ref[0, 2 * pr + 1] = t[hp:2 * hp]

    acc = x_ref[...]
    for g in range(N_BC_GROUPS):
        glanes = slice(g * gw, (g + 1) * gw)
        z = jnp.dot(h_buf[...], win_ref[:, glanes], preferred_element_type=F32)
        gg = y_buf[:, glanes] * _silu(z)
        gg = gg * lax.rsqrt(jnp.mean(gg * gg, axis=-1, keepdims=True) + EPS) * gnw_ref[:, glanes]
        acc = acc + jnp.dot(gg.astype(BF16), wout_ref[glanes, :], preferred_element_type=F32)
    out_ref[...] = acc


def _mamba_prompt(x, nw, win, cw, cb, dtb, alog, dsk, gnw, wout, *, bsz, seq, ts):
    d = x.shape[1]
    d_inner = wout.shape[0]
    n_heads = d_inner // SSM_HEAD_DIM
    conv_dim = cw.shape[1]
    ns = seq // ts
    body = functools.partial(_mamba_prompt_body, d_inner=d_inner, n_heads=n_heads)
    return pl.pallas_call(
        body,
        grid=(bsz, ns),
        in_specs=[
            pl.BlockSpec((ts, d), lambda b, s: (b * ns + s, 0)),
            _full(nw.shape), _full(win.shape), _full(cw.shape), _full(cb.shape), _full(dtb.shape),
            _full(alog.shape), _full(dsk.shape), _full(gnw.shape), _full(wout.shape),
        ],
        out_specs=[
            pl.BlockSpec((ts, d), lambda b, s: (b * ns + s, 0)),
            pl.BlockSpec((1, CONV_W - 1, conv_dim), lambda b, s: (b, 0, 0)),
            pl.BlockSpec((1, n_heads, SSM_HEAD_DIM, D_STATE), lambda b, s: (b, 0, 0, 0)),
        ],
        out_shape=[
            jax.ShapeDtypeStruct((bsz * seq, d), F32),
            jax.ShapeDtypeStruct((bsz, CONV_W - 1, conv_dim), F32),
            jax.ShapeDtypeStruct((bsz, n_heads, SSM_HEAD_DIM, D_STATE), F32),
        ],
        scratch_shapes=[
            pltpu.VMEM((ts, d), BF16),
            pltpu.VMEM((8 + ts, conv_dim), F32),
            pltpu.VMEM((ts, conv_dim), F32),
            pltpu.VMEM((D_STATE, d_inner), F32),
            pltpu.VMEM((ts, d_inner), F32),
        ],
        compiler_params=_cparams(("arbitrary", "arbitrary")),
        name="mamba_prompt",
    )(x, nw, win, cw, cb, dtb, alog, dsk, gnw, wout)


def _mamba_weights(w_in, conv_w, conv_b, dt_bias, a_log, d_skip, norm_w, w_out):
    d_inner = w_out.shape[0]
    n_heads = dt_bias.shape[0]
    pad = LANES - n_heads
    win = jnp.pad(w_in, ((0, 0), (0, pad))).astype(BF16)
    dtb = jnp.pad(dt_bias, (0, pad)).reshape(1, LANES)
    alog = jnp.pad(a_log, (0, pad)).reshape(1, LANES)
    dsk = jnp.repeat(d_skip, SSM_HEAD_DIM).reshape(1, d_inner)
    return (win, conv_w, conv_b.reshape(1, -1), dtb, alog, dsk, norm_w.reshape(1, d_inner),
            w_out.astype(BF16))


def _sink_softmax_pv(s, sink, v):
    m = jnp.maximum(jnp.max(s, axis=-1, keepdims=True), sink)
    p = jnp.exp(s - m)
    denom = jnp.sum(p, axis=-1, keepdims=True) + jnp.exp(sink - m)
    return _bdot(p, v) / denom


def _attn_prompt_body(sink_ref, x_ref, nw_ref, wqkv_ref, bqkv_ref, wo_ref, bo_ref,
                      out_ref, kwin_ref, vwin_ref, kv_buf, o_buf):
    blk = WINDOW
    hd = HEAD_DIM
    nq = N_Q_HEADS * hd
    nk = N_KV_HEADS * hd
    qpk = N_Q_HEADS // N_KV_HEADS
    s_id = pl.program_id(1)

    @pl.when(s_id == 0)
    def _():
        kv_buf[0:blk, :] = jnp.zeros((blk, 2 * nk), F32)

    x = x_ref[...]
    h = _rms(x, nw_ref[...]).astype(BF16)
    q = jnp.dot(h, wqkv_ref[:, 0:nq], preferred_element_type=F32) + bqkv_ref[:, 0:nq]
    kv = jnp.dot(h, wqkv_ref[:, nq:], preferred_element_type=F32) + bqkv_ref[:, nq:]
    kv_buf[blk:2 * blk, :] = kv
    kwin_ref[0] = kv[:, 0:nk]
    vwin_ref[0] = kv[:, nk:]

    row = lax.broadcasted_iota(I32, (blk, 2 * blk), 0)
    col = lax.broadcasted_iota(I32, (blk, 2 * blk), 1)
    diff = row + blk - col
    ok = (diff >= 0) & (diff <= WINDOW) & ((col >= blk) | (s_id > 0))
    scale = hd ** -0.5
    for g in range(N_KV_HEADS):
        k_g = kv_buf[:, g * hd:(g + 1) * hd]
        v_g = kv_buf[:, nk + g * hd:nk + (g + 1) * hd]
        for j in range(qpk):
            hh = g * qpk + j
            s = _bdot_nt(q[:, hh * hd:(hh + 1) * hd], k_g) * scale
            s = jnp.where(ok, s, -jnp.inf)
            o_buf[:, hh * hd:(hh + 1) * hd] = _sink_softmax_pv(s, sink_ref[hh], v_g)
    kv_buf[0:blk, :] = kv
    out_ref[...] = x + jnp.dot(o_buf[...].astype(BF16), wo_ref[...], preferred_element_type=F32) + bo_ref[...]


def _attn_prompt(x, sinks, nw, wqkv, bqkv, wo, bo, *, bsz, seq):
    d = x.shape[1]
    blk = WINDOW
    nb = seq // blk
    nk = N_KV_HEADS * HEAD_DIM
    return pl.pallas_call(
        _attn_prompt_body,
        grid=(bsz, nb),
        in_specs=[
            pl.BlockSpec(memory_space=pltpu.SMEM),
            pl.BlockSpec((blk, d), lambda b, s: (b * nb + s, 0)),
            _full(nw.shape), _full(wqkv.shape), _full(bqkv.shape), _full(wo.shape), _full(bo.shape),
        ],
        out_specs=[
            pl.BlockSpec((blk, d), lambda b, s: (b * nb + s, 0)),
            pl.BlockSpec((1, blk, nk), lambda b, s: (b, 0, 0)),
            pl.BlockSpec((1, blk, nk), lambda b, s: (b, 0, 0)),
        ],
        out_shape=[
            jax.ShapeDtypeStruct((bsz * seq, d), F32),
            jax.ShapeDtypeStruct((bsz, blk, nk), F32),
            jax.ShapeDtypeStruct((bsz, blk, nk), F32),
        ],
        scratch_shapes=[
            pltpu.VMEM((2 * blk, 2 * nk), F32),
            pltpu.VMEM((blk, N_Q_HEADS * HEAD_DIM), F32),
        ],
        compiler_params=_cparams(("arbitrary", "arbitrary")),
        name="attn_prompt",
    )(sinks, x, nw, wqkv, bqkv, wo, bo)


def _route_body(x_ref, nw_ref, wr_ref, br_ref, info_ref, cnt_ref):
    tm = x_ref.shape[0]
    h = _rms(x_ref[...], nw_ref[...])
    logits = _fdot(h, wr_ref[...]) + br_ref[...]
    lane_i = lax.broadcasted_iota(I32, (tm, LANES), 1)
    lane = lane_i.astype(F32)
    lane_grp = (lane_i // EXPERTS_PER_GROUP).astype(F32)
    big = float(LANES)
    ninf = -jnp.inf

    def first_argmax(v):
        m = jnp.max(v, axis=-1, keepdims=True)
        return m, jnp.min(jnp.where(v == m, lane, big), axis=-1, keepdims=True)

    gmask = (lane_i >= N_EXPERTS) & (lane_i < N_EXPERTS + N_EXPERT_GROUPS)
    gl = jnp.where(gmask, logits, ninf)
    gmax, gi = first_argmax(gl)
    gi = gi - float(N_EXPERTS)
    pg = 1.0 / jnp.sum(jnp.exp(gl - gmax), axis=-1, keepdims=True)
    emask = (lane_i < N_EXPERTS) & (lane_grp == gi)
    el = jnp.where(emask, logits, ninf)
    m1, i1 = first_argmax(el)
    el2 = jnp.where(lane == i1, ninf, el)
    m2, i2 = first_argmax(el2)
    den = jnp.sum(jnp.exp(el - m1), axis=-1, keepdims=True)
    tp1 = 1.0 / den
    tp2 = jnp.exp(m2 - m1) / den
    g1 = pg * tp1 / (tp1 + tp2)
    g2 = pg * tp2 / (tp1 + tp2)
    hot1 = lane == i1
    hot2 = lane == i2
    onehot = jnp.where(hot1 | hot2, 1.0, 0.0)
    rr = lax.broadcasted_iota(I32, (tm, tm), 0)
    cc = lax.broadcasted_iota(I32, (tm, tm), 1)
    before = jnp.where(rr > cc, 1.0, 0.0)
    cum = _bdot(before, onehot)
    r1 = jnp.sum(jnp.where(hot1, cum, 0.0), axis=-1, keepdims=True)
    r2 = jnp.sum(jnp.where(hot2, cum, 0.0), axis=-1, keepdims=True)
    info = jnp.zeros((tm, LANES), F32)
    for k, v in enumerate((g1, g2, i1, i2, r1, r2)):
        info = jnp.where(lane_i == k, v, info)
    info_ref[...] = info
    cnt_ref[0] = jnp.broadcast_to(jnp.sum(onehot, axis=0, keepdims=True), (8, LANES))


def _route(x, nw, wr, br, *, tm):
    t, d = x.shape
    nt = t // tm
    return pl.pallas_call(
        _route_body,
        grid=(nt,),
        in_specs=[pl.BlockSpec((tm, d), lambda i: (i, 0)), _full(nw.shape), _full(wr.shape), _full(br.shape)],
        out_specs=[pl.BlockSpec((tm, LANES), lambda i: (i, 0)), pl.BlockSpec((1, 8, LANES), lambda i: (i, 0, 0))],
        out_shape=[jax.ShapeDtypeStruct((t, LANES), F32), jax.ShapeDtypeStruct((nt, 8, LANES), F32)],
        compiler_params=_cparams(("arbitrary",)),
        name="moe_route",
    )(x, nw, wr, br)


def _row_copy(src, i, dst, j, sem):
    return pltpu.make_async_copy(src.at[pl.ds(i, 1), :], dst.at[pl.ds(j, 1), :], sem)


def _dispatch_body(dest_ref, x_ref, nw_ref, xb_in_ref, xb_ref, h_buf, sem):
    del xb_in_ref
    tm = x_ref.shape[0]
    h_buf[...] = _rms(x_ref[...], nw_ref[...])

    def issue(a, carry):
        _row_copy(h_buf, a // TOP_K, xb_ref, dest_ref[0, 0, a], sem).start()
        return carry

    lax.fori_loop(0, TOP_K * tm, issue, 0)

    def drain(a, carry):
        _row_copy(h_buf, 0, xb_ref, 0, sem).wait()
        return carry

    lax.fori_loop(0, TOP_K * tm, drain, 0)


def _dispatch(x, nw, dest, xb_zero, *, tm):
    t, d = x.shape
    nt = t // tm
    return pl.pallas_call(
        _dispatch_body,
        grid=(nt,),
        in_specs=[
            pl.BlockSpec((1, 1, TOP_K * tm), lambda i: (i, 0, 0), memory_space=pltpu.SMEM),
            pl.BlockSpec((tm, d), lambda i: (i, 0)),
            _full(nw.shape),
            pl.BlockSpec(memory_space=pl.ANY),
        ],
        out_specs=pl.BlockSpec(memory_space=pl.ANY),
        out_shape=jax.ShapeDtypeStruct(xb_zero.shape, F32),
        scratch_shapes=[pltpu.VMEM((tm, d), F32), pltpu.SemaphoreType.DMA(())],
        input_output_aliases={3: 0},
        compiler_params=_cparams(("arbitrary",)),
        name="moe_dispatch",
    )(dest, x, nw, xb_zero)


def _expert_body(be_ref, nu_ref, xb_ref, wg_ref, wu_ref, wd_ref, yb_ref, wg_buf, wu_buf, wd_buf):
    b = pl.program_id(0)
    prev = be_ref[jnp.maximum(b - 1, 0)]
    fresh = (b == 0) | (be_ref[b] != prev)

    @pl.when((b < nu_ref[0]) & fresh)
    def _():
        wg_buf[...] = wg_ref[0, 0].astype(BF16)
        wu_buf[...] = wu_ref[0, 0].astype(BF16)
        wd_buf[...] = wd_ref[0, 0].astype(BF16)

    @pl.when(b < nu_ref[0])
    def _():
        xb = xb_ref[...].astype(BF16)
        gate = jnp.dot(xb, wg_buf[...], preferred_element_type=F32)
        up = jnp.dot(xb, wu_buf[...], preferred_element_type=F32)
        hid = (_silu(gate) * up).astype(BF16)
        yb_ref[...] = jnp.dot(hid, wd_buf[...], preferred_element_type=F32)

    @pl.when(b >= nu_ref[0])
    def _():
        yb_ref[...] = jnp.zeros_like(yb_ref)


def _experts(blk_exp, n_used, xb, wg, wu, wd, *, layer):
    nslot, d = xb.shape
    nb = nslot // MOE_BLOCK
    f = wg.shape[3]

    def xmap(b, be, nu):
        return (jnp.minimum(b, nu[0] - 1), 0)

    def wmap(b, be, nu):
        return (layer, be[b], 0, 0)

    return pl.pallas_call(
        _expert_body,
        grid_spec=pltpu.PrefetchScalarGridSpec(
            num_scalar_prefetch=2,
            grid=(nb,),
            in_specs=[
                pl.BlockSpec((MOE_BLOCK, d), xmap),
                pl.BlockSpec((1, 1, d, f), wmap), pl.BlockSpec((1, 1, d, f), wmap),
                pl.BlockSpec((1, 1, f, d), wmap),
            ],
            out_specs=pl.BlockSpec((MOE_BLOCK, d), lambda b, be, nu: (b, 0)),
            scratch_shapes=[pltpu.VMEM((d, f), BF16), pltpu.VMEM((d, f), BF16), pltpu.VMEM((f, d), BF16)],
        ),
        out_shape=jax.ShapeDtypeStruct((nslot, d), F32),
        compiler_params=_cparams(("arbitrary",)),
        name="moe_experts",
    )(blk_exp, n_used, xb, wg, wu, wd)


def _combine_body(dest_ref, x_ref, info_ref, fw_ref, yb_ref, out_ref, y_buf, sem, *, final_norm):
    tm = x_ref.shape[0]

    def issue(a, carry):
        _row_copy(yb_ref, dest_ref[0, 0, a], y_buf, a, sem).start()
        return carry

    lax.fori_loop(0, TOP_K * tm, issue, 0)

    def drain(a, carry):
        _row_copy(yb_ref, 0, y_buf, 0, sem).wait()
        return carry

    lax.fori_loop(0, TOP_K * tm, drain, 0)
    info = info_ref[...]
    out = x_ref[...]
    for k in range(TOP_K):
        out = out + info[:, k:k + 1] * y_buf[k * tm:(k + 1) * tm, :]
    if final_norm:
        out = _rms(out, fw_ref[...])
    out_ref[...] = out


def _combine(x, info, dest_kmajor, yb, fw, *, tm, final_norm):
    t, d = x.shape
    nt = t // tm
    return pl.pallas_call(
        functools.partial(_combine_body, final_norm=final_norm),
        grid=(nt,),
        in_specs=[
            pl.BlockSpec((1, 1, TOP_K * tm), lambda i: (i, 0, 0), memory_space=pltpu.SMEM),
            pl.BlockSpec((tm, d), lambda i: (i, 0)),
            pl.BlockSpec((tm, LANES), lambda i: (i, 0)),
            _full(fw.shape),
            pl.BlockSpec(memory_space=pl.ANY),
        ],
        out_specs=pl.BlockSpec((tm, d), lambda i: (i, 0)),
        out_shape=jax.ShapeDtypeStruct((t, d), F32),
        scratch_shapes=[pltpu.VMEM((TOP_K * tm, d), F32), pltpu.SemaphoreType.DMA(())],
        compiler_params=_cparams(("arbitrary",)),
        name="moe_combine",
    )(dest_kmajor, x, info, fw, yb)


def _moe(x, nw, w_group, b_group, w_expert, b_expert, wg, wu, wd, fw, *, layer, final_norm):
    t, d = x.shape
    tm = min(256, t)
    nt = t // tm
    pad = LANES - N_EXPERTS - N_EXPERT_GROUPS
    wr = jnp.pad(jnp.concatenate([w_expert, w_group], axis=1), ((0, 0), (0, pad)))
    br = jnp.pad(jnp.concatenate([b_expert, b_group]), (0, pad)).reshape(1, LANES)
    nw2 = nw.reshape(1, d)
    info, cnt = _route(x, nw2, wr, br, tm=tm)

    eid = info[:, 2:2 + TOP_K].astype(I32)
    rank = info[:, 4:4 + TOP_K].astype(I32)
    cnt = cnt[:, 0, :N_EXPERTS].astype(I32)
    base = jnp.cumsum(cnt, axis=0) - cnt
    total = jnp.sum(cnt, axis=0)
    padded = (total + MOE_BLOCK - 1) // MOE_BLOCK * MOE_BLOCK
    pend = jnp.cumsum(padded)
    off = (pend - padded)[None, :] + base
    off_t = jnp.repeat(off, tm, axis=0)
    hot = eid[:, :, None] == jnp.arange(N_EXPERTS, dtype=I32)[None, None, :]
    dest = jnp.sum(jnp.where(hot, off_t[:, None, :], 0), axis=-1) + rank
    n_blocks = -(-(t * TOP_K + N_EXPERTS * (MOE_BLOCK - 1)) // MOE_BLOCK)
    blk_exp = jnp.minimum(jnp.sum(pend[None, :] <= (jnp.arange(n_blocks, dtype=I32) * MOE_BLOCK)[:, None], axis=1),
                          N_EXPERTS - 1).astype(I32)
    n_used = (pend[-1] // MOE_BLOCK).astype(I32).reshape(1)
    dest_tok = dest.reshape(nt, 1, tm * TOP_K)
    dest_k = dest.reshape(nt, tm, TOP_K).transpose(0, 2, 1).reshape(nt, 1, TOP_K * tm)

    xb = _dispatch(x, nw2, dest_tok, jnp.zeros((n_blocks * MOE_BLOCK, d), F32), tm=tm)
    yb = _experts(blk_exp, n_used, xb, wg, wu, wd, layer=layer)
    return _combine(x, info, dest_k, yb, fw.reshape(1, d), tm=tm, final_norm=final_norm)


def _linear_body(x_ref, nw_ref, w_ref, b_ref, r_ref, out_ref, *, norm):
    x = x_ref[...]
    if norm:
        x = _rms(x, nw_ref[...])
    out_ref[...] = _bdot(x, w_ref[...]) + b_ref[...] + r_ref[...]


def _linear(x, w, *, nw=None, bias=None, res=None, tn):
    m, kd = x.shape
    n = w.shape[1]
    norm = nw is not None
    nw = jnp.ones((1, kd), F32) if nw is None else nw
    bias = jnp.zeros((1, n), F32) if bias is None else bias
    res = jnp.zeros((m, n), F32) if res is None else res
    tn = min(tn, n)
    return pl.pallas_call(
        functools.partial(_linear_body, norm=norm),
        grid=(n // tn,),
        in_specs=[
            _full(x.shape), _full(nw.shape),
            pl.BlockSpec((kd, tn), lambda j: (0, j)),
            pl.BlockSpec((1, tn), lambda j: (0, j)),
            pl.BlockSpec((m, tn), lambda j: (0, j)),
        ],
        out_specs=pl.BlockSpec((m, tn), lambda j: (0, j)),
        out_shape=jax.ShapeDtypeStruct((m, n), F32),
        compiler_params=_cparams(("arbitrary",)),
        name="sample_linear",
    )(x, nw, w, bias, res)


def _sample_conv_body(xbc_ref, st_ref, cw_ref, cb_ref, dtr_ref, dtb_ref, xc_ref, stn_ref, dt_ref, cbg_ref,
                      *, d_inner):
    nst = D_STATE
    xbc = xbc_ref[...]
    acc = cb_ref[...] + cw_ref[3:4, :] * xbc
    for k in range(CONV_W - 1):
        acc = acc + cw_ref[k:k + 1, :] * st_ref[k]
    xc = _silu(acc)
    xc_ref[...] = xc
    stn_ref[0] = st_ref[1]
    stn_ref[1] = st_ref[2]
    stn_ref[2] = xbc
    dt_ref[...] = _softplus(dtr_ref[...] + dtb_ref[...])
    lane = lax.broadcasted_iota(I32, (xbc.shape[0], LANES), 1)
    cbg = jnp.zeros((xbc.shape[0], LANES), F32)
    for g in range(N_BC_GROUPS):
        b_g = xc[:, d_inner + g * nst:d_inner + (g + 1) * nst]
        c_g = xc[:, d_inner + (N_BC_GROUPS + g) * nst:d_inner + (N_BC_GROUPS + g + 1) * nst]
        cbg = jnp.where(lane == g, jnp.sum(b_g * c_g, axis=-1, keepdims=True), cbg)
    cbg_ref[...] = cbg


def _sample_conv(xbc, st_t, cw, cb, dtr, dtb, *, d_inner):
    m, cd = xbc.shape
    return pl.pallas_call(
        functools.partial(_sample_conv_body, d_inner=d_inner),
        grid=(1,),
        in_specs=[_full(xbc.shape), _full(st_t.shape), _full(cw.shape), _full(cb.shape), _full(dtr.shape),
                  _full(dtb.shape)],
        out_specs=[_full((m, cd)), _full(st_t.shape), _full((m, LANES)), _full((m, LANES))],
        out_shape=[jax.ShapeDtypeStruct((m, cd), F32), jax.ShapeDtypeStruct(st_t.shape, F32),
                   jax.ShapeDtypeStruct((m, LANES), F32), jax.ShapeDtypeStruct((m, LANES), F32)],
        compiler_params=_cparams(("arbitrary",)),
        name="sample_conv",
    )(xbc, st_t, cw, cb, dtr, dtb)


def _sample_ssd_body(s0_ref, xt_ref, bc_ref, hs_ref, par_ref, sn_ref, yt_ref, *, n_heads):
    hp = SSM_HEAD_DIM
    hpg = n_heads // N_BC_GROUPS
    xt = xt_ref[0]
    dt = hs_ref[0, 0:1, :]
    cbh = hs_ref[0, 1:2, :]
    a = -jnp.exp(par_ref[0:1, :])
    dsk = par_ref[1:2, :]
    dec = jnp.exp(dt * a)
    xdt = xt * dt
    lane = lax.broadcasted_iota(I32, (hp, n_heads), 1)
    yoff = jnp.zeros((hp, n_heads), F32)
    for hh in range(n_heads):
        g = hh // hpg
        b_row = bc_ref[0, g:g + 1, :]
        c_row = bc_ref[0, N_BC_GROUPS + g:N_BC_GROUPS + g + 1, :]
        s0 = s0_ref[0, hh]
        yo = jnp.sum(s0 * c_row, axis=-1, keepdims=True)
        yoff = jnp.where(lane == hh, yo, yoff)
        sn_ref[0, hh] = s0 * dec[:, hh:hh + 1] + xdt[:, hh:hh + 1] * b_row
    yt_ref[0] = cbh * xdt + yoff * dec + xt * dsk


def _sample_ssd(s0, xt, bc, hs, par):
    bsz, n_heads, hp, nst = s0.shape
    return pl.pallas_call(
        functools.partial(_sample_ssd_body, n_heads=n_heads),
        grid=(bsz,),
        in_specs=[
            pl.BlockSpec((1, n_heads, hp, nst), lambda b: (b, 0, 0, 0)),
            pl.BlockSpec((1, hp, n_heads), lambda b: (b, 0, 0)),
            pl.BlockSpec((1,) + bc.shape[1:], lambda b: (b, 0, 0)),
            pl.BlockSpec((1,) + hs.shape[1:], lambda b: (b, 0, 0)),
            _full(par.shape),
        ],
        out_specs=[
            pl.BlockSpec((1, n_heads, hp, nst), lambda b: (b, 0, 0, 0)),
            pl.BlockSpec((1, hp, n_heads), lambda b: (b, 0, 0)),
        ],
        out_shape=[jax.ShapeDtypeStruct(s0.shape, F32), jax.ShapeDtypeStruct((bsz, hp, n_heads), F32)],
        compiler_params=_cparams(("arbitrary",)),
        name="sample_ssd",
    )(s0, xt, bc, hs, par)


def _sample_gnorm_out_body(y_ref, z_ref, gnw_ref, wout_ref, x_ref, out_ref, *, d_inner):
    gw = d_inner // N_BC_GROUPS
    acc = x_ref[...]
    for g in range(N_BC_GROUPS):
        glanes = slice(g * gw, (g + 1) * gw)
        gg = y_ref[:, glanes] * _silu(z_ref[:, glanes])
        gg = gg * lax.rsqrt(jnp.mean(gg * gg, axis=-1, keepdims=True) + EPS) * gnw_ref[:, glanes]
        acc = acc + jnp.dot(gg.astype(BF16), wout_ref[glanes, :], preferred_element_type=F32)
    out_ref[...] = acc


def _sample_gnorm_out(y, z, gnw, wout, x):
    d_inner = y.shape[1]
    return pl.pallas_call(
        functools.partial(_sample_gnorm_out_body, d_inner=d_inner),
        grid=(1,),
        in_specs=[_full(y.shape), _full(z.shape), _full(gnw.shape), _full(wout.shape), _full(x.shape)],
        out_specs=_full(x.shape),
        out_shape=jax.ShapeDtypeStruct(x.shape, F32),
        compiler_params=_cparams(("arbitrary",)),
        name="sample_gnorm_out",
    )(y, z, gnw, wout, x)


def _sample_attn_body(q_ref, kn_ref, vn_ref, kc_ref, vc_ref, sink_ref, o_ref, ko_ref, vo_ref):
    bt = q_ref.shape[0]
    wb = kc_ref.shape[1]
    hd = HEAD_DIM
    qpk = N_Q_HEADS // N_KV_HEADS
    scale = hd ** -0.5
    for b in range(bt):
        kc = kc_ref[b]
        vc = vc_ref[b]
        kn = kn_ref[b]
        vn = vn_ref[b]
        for g in range(N_KV_HEADS):
            cols = slice(g * hd, (g + 1) * hd)
            q_g = q_ref[b, g * qpk:(g + 1) * qpk, :]
            sink = sink_ref[g * qpk:(g + 1) * qpk, :]
            s = _bdot_nt(q_g, kc[:, cols]) * scale
            s_new = jnp.sum(q_g * kn[:, cols], axis=-1, keepdims=True) * scale
            m = jnp.maximum(jnp.maximum(jnp.max(s, axis=-1, keepdims=True), s_new), sink)
            p = jnp.exp(s - m)
            p_new = jnp.exp(s_new - m)
            denom = jnp.sum(p, axis=-1, keepdims=True) + p_new + jnp.exp(sink - m)
            pv = _bdot(p, vc[:, cols]) + p_new * vn[:, cols]
            o_ref[b, g * qpk:(g + 1) * qpk, :] = pv / denom
        ko_ref[b, 0:wb - 1, :] = kc[1:wb, :]
        ko_ref[b, wb - 1:wb, :] = kn
        vo_ref[b, 0:wb - 1, :] = vc[1:wb, :]
        vo_ref[b, wb - 1:wb, :] = vn


def _sample_attn(q3, kn, vn, kc, vc, sinks, *, bt):
    bsz, nqh, hd = q3.shape
    wb, nk = kc.shape[1], kc.shape[2]
    return pl.pallas_call(
        _sample_attn_body,
        grid=(bsz // bt,),
        in_specs=[
            pl.BlockSpec((bt, nqh, hd), lambda i: (i, 0, 0)),
            pl.BlockSpec((bt, 1, nk), lambda i: (i, 0, 0)),
            pl.BlockSpec((bt, 1, nk), lambda i: (i, 0, 0)),
            pl.BlockSpec((bt, wb, nk), lambda i: (i, 0, 0)),
            pl.BlockSpec((bt, wb, nk), lambda i: (i, 0, 0)),
            _full(sinks.shape),
        ],
        out_specs=[
            pl.BlockSpec((bt, nqh, hd), lambda i: (i, 0, 0)),
            pl.BlockSpec((bt, wb, nk), lambda i: (i, 0, 0)),
            pl.BlockSpec((bt, wb, nk), lambda i: (i, 0, 0)),
        ],
        out_shape=[jax.ShapeDtypeStruct(q3.shape, F32), jax.ShapeDtypeStruct(kc.shape, F32),
                   jax.ShapeDtypeStruct(vc.shape, F32)],
        compiler_params=_cparams(("arbitrary",)),
        name="sample_attn",
    )(q3, kn, vn, kc, vc, sinks)


def _mamba_sample(x, nw, mw, state_conv, state_ssm):
    win, cw, cb, dtb, alog, dsk, gnw, wout = mw
    bsz, d = x.shape
    d_inner = wout.shape[0]
    conv_dim = cw.shape[1]
    n_heads = d_inner // SSM_HEAD_DIM
    hp = SSM_HEAD_DIM
    proj = _linear(x, win, nw=nw, tn=896)
    z = proj[:, :d_inner]
    xbc = proj[:, d_inner:d_inner + conv_dim]
    dtr = proj[:, d_inner + conv_dim:]
    st_t = jnp.transpose(state_conv, (1, 0, 2))
    xc, stn_t, dt, cbg = _sample_conv(xbc, st_t, cw, cb, dtr, dtb, d_inner=d_inner)
    conv_new = jnp.transpose(stn_t, (1, 0, 2))
    xt = jnp.transpose(xc[:, :d_inner].reshape(bsz, n_heads, hp), (0, 2, 1))
    bc = xc[:, d_inner:].reshape(bsz, 2 * N_BC_GROUPS, D_STATE)
    cbh = jnp.repeat(cbg[:, :N_BC_GROUPS], n_heads // N_BC_GROUPS, axis=1)
    hs = jnp.stack([dt[:, :n_heads], cbh], axis=1)
    par = jnp.stack([alog[0, :n_heads], dsk.reshape(n_heads, hp)[:, 0]], axis=0)
    ssm_new, yt = _sample_ssd(state_ssm, xt, bc, hs, par)
    y = jnp.transpose(yt, (0, 2, 1)).reshape(bsz, d_inner)
    out = _sample_gnorm_out(y, z, gnw, wout, x)
    return out, conv_new, ssm_new


def _attn_sample(x, nw, wqkv, bqkv, sinks, wo, bo, cache_k, cache_v):
    bsz, d = x.shape
    wb = cache_k.shape[1]
    nq = N_Q_HEADS * HEAD_DIM
    nk = N_KV_HEADS * HEAD_DIM
    qkv = _linear(x, wqkv, nw=nw, bias=bqkv, tn=512)
    q3 = qkv[:, :nq].reshape(bsz, N_Q_HEADS, HEAD_DIM)
    kn = qkv[:, nq:nq + nk].reshape(bsz, 1, nk)
    vn = qkv[:, nq + nk:].reshape(bsz, 1, nk)
    o3, ko, vo = _sample_attn(q3, kn, vn, cache_k.reshape(bsz, wb, nk), cache_v.reshape(bsz, wb, nk),
                              sinks.reshape(N_Q_HEADS, 1), bt=8)
    out = _linear(o3.reshape(bsz, nq), wo, bias=bo, res=x, tn=512)
    return out, ko.reshape(cache_k.shape), vo.reshape(cache_v.shape)


def kernel(x_prompt, x_sample, state_ssm, state_conv, cache_k_win, cache_v_win,
           mamba_w_in, mamba_conv_w, mamba_conv_b, mamba_dt_bias, mamba_a_log, mamba_d,
           mamba_norm_w, mamba_w_out, attn_w_qkv, attn_b_qkv, attn_sinks, attn_w_o, attn_b_o,
           norm_mix, norm_ffn, router_w_group, router_b_group, router_w_expert, router_b_expert,
           expert_w_gate, expert_w_up, expert_w_down, norm_final):
    bsz, seq, d = x_prompt.shape
    dbsz, dseq, _ = x_sample.shape
    assert dseq == 1 and cache_k_win.shape[2] <= WINDOW and seq % WINDOW == 0
    depth = norm_mix.shape[0]
    xp = x_prompt.reshape(bsz * seq, d)
    xs = x_sample.reshape(dbsz, d)
    ssm_p, conv_p, kp_l, vp_l = [], [], [], []
    ssm_s, conv_s, ks_l, vs_l = [], [], [], []
    for i in range(depth):
        j = i // 2
        nw = norm_mix[i].reshape(1, d)
        if i % 2 == 0:
            mw = _mamba_weights(mamba_w_in[j], mamba_conv_w[j], mamba_conv_b[j], mamba_dt_bias[j],
                                mamba_a_log[j], mamba_d[j], mamba_norm_w[j], mamba_w_out[j])
            xp, cp, sp = _mamba_prompt(xp, nw, *mw, bsz=bsz, seq=seq, ts=SSD_CHUNK)
            xs, cs_, ss_ = _mamba_sample(xs, nw, mw, state_conv[j], state_ssm[j])
            ssm_p.append(sp)
            conv_p.append(cp)
            ssm_s.append(ss_)
            conv_s.append(cs_)
        else:
            wqkv = attn_w_qkv[j].astype(BF16)
            bqkv = attn_b_qkv[j].reshape(1, -1)
            wo = attn_w_o[j].astype(BF16)
            bo = attn_b_o[j].reshape(1, d)
            xp, kp, vp = _attn_prompt(xp, attn_sinks[j], nw, wqkv, bqkv, wo, bo, bsz=bsz, seq=seq)
            xs, ks_, vs_ = _attn_sample(xs, nw, wqkv, bqkv, attn_sinks[j], wo, bo, cache_k_win[j], cache_v_win[j])
            kp_l.append(kp.reshape(bsz, WINDOW, N_KV_HEADS, HEAD_DIM))
            vp_l.append(vp.reshape(bsz, WINDOW, N_KV_HEADS, HEAD_DIM))
            ks_l.append(ks_)
            vs_l.append(vs_)
        last = i == depth - 1
        moe_w = (norm_ffn[i], router_w_group[i], router_b_group[i], router_w_expert[i], router_b_expert[i],
                 expert_w_gate, expert_w_up, expert_w_down, norm_final)
        xp = _moe(xp, *moe_w, layer=i, final_norm=last)
        xs = _moe(xs, *moe_w, layer=i, final_norm=last)
    return (xp.reshape(bsz, seq, d), xs.reshape(dbsz, dseq, d),
            jnp.stack(ssm_p), jnp.stack(conv_p), jnp.stack(kp_l), jnp.stack(vp_l),
            jnp.stack(ssm_s), jnp.stack(conv_s), jnp.stack(ks_l), jnp.stack(vs_l))
```

```python
import functools
import math

import jax
import jax.numpy as jnp
from jax import lax
from jax.experimental import pallas as pl
from jax.experimental.pallas import tpu as pltpu

F32 = jnp.float32
BF16 = jnp.bfloat16
I32 = jnp.int32

EPS = 1e-5
LANES = 128
VMEM_LIMIT = 56 * 1024 * 1024

SSM_HEAD_DIM = 64
D_STATE = 128
N_BC_GROUPS = 8
CONV_W = 4
SSD_CHUNK = 128
N_Q_HEADS = 16
N_KV_HEADS = 4
HEAD_DIM = 64
WINDOW = 128
N_EXPERT_GROUPS = 4
EXPERTS_PER_GROUP = 8
N_EXPERTS = N_EXPERT_GROUPS * EXPERTS_PER_GROUP
TOP_K = 2
MOE_BLOCK = 256


def _cparams(sem):
    return pltpu.CompilerParams(dimension_semantics=sem, vmem_limit_bytes=VMEM_LIMIT)


def _full(shape):
    n = len(shape)
    return pl.BlockSpec(shape, lambda *_: (0,) * n)


def _rms(x, w):
    return x * lax.rsqrt(jnp.mean(x * x, axis=-1, keepdims=True) + EPS) * w


def _silu(x):
    return x / (1.0 + jnp.exp(-x))


def _softplus(x):
    return jnp.maximum(x, 0.0) + jnp.log(1.0 + jnp.exp(-jnp.abs(x)))


def _bdot(a, b):
    return jnp.dot(a.astype(BF16), b.astype(BF16), preferred_element_type=F32)


def _bdot_nt(a, b):
    return lax.dot_general(a.astype(BF16), b.astype(BF16), (((1,), (1,)), ((), ())),
                           preferred_element_type=F32)


def _bdot_tn(a, b):
    return lax.dot_general(a.astype(BF16), b.astype(BF16), (((0,), (0,)), ((), ())),
                           preferred_element_type=F32)


def _fdot(a, b):
    return jnp.dot(a, b, preferred_element_type=F32, precision=lax.Precision.HIGHEST)


def _mamba_prompt_body(x_ref, nw_ref, win_ref, cw_ref, cb_ref, dtb_ref, alog_ref, dsk_ref, gnw_ref,
                       wout_ref, out_ref, conv_ref, ssm_ref, h_buf, xbc_buf, xc_buf, st_buf, y_buf,
                       *, d_inner, n_heads):
    ts = x_ref.shape[0]
    cs = SSD_CHUNK
    hp = SSM_HEAD_DIM
    nst = D_STATE
    gw = d_inner // N_BC_GROUPS
    hpg = n_heads // N_BC_GROUPS
    conv_dim = d_inner + 2 * N_BC_GROUPS * nst
    s = pl.program_id(1)

    @pl.when(s == 0)
    def _():
        xbc_buf[0:8, :] = jnp.zeros((8, conv_dim), F32)
        st_buf[...] = jnp.zeros_like(st_buf)

    h_buf[...] = _rms(x_ref[...], nw_ref[...]).astype(BF16)
    dtr = jnp.dot(h_buf[...], win_ref[:, d_inner + conv_dim:], preferred_element_type=F32)
    ct = 512
    for j in range(conv_dim // ct):
        cols = slice(j * ct, (j + 1) * ct)
        xbc_buf[8:8 + ts, cols] = jnp.dot(h_buf[...], win_ref[:, d_inner + j * ct:d_inner + (j + 1) * ct],
                                          preferred_element_type=F32)
        acc = cb_ref[:, cols] + cw_ref[3:4, cols] * xbc_buf[8:8 + ts, cols]
        acc = acc + cw_ref[2:3, cols] * xbc_buf[7:7 + ts, cols]
        acc = acc + cw_ref[1:2, cols] * xbc_buf[6:6 + ts, cols]
        acc = acc + cw_ref[0:1, cols] * xbc_buf[5:5 + ts, cols]
        xc_buf[:, cols] = _silu(acc)
    last3 = xbc_buf[5 + ts:8 + ts, :]
    xbc_buf[5:8, :] = last3
    conv_ref[0] = last3

    dt = _softplus(dtr + dtb_ref[...])
    da = dt * (-jnp.exp(alog_ref[...]))
    row = lax.broadcasted_iota(I32, (cs, cs), 0)
    col = lax.broadcasted_iota(I32, (cs, cs), 1)
    causal = row >= col
    tril = causal.astype(F32)
    lane = lax.broadcasted_iota(I32, (cs, LANES), 1)
    lo_half = lane < hp

    for c in range(ts // cs):
        rows = slice(c * cs, (c + 1) * cs)
        da_c = da[rows]
        dt_c = dt[rows]
        acum = _fdot(tril, da_c)
        acum_t = acum.T
        dt_t = dt_c.T
        a_last = acum[cs - 1:cs, :]
        to_end = jnp.exp(a_last - acum)
        w_all = dt_c * to_end
        ea = jnp.exp(acum)
        cd = jnp.exp(a_last)
        for g in range(N_BC_GROUPS):
            b_g = xc_buf[rows, d_inner + g * nst:d_inner + (g + 1) * nst]
            c_g = xc_buf[rows, d_inner + (N_BC_GROUPS + g) * nst:d_inner + (N_BC_GROUPS + g + 1) * nst]
            cb = _bdot_nt(c_g, b_g)
            xw_parts = []
            for pr in range(hpg // 2):
                h0 = g * hpg + 2 * pr
                lanes0 = slice(h0 * hp, h0 * hp + 2 * hp)
                x_pair = xc_buf[rows, lanes0]
                st_pair = st_buf[:, lanes0]
                y_pair = jnp.zeros((cs, 2 * hp), F32)
                w_pair = jnp.zeros((cs, 2 * hp), F32)
                for k in range(2):
                    hh = h0 + k
                    seg = acum[:, hh:hh + 1] - acum_t[hh:hh + 1, :]
                    dec = jnp.exp(jnp.where(causal, seg, -jnp.inf))
                    m = cb * dec * dt_t[hh:hh + 1, :]
                    sel = lo_half if k == 0 else jnp.logical_not(lo_half)
                    x_k = jnp.where(sel, x_pair, 0.0)
                    st_k = jnp.where(sel, st_pair, 0.0)
                    lhs = jnp.concatenate([m, c_g * ea[:, hh:hh + 1]], axis=1)
                    rhs = jnp.concatenate([x_k, st_k], axis=0)
                    y_pair = y_pair + _bdot(lhs, rhs)
                    w_pair = jnp.where(sel, w_all[:, hh:hh + 1], w_pair)
                y_buf[rows, lanes0] = y_pair + x_pair * dsk_ref[:, lanes0]
                xw_parts.append(x_pair * w_pair)
            xw = jnp.concatenate(xw_parts, axis=1)
            glanes = slice(g * gw, (g + 1) * gw)
            cd_parts = [jnp.broadcast_to(cd[:, g * hpg + k:g * hpg + k + 1], (1, hp)) for k in range(hpg)]
            cd_g = jnp.concatenate(cd_parts, axis=1)
            st_buf[:, glanes] = st_buf[:, glanes] * cd_g + _bdot_tn(b_g, xw)

    @pl.when(s == pl.num_programs(1) - 1)
    def _():
        for pr in range(n_heads // 2):
            t = st_buf[:, 2 * pr * hp:2 * (pr + 1) * hp].T
            ssm_ref[0, 2 * pr] = t[0:hp]
            ssm_ref[0, 2 * pr + 1] = t[hp:2 * hp]

    acc = x_ref[...]
    for g in range(N_BC_GROUPS):
        glanes = slice(g * gw, (g + 1) * gw)
        z = jnp.dot(h_buf[...], win_ref[:, glanes], preferred_element_type=F32)
        gg = y_buf[:, glanes] * _silu(z)
        gg = gg * lax.rsqrt(jnp.mean(gg * gg, axis=-1, keepdims=True) + EPS) * gnw_ref[:, glanes]
        acc = acc + jnp.dot(gg.astype(BF16), wout_ref[glanes, :], preferred_element_type=F32)
    out_ref[...] = acc


def _mamba_prompt(x, nw, win, cw, cb, dtb, alog, dsk, gnw, wout, *, bsz, seq, ts):
    d = x.shape[1]
    d_inner = wout.shape[0]
    n_heads = d_inner // SSM_HEAD_DIM
    conv_dim = cw.shape[1]
    ns = seq // ts
    body = functools.partial(_mamba_prompt_body, d_inner=d_inner, n_heads=n_heads)
    return pl.pallas_call(
        body,
        grid=(bsz, ns),
        in_specs=[
            pl.BlockSpec((ts, d), lambda b, s: (b * ns + s, 0)),
            _full(nw.shape), _full(win.shape), _full(cw.shape), _full(cb.shape), _full(dtb.shape),
            _full(alog.shape), _full(dsk.shape), _full(gnw.shape), _full(wout.shape),
        ],
        out_specs=[
            pl.BlockSpec((ts, d), lambda b, s: (b * ns + s, 0)),
            pl.BlockSpec((1, CONV_W - 1, conv_dim), lambda b, s: (b, 0, 0)),
            pl.BlockSpec((1, n_heads, SSM_HEAD_DIM, D_STATE), lambda b, s: (b, 0, 0, 0)),
        ],
        out_shape=[
            jax.ShapeDtypeStruct((bsz * seq, d), F32),
            jax.ShapeDtypeStruct((bsz, CONV_W - 1, conv_dim), F32),
            jax.ShapeDtypeStruct((bsz, n_heads, SSM_HEAD_DIM, D_STATE), F32),
        ],
        scratch_shapes=[
            pltpu.VMEM((ts, d), BF16),
            pltpu.VMEM((8 + ts, conv_dim), F32),
            pltpu.VMEM((ts, conv_dim), F32),
            pltpu.VMEM((D_STATE, d_inner), F32),
            pltpu.VMEM((ts, d_inner), F32),
        ],
        compiler_params=_cparams(("arbitrary", "arbitrary")),
        name="mamba_prompt",
    )(x, nw, win, cw, cb, dtb, alog, dsk, gnw, wout)


def _mamba_weights(w_in, conv_w, conv_b, dt_bias, a_log, d_skip, norm_w, w_out):
    d_inner = w_out.shape[0]
    n_heads = dt_bias.shape[0]
    pad = LANES - n_heads
    win = jnp.pad(w_in, ((0, 0), (0, pad))).astype(BF16)
    dtb = jnp.pad(dt_bias, (0, pad)).reshape(1, LANES)
    alog = jnp.pad(a_log, (0, pad)).reshape(1, LANES)
    dsk = jnp.repeat(d_skip, SSM_HEAD_DIM).reshape(1, d_inner)
    return (win, conv_w, conv_b.reshape(1, -1), dtb, alog, dsk, norm_w.reshape(1, d_inner),
            w_out.astype(BF16))


def _sink_softmax_pv(s, sink, v):
    m = jnp.maximum(jnp.max(s, axis=-1, keepdims=True), sink)
    p = jnp.exp(s - m)
    denom = jnp.sum(p, axis=-1, keepdims=True) + jnp.exp(sink - m)
    return _bdot(p, v) / denom


def _attn_prompt_body(sink_ref, x_ref, nw_ref, wqkv_ref, bqkv_ref, wo_ref, bo_ref,
                      out_ref, kwin_ref, vwin_ref, kv_buf, o_buf):
    blk = WINDOW
    hd = HEAD_DIM
    nq = N_Q_HEADS * hd
    nk = N_KV_HEADS * hd
    qpk = N_Q_HEADS // N_KV_HEADS
    s_id = pl.program_id(1)

    @pl.when(s_id == 0)
    def _():
        kv_buf[0:blk, :] = jnp.zeros((blk, 2 * nk), F32)

    x = x_ref[...]
    h = _rms(x, nw_ref[...]).astype(BF16)
    q = jnp.dot(h, wqkv_ref[:, 0:nq], preferred_element_type=F32) + bqkv_ref[:, 0:nq]
    kv = jnp.dot(h, wqkv_ref[:, nq:], preferred_element_type=F32) + bqkv_ref[:, nq:]
    kv_buf[blk:2 * blk, :] = kv
    kwin_ref[0] = kv[:, 0:nk]
    vwin_ref[0] = kv[:, nk:]

    row = lax.broadcasted_iota(I32, (blk, 2 * blk), 0)
    col = lax.broadcasted_iota(I32, (blk, 2 * blk), 1)
    diff = row + blk - col
    ok = (diff >= 0) & (diff <= WINDOW) & ((col >= blk) | (s_id > 0))
    scale = hd ** -0.5
    for g in range(N_KV_HEADS):
        k_g = kv_buf[:, g * hd:(g + 1) * hd]
        v_g = kv_buf[:, nk + g * hd:nk + (g + 1) * hd]
        for j in range(qpk):
            hh = g * qpk + j
            s = _bdot_nt(q[:, hh * hd:(hh + 1) * hd], k_g) * scale
            s = jnp.where(ok, s, -jnp.inf)
            o_buf[:, hh * hd:(hh + 1) * hd] = _sink_softmax_pv(s, sink_ref[hh], v_g)
    kv_buf[0:blk, :] = kv
    out_ref[...] = x + jnp.dot(o_buf[...].astype(BF16), wo_ref[...], preferred_element_type=F32) + bo_ref[...]


def _attn_prompt(x, sinks, nw, wqkv, bqkv, wo, bo, *, bsz, seq):
    d = x.shape[1]
    blk = WINDOW
    nb = seq // blk
    nk = N_KV_HEADS * HEAD_DIM
    return pl.pallas_call(
        _attn_prompt_body,
        grid=(bsz, nb),
        in_specs=[
            pl.BlockSpec(memory_space=pltpu.SMEM),
            pl.BlockSpec((blk, d), lambda b, s: (b * nb + s, 0)),
            _full(nw.shape), _full(wqkv.shape), _full(bqkv.shape), _full(wo.shape), _full(bo.shape),
        ],
        out_specs=[
            pl.BlockSpec((blk, d), lambda b, s: (b * nb + s, 0)),
            pl.BlockSpec((1, blk, nk), lambda b, s: (b, 0, 0)),
            pl.BlockSpec((1, blk, nk), lambda b, s: (b, 0, 0)),
        ],
        out_shape=[
            jax.ShapeDtypeStruct((bsz * seq, d), F32),
            jax.ShapeDtypeStruct((bsz, blk, nk), F32),
            jax.ShapeDtypeStruct((bsz, blk, nk), F32),
        ],
        scratch_shapes=[
            pltpu.VMEM((2 * blk, 2 * nk), F32),
            pltpu.VMEM((blk, N_Q_HEADS * HEAD_DIM), F32),
        ],
        compiler_params=_cparams(("arbitrary", "arbitrary")),
        name="attn_prompt",
    )(sinks, x, nw, wqkv, bqkv, wo, bo)


def _route_body(x_ref, nw_ref, wr_ref, br_ref, info_ref, cnt_ref):
    tm = x_ref.shape[0]
    h = _rms(x_ref[...], nw_ref[...])
    logits = _fdot(h, wr_ref[...]) + br_ref[...]
    lane_i = lax.broadcasted_iota(I32, (tm, LANES), 1)
    lane = lane_i.astype(F32)
    lane_grp = (lane_i // EXPERTS_PER_GROUP).astype(F32)
    big = float(LANES)
    ninf = -jnp.inf

    def first_argmax(v):
        m = jnp.max(v, axis=-1, keepdims=True)
        return m, jnp.min(jnp.where(v == m, lane, big), axis=-1, keepdims=True)

    gmask = (lane_i >= N_EXPERTS) & (lane_i < N_EXPERTS + N_EXPERT_GROUPS)
    gl = jnp.where(gmask, logits, ninf)
    gmax, gi = first_argmax(gl)
    gi = gi - float(N_EXPERTS)
    pg = 1.0 / jnp.sum(jnp.exp(gl - gmax), axis=-1, keepdims=True)
    emask = (lane_i < N_EXPERTS) & (lane_grp == gi)
    el = jnp.where(emask, logits, ninf)
    m1, i1 = first_argmax(el)
    el2 = jnp.where(lane == i1, ninf, el)
    m2, i2 = first_argmax(el2)
    den = jnp.sum(jnp.exp(el - m1), axis=-1, keepdims=True)
    tp1 = 1.0 / den
    tp2 = jnp.exp(m2 - m1) / den
    g1 = pg * tp1 / (tp1 + tp2)
    g2 = pg * tp2 / (tp1 + tp2)
    hot1 = lane == i1
    hot2 = lane == i2
    onehot = jnp.where(hot1 | hot2, 1.0, 0.0)
    rr = lax.broadcasted_iota(I32, (tm, tm), 0)
    cc = lax.broadcasted_iota(I32, (tm, tm), 1)
    before = jnp.where(rr > cc, 1.0, 0.0)
    cum = _bdot(before, onehot)
    r1 = jnp.sum(jnp.where(hot1, cum, 0.0), axis=-1, keepdims=True)
    r2 = jnp.sum(jnp.where(hot2, cum, 0.0), axis=-1, keepdims=True)
    info = jnp.zeros((tm, LANES), F32)
    for k, v in enumerate((g1, g2, i1, i2, r1, r2)):
        info = jnp.where(lane_i == k, v, info)
    info_ref[...] = info
    cnt_ref[0] = jnp.broadcast_to(jnp.sum(onehot, axis=0, keepdims=True), (8, LANES))


def _route(x, nw, wr, br, *, tm):
    t, d = x.shape
    nt = t // tm
    return pl.pallas_call(
        _route_body,
        grid=(nt,),
        in_specs=[pl.BlockSpec((tm, d), lambda i: (i, 0)), _full(nw.shape), _full(wr.shape), _full(br.shape)],
        out_specs=[pl.BlockSpec((tm, LANES), lambda i: (i, 0)), pl.BlockSpec((1, 8, LANES), lambda i: (i, 0, 0))],
        out_shape=[jax.ShapeDtypeStruct((t, LANES), F32), jax.ShapeDtypeStruct((nt, 8, LANES), F32)],
        compiler_params=_cparams(("arbitrary",)),
        name="moe_route",
    )(x, nw, wr, br)


def _to_tiles(ref, base, val):
    m, rt = val.shape[0], val.shape[1] // LANES
    for j in range(rt):
        ref[pl.ds(base * rt + j, m, stride=rt), :] = val[:, j * LANES:(j + 1) * LANES]


def _from_tiles(ref, base, m, rt):
    return jnp.concatenate([ref[pl.ds(base * rt + j, m, stride=rt), :] for j in range(rt)], axis=1)


def _row_copy(src, i, dst, j, sem, rt):
    return pltpu.make_async_copy(src.at[pl.ds(pl.multiple_of(i * rt, rt), rt), :],
                                 dst.at[pl.ds(pl.multiple_of(j * rt, rt), rt), :], sem)


def _dispatch_body(dest_ref, x_ref, nw_ref, xb_in_ref, xb_ref, h_buf, sem):
    del xb_in_ref
    tm, rt = x_ref.shape[0], x_ref.shape[1] // LANES
    _to_tiles(h_buf, 0, _rms(x_ref[...], nw_ref[...]))

    def issue(a, carry):
        _row_copy(h_buf, a // TOP_K, xb_ref, dest_ref[0, 0, a], sem, rt).start()
        return carry

    lax.fori_loop(0, TOP_K * tm, issue, 0, unroll=8)

    def drain(a, carry):
        _row_copy(h_buf, 0, xb_ref, 0, sem, rt).wait()
        return carry

    lax.fori_loop(0, TOP_K * tm, drain, 0, unroll=8)


def _dispatch(x, nw, dest, xb_zero, *, tm):
    t, d = x.shape
    nt = t // tm
    return pl.pallas_call(
        _dispatch_body,
        grid=(nt,),
        in_specs=[
            pl.BlockSpec((1, 1, TOP_K * tm), lambda i: (i, 0, 0), memory_space=pltpu.SMEM),
            pl.BlockSpec((tm, d), lambda i: (i, 0)),
            _full(nw.shape),
            pl.BlockSpec(memory_space=pl.ANY),
        ],
        out_specs=pl.BlockSpec(memory_space=pl.ANY),
        out_shape=jax.ShapeDtypeStruct(xb_zero.shape, F32),
        scratch_shapes=[pltpu.VMEM((tm * (d // LANES), LANES), F32), pltpu.SemaphoreType.DMA(())],
        input_output_aliases={3: 0},
        compiler_params=_cparams(("arbitrary",)),
        name="moe_dispatch",
    )(dest, x, nw, xb_zero)


def _expert_body(be_ref, nu_ref, xb_ref, wg_ref, wu_ref, wd_ref, yb_ref, wg_buf, wu_buf, wd_buf):
    b = pl.program_id(0)
    prev = be_ref[jnp.maximum(b - 1, 0)]
    fresh = (b == 0) | (be_ref[b] != prev)

    @pl.when((b < nu_ref[0]) & fresh)
    def _():
        wg_buf[...] = wg_ref[0, 0].astype(BF16)
        wu_buf[...] = wu_ref[0, 0].astype(BF16)
        wd_buf[...] = wd_ref[0, 0].astype(BF16)

    @pl.when(b < nu_ref[0])
    def _():
        xb = _from_tiles(xb_ref, 0, MOE_BLOCK, wg_buf.shape[0] // LANES).astype(BF16)
        gate = jnp.dot(xb, wg_buf[...], preferred_element_type=F32)
        up = jnp.dot(xb, wu_buf[...], preferred_element_type=F32)
        hid = (_silu(gate) * up).astype(BF16)
        _to_tiles(yb_ref, 0, jnp.dot(hid, wd_buf[...], preferred_element_type=F32))

    @pl.when(b >= nu_ref[0])
    def _():
        yb_ref[...] = jnp.zeros_like(yb_ref)


def _experts(blk_exp, n_used, xb, wg, wu, wd, *, layer):
    d, f = wg.shape[2], wg.shape[3]
    rt = d // LANES
    nb = xb.shape[0] // rt // MOE_BLOCK
    blk_rows = MOE_BLOCK * rt

    def xmap(b, be, nu):
        return (jnp.minimum(b, nu[0] - 1), 0)

    def wmap(b, be, nu):
        return (layer, be[b], 0, 0)

    return pl.pallas_call(
        _expert_body,
        grid_spec=pltpu.PrefetchScalarGridSpec(
            num_scalar_prefetch=2,
            grid=(nb,),
            in_specs=[
                pl.BlockSpec((blk_rows, LANES), xmap),
                pl.BlockSpec((1, 1, d, f), wmap), pl.BlockSpec((1, 1, d, f), wmap),
                pl.BlockSpec((1, 1, f, d), wmap),
            ],
            out_specs=pl.BlockSpec((blk_rows, LANES), lambda b, be, nu: (b, 0)),
            scratch_shapes=[pltpu.VMEM((d, f), BF16), pltpu.VMEM((d, f), BF16), pltpu.VMEM((f, d), BF16)],
        ),
        out_shape=jax.ShapeDtypeStruct(xb.shape, F32),
        compiler_params=_cparams(("arbitrary",)),
        name="moe_experts",
    )(blk_exp, n_used, xb, wg, wu, wd)


def _combine_body(dest_ref, x_ref, info_ref, fw_ref, yb_ref, out_ref, y_buf, sem, *, final_norm):
    tm, rt = x_ref.shape[0], x_ref.shape[1] // LANES

    def issue(a, carry):
        _row_copy(yb_ref, dest_ref[0, 0, a], y_buf, a, sem, rt).start()
        return carry

    lax.fori_loop(0, TOP_K * tm, issue, 0, unroll=8)

    def drain(a, carry):
        _row_copy(yb_ref, 0, y_buf, 0, sem, rt).wait()
        return carry

    lax.fori_loop(0, TOP_K * tm, drain, 0, unroll=8)
    info = info_ref[...]
    out = x_ref[...]
    for k in range(TOP_K):
        out = out + info[:, k:k + 1] * _from_tiles(y_buf, k * tm, tm, rt)
    if final_norm:
        out = _rms(out, fw_ref[...])
    out_ref[...] = out


def _combine(x, info, dest_kmajor, yb, fw, *, tm, final_norm):
    t, d = x.shape
    nt = t // tm
    return pl.pallas_call(
        functools.partial(_combine_body, final_norm=final_norm),
        grid=(nt,),
        in_specs=[
            pl.BlockSpec((1, 1, TOP_K * tm), lambda i: (i, 0, 0), memory_space=pltpu.SMEM),
            pl.BlockSpec((tm, d), lambda i: (i, 0)),
            pl.BlockSpec((tm, LANES), lambda i: (i, 0)),
            _full(fw.shape),
            pl.BlockSpec(memory_space=pl.ANY),
        ],
        out_specs=pl.BlockSpec((tm, d), lambda i: (i, 0)),
        out_shape=jax.ShapeDtypeStruct((t, d), F32),
        scratch_shapes=[pltpu.VMEM((TOP_K * tm * (d // LANES), LANES), F32), pltpu.SemaphoreType.DMA(())],
        compiler_params=_cparams(("arbitrary",)),
        name="moe_combine",
    )(dest_kmajor, x, info, fw, yb)


def _moe(x, nw, w_group, b_group, w_expert, b_expert, wg, wu, wd, fw, *, layer, final_norm):
    t, d = x.shape
    tm = min(256, t)
    nt = t // tm
    pad = LANES - N_EXPERTS - N_EXPERT_GROUPS
    wr = jnp.pad(jnp.concatenate([w_expert, w_group], axis=1), ((0, 0), (0, pad)))
    br = jnp.pad(jnp.concatenate([b_expert, b_group]), (0, pad)).reshape(1, LANES)
    nw2 = nw.reshape(1, d)
    info, cnt = _route(x, nw2, wr, br, tm=tm)

    eid = info[:, 2:2 + TOP_K].astype(I32)
    rank = info[:, 4:4 + TOP_K].astype(I32)
    cnt = cnt[:, 0, :N_EXPERTS].astype(I32)
    base = jnp.cumsum(cnt, axis=0) - cnt
    total = jnp.sum(cnt, axis=0)
    padded = (total + MOE_BLOCK - 1) // MOE_BLOCK * MOE_BLOCK
    pend = jnp.cumsum(padded)
    off = (pend - padded)[None, :] + base
    off_t = jnp.repeat(off, tm, axis=0)
    hot = eid[:, :, None] == jnp.arange(N_EXPERTS, dtype=I32)[None, None, :]
    dest = jnp.sum(jnp.where(hot, off_t[:, None, :], 0), axis=-1) + rank
    n_blocks = -(-(t * TOP_K + N_EXPERTS * (MOE_BLOCK - 1)) // MOE_BLOCK)
    blk_exp = jnp.minimum(jnp.sum(pend[None, :] <= (jnp.arange(n_blocks, dtype=I32) * MOE_BLOCK)[:, None], axis=1),
                          N_EXPERTS - 1).astype(I32)
    n_used = (pend[-1] // MOE_BLOCK).astype(I32).reshape(1)
    dest_tok = dest.reshape(nt, 1, tm * TOP_K)
    dest_k = dest.reshape(nt, tm, TOP_K).transpose(0, 2, 1).reshape(nt, 1, TOP_K * tm)

    xb = _dispatch(x, nw2, dest_tok, jnp.zeros((n_blocks * MOE_BLOCK * (d // LANES), LANES), F32), tm=tm)
    yb = _experts(blk_exp, n_used, xb, wg, wu, wd, layer=layer)
    return _combine(x, info, dest_k, yb, fw.reshape(1, d), tm=tm, final_norm=final_norm)


def _linear_body(x_ref, nw_ref, w_ref, b_ref, r_ref, out_ref, *, norm):
    x = x_ref[...]
    if norm:
        x = _rms(x, nw_ref[...])
    out_ref[...] = _bdot(x, w_ref[...]) + b_ref[...] + r_ref[...]


def _linear(x, w, *, nw=None, bias=None, res=None, tn):
    m, kd = x.shape
    n = w.shape[1]
    norm = nw is not None
    nw = jnp.ones((1, kd), F32) if nw is None else nw
    bias = jnp.zeros((1, n), F32) if bias is None else bias
    res = jnp.zeros((m, n), F32) if res is None else res
    tn = min(tn, n)
    return pl.pallas_call(
        functools.partial(_linear_body, norm=norm),
        grid=(n // tn,),
        in_specs=[
            _full(x.shape), _full(nw.shape),
            pl.BlockSpec((kd, tn), lambda j: (0, j)),
            pl.BlockSpec((1, tn), lambda j: (0, j)),
            pl.BlockSpec((m, tn), lambda j: (0, j)),
        ],
        out_specs=pl.BlockSpec((m, tn), lambda j: (0, j)),
        out_shape=jax.ShapeDtypeStruct((m, n), F32),
        compiler_params=_cparams(("arbitrary",)),
        name="sample_linear",
    )(x, nw, w, bias, res)


def _sample_conv_body(xbc_ref, st_ref, cw_ref, cb_ref, dtr_ref, dtb_ref, xc_ref, stn_ref, dt_ref, cbg_ref,
                      *, d_inner):
    nst = D_STATE
    xbc = xbc_ref[...]
    acc = cb_ref[...] + cw_ref[3:4, :] * xbc
    for k in range(CONV_W - 1):
        acc = acc + cw_ref[k:k + 1, :] * st_ref[k]
    xc = _silu(acc)
    xc_ref[...] = xc
    stn_ref[0] = st_ref[1]
    stn_ref[1] = st_ref[2]
    stn_ref[2] = xbc
    dt_ref[...] = _softplus(dtr_ref[...] + dtb_ref[...])
    lane = lax.broadcasted_iota(I32, (xbc.shape[0], LANES), 1)
    cbg = jnp.zeros((xbc.shape[0], LANES), F32)
    for g in range(N_BC_GROUPS):
        b_g = xc[:, d_inner + g * nst:d_inner + (g + 1) * nst]
        c_g = xc[:, d_inner + (N_BC_GROUPS + g) * nst:d_inner + (N_BC_GROUPS + g + 1) * nst]
        cbg = jnp.where(lane == g, jnp.sum(b_g * c_g, axis=-1, keepdims=True), cbg)
    cbg_ref[...] = cbg


def _sample_conv(xbc, st_t, cw, cb, dtr, dtb, *, d_inner):
    m, cd = xbc.shape
    return pl.pallas_call(
        functools.partial(_sample_conv_body, d_inner=d_inner),
        grid=(1,),
        in_specs=[_full(xbc.shape), _full(st_t.shape), _full(cw.shape), _full(cb.shape), _full(dtr.shape),
                  _full(dtb.shape)],
        out_specs=[_full((m, cd)), _full(st_t.shape), _full((m, LANES)), _full((m, LANES))],
        out_shape=[jax.ShapeDtypeStruct((m, cd), F32), jax.ShapeDtypeStruct(st_t.shape, F32),
                   jax.ShapeDtypeStruct((m, LANES), F32), jax.ShapeDtypeStruct((m, LANES), F32)],
        compiler_params=_cparams(("arbitrary",)),
        name="sample_conv",
    )(xbc, st_t, cw, cb, dtr, dtb)


def _sample_ssd_body(s0_ref, xt_ref, bc_ref, hs_ref, par_ref, sn_ref, yt_ref, *, n_heads):
    hp = SSM_HEAD_DIM
    hpg = n_heads // N_BC_GROUPS
    xt = xt_ref[0]
    dt = hs_ref[0, 0:1, :]
    cbh = hs_ref[0, 1:2, :]
    a = -jnp.exp(par_ref[0:1, :])
    dsk = par_ref[1:2, :]
    dec = jnp.exp(dt * a)
    xdt = xt * dt
    lane = lax.broadcasted_iota(I32, (hp, n_heads), 1)
    yoff = jnp.zeros((hp, n_heads), F32)
    for hh in range(n_heads):
        g = hh // hpg
        b_row = bc_ref[0, g:g + 1, :]
        c_row = bc_ref[0, N_BC_GROUPS + g:N_BC_GROUPS + g + 1, :]
        s0 = s0_ref[0, hh]
        yo = jnp.sum(s0 * c_row, axis=-1, keepdims=True)
        yoff = jnp.where(lane == hh, yo, yoff)
        sn_ref[0, hh] = s0 * dec[:, hh:hh + 1] + xdt[:, hh:hh + 1] * b_row
    yt_ref[0] = cbh * xdt + yoff * dec + xt * dsk


def _sample_ssd(s0, xt, bc, hs, par):
    bsz, n_heads, hp, nst = s0.shape
    return pl.pallas_call(
        functools.partial(_sample_ssd_body, n_heads=n_heads),
        grid=(bsz,),
        in_specs=[
            pl.BlockSpec((1, n_heads, hp, nst), lambda b: (b, 0, 0, 0)),
            pl.BlockSpec((1, hp, n_heads), lambda b: (b, 0, 0)),
            pl.BlockSpec((1,) + bc.shape[1:], lambda b: (b, 0, 0)),
            pl.BlockSpec((1,) + hs.shape[1:], lambda b: (b, 0, 0)),
            _full(par.shape),
        ],
        out_specs=[
            pl.BlockSpec((1, n_heads, hp, nst), lambda b: (b, 0, 0, 0)),
            pl.BlockSpec((1, hp, n_heads), lambda b: (b, 0, 0)),
        ],
        out_shape=[jax.ShapeDtypeStruct(s0.shape, F32), jax.ShapeDtypeStruct((bsz, hp, n_heads), F32)],
        compiler_params=_cparams(("arbitrary",)),
        name="sample_ssd",
    )(s0, xt, bc, hs, par)


def _sample_gnorm_out_body(y_ref, z_ref, gnw_ref, wout_ref, x_ref, out_ref, *, d_inner):
    gw = d_inner // N_BC_GROUPS
    acc = x_ref[...]
    for g in range(N_BC_GROUPS):
        glanes = slice(g * gw, (g + 1) * gw)
        gg = y_ref[:, glanes] * _silu(z_ref[:, glanes])
        gg = gg * lax.rsqrt(jnp.mean(gg * gg, axis=-1, keepdims=True) + EPS) * gnw_ref[:, glanes]
        acc = acc + jnp.dot(gg.astype(BF16), wout_ref[glanes, :], preferred_element_type=F32)
    out_ref[...] = acc


def _sample_gnorm_out(y, z, gnw, wout, x):
    d_inner = y.shape[1]
    return pl.pallas_call(
        functools.partial(_sample_gnorm_out_body, d_inner=d_inner),
        grid=(1,),
        in_specs=[_full(y.shape), _full(z.shape), _full(gnw.shape), _full(wout.shape), _full(x.shape)],
        out_specs=_full(x.shape),
        out_shape=jax.ShapeDtypeStruct(x.shape, F32),
        compiler_params=_cparams(("arbitrary",)),
        name="sample_gnorm_out",
    )(y, z, gnw, wout, x)


def _sample_attn_body(q_ref, kn_ref, vn_ref, kc_ref, vc_ref, sink_ref, o_ref, ko_ref, vo_ref):
    bt = q_ref.shape[0]
    wb = kc_ref.shape[1]
    hd = HEAD_DIM
    qpk = N_Q_HEADS // N_KV_HEADS
    scale = hd ** -0.5
    for b in range(bt):
        kc = kc_ref[b]
        vc = vc_ref[b]
        kn = kn_ref[b]
        vn = vn_ref[b]
        for g in range(N_KV_HEADS):
            cols = slice(g * hd, (g + 1) * hd)
            q_g = q_ref[b, g * qpk:(g + 1) * qpk, :]
            sink = sink_ref[g * qpk:(g + 1) * qpk, :]
            s = _bdot_nt(q_g, kc[:, cols]) * scale
            s_new = jnp.sum(q_g * kn[:, cols], axis=-1, keepdims=True) * scale
            m = jnp.maximum(jnp.maximum(jnp.max(s, axis=-1, keepdims=True), s_new), sink)
            p = jnp.exp(s - m)
            p_new = jnp.exp(s_new - m)
            denom = jnp.sum(p, axis=-1, keepdims=True) + p_new + jnp.exp(sink - m)
            pv = _bdot(p, vc[:, cols]) + p_new * vn[:, cols]
            o_ref[b, g * qpk:(g + 1) * qpk, :] = pv / denom
        ko_ref[b, 0:wb - 1, :] = kc[1:wb, :]
        ko_ref[b, wb - 1:wb, :] = kn
        vo_ref[b, 0:wb - 1, :] = vc[1:wb, :]
        vo_ref[b, wb - 1:wb, :] = vn


def _sample_attn(q3, kn, vn, kc, vc, sinks, *, bt):
    bsz, nqh, hd = q3.shape
    wb, nk = kc.shape[1], kc.shape[2]
    return pl.pallas_call(
        _sample_attn_body,
        grid=(bsz // bt,),
        in_specs=[
            pl.BlockSpec((bt, nqh, hd), lambda i: (i, 0, 0)),
            pl.BlockSpec((bt, 1, nk), lambda i: (i, 0, 0)),
            pl.BlockSpec((bt, 1, nk), lambda i: (i, 0, 0)),
            pl.BlockSpec((bt, wb, nk), lambda i: (i, 0, 0)),
            pl.BlockSpec((bt, wb, nk), lambda i: (i, 0, 0)),
            _full(sinks.shape),
        ],
        out_specs=[
            pl.BlockSpec((bt, nqh, hd), lambda i: (i, 0, 0)),
            pl.BlockSpec((bt, wb, nk), lambda i: (i, 0, 0)),
            pl.BlockSpec((bt, wb, nk), lambda i: (i, 0, 0)),
        ],
        out_shape=[jax.ShapeDtypeStruct(q3.shape, F32), jax.ShapeDtypeStruct(kc.shape, F32),
                   jax.ShapeDtypeStruct(vc.shape, F32)],
        compiler_params=_cparams(("arbitrary",)),
        name="sample_attn",
    )(q3, kn, vn, kc, vc, sinks)


def _mamba_sample(x, nw, mw, state_conv, state_ssm):
    win, cw, cb, dtb, alog, dsk, gnw, wout = mw
    bsz, d = x.shape
    d_inner = wout.shape[0]
    conv_dim = cw.shape[1]
    n_heads = d_inner // SSM_HEAD_DIM
    hp = SSM_HEAD_DIM
    proj = _linear(x, win, nw=nw, tn=896)
    z = proj[:, :d_inner]
    xbc = proj[:, d_inner:d_inner + conv_dim]
    dtr = proj[:, d_inner + conv_dim:]
    st_t = jnp.transpose(state_conv, (1, 0, 2))
    xc, stn_t, dt, cbg = _sample_conv(xbc, st_t, cw, cb, dtr, dtb, d_inner=d_inner)
    conv_new = jnp.transpose(stn_t, (1, 0, 2))
    xt = jnp.transpose(xc[:, :d_inner].reshape(bsz, n_heads, hp), (0, 2, 1))
    bc = xc[:, d_inner:].reshape(bsz, 2 * N_BC_GROUPS, D_STATE)
    cbh = jnp.repeat(cbg[:, :N_BC_GROUPS], n_heads // N_BC_GROUPS, axis=1)
    hs = jnp.stack([dt[:, :n_heads], cbh], axis=1)
    par = jnp.stack([alog[0, :n_heads], dsk.reshape(n_heads, hp)[:, 0]], axis=0)
    ssm_new, yt = _sample_ssd(state_ssm, xt, bc, hs, par)
    y = jnp.transpose(yt, (0, 2, 1)).reshape(bsz, d_inner)
    out = _sample_gnorm_out(y, z, gnw, wout, x)
    return out, conv_new, ssm_new


def _attn_sample(x, nw, wqkv, bqkv, sinks, wo, bo, cache_k, cache_v):
    bsz, d = x.shape
    wb = cache_k.shape[1]
    nq = N_Q_HEADS * HEAD_DIM
    nk = N_KV_HEADS * HEAD_DIM
    qkv = _linear(x, wqkv, nw=nw, bias=bqkv, tn=512)
    q3 = qkv[:, :nq].reshape(bsz, N_Q_HEADS, HEAD_DIM)
    kn = qkv[:, nq:nq + nk].reshape(bsz, 1, nk)
    vn = qkv[:, nq + nk:].reshape(bsz, 1, nk)
    o3, ko, vo = _sample_attn(q3, kn, vn, cache_k.reshape(bsz, wb, nk), cache_v.reshape(bsz, wb, nk),
                              sinks.reshape(N_Q_HEADS, 1), bt=8)
    out = _linear(o3.reshape(bsz, nq), wo, bias=bo, res=x, tn=512)
    return out, ko.reshape(cache_k.shape), vo.reshape(cache_v.shape)


def kernel(x_prompt, x_sample, state_ssm, state_conv, cache_k_win, cache_v_win,
           mamba_w_in, mamba_conv_w, mamba_conv_b, mamba_dt_bias, mamba_a_log, mamba_d,
           mamba_norm_w, mamba_w_out, attn_w_qkv, attn_b_qkv, attn_sinks, attn_w_o, attn_b_o,
           norm_mix, norm_ffn, router_w_group, router_b_group, router_w_expert, router_b_expert,
           expert_w_gate, expert_w_up, expert_w_down, norm_final):
    bsz, seq, d = x_prompt.shape
    dbsz, dseq, _ = x_sample.shape
    assert dseq == 1 and cache_k_win.shape[2] <= WINDOW and seq % WINDOW == 0
    depth = norm_mix.shape[0]
    xp = x_prompt.reshape(bsz * seq, d)
    xs = x_sample.reshape(dbsz, d)
    ssm_p, conv_p, kp_l, vp_l = [], [], [], []
    ssm_s, conv_s, ks_l, vs_l = [], [], [], []
    for i in range(depth):
        j = i // 2
        nw = norm_mix[i].reshape(1, d)
        if i % 2 == 0:
            mw = _mamba_weights(mamba_w_in[j], mamba_conv_w[j], mamba_conv_b[j], mamba_dt_bias[j],
                                mamba_a_log[j], mamba_d[j], mamba_norm_w[j], mamba_w_out[j])
            xp, cp, sp = _mamba_prompt(xp, nw, *mw, bsz=bsz, seq=seq, ts=SSD_CHUNK)
            xs, cs_, ss_ = _mamba_sample(xs, nw, mw, state_conv[j], state_ssm[j])
            ssm_p.append(sp)
            conv_p.append(cp)
            ssm_s.append(ss_)
            conv_s.append(cs_)
        else:
            wqkv = attn_w_qkv[j].astype(BF16)
            bqkv = attn_b_qkv[j].reshape(1, -1)
            wo = attn_w_o[j].astype(BF16)
            bo = attn_b_o[j].reshape(1, d)
            xp, kp, vp = _attn_prompt(xp, attn_sinks[j], nw, wqkv, bqkv, wo, bo, bsz=bsz, seq=seq)
            xs, ks_, vs_ = _attn_sample(xs, nw, wqkv, bqkv, attn_sinks[j], wo, bo, cache_k_win[j], cache_v_win[j])
            kp_l.append(kp.reshape(bsz, WINDOW, N_KV_HEADS, HEAD_DIM))
            vp_l.append(vp.reshape(bsz, WINDOW, N_KV_HEADS, HEAD_DIM))
            ks_l.append(ks_)
            vs_l.append(vs_)
        last = i == depth - 1
        moe_w = (norm_ffn[i], router_w_group[i], router_b_group[i], router_w_expert[i], router_b_expert[i],
                 expert_w_gate, expert_w_up, expert_w_down, norm_final)
        xp = _moe(xp, *moe_w, layer=i, final_norm=last)
        xs = _moe(xs, *moe_w, layer=i, final_norm=last)
    return (xp.reshape(bsz, seq, d), xs.reshape(dbsz, dseq, d),
            jnp.stack(ssm_p), jnp.stack(conv_p), jnp.stack(kp_l), jnp.stack(vp_l),
            jnp.stack(ssm_s), jnp.stack(conv_s), jnp.stack(ks_l), jnp.stack(vs_l))
```

```python
import functools
import math

import jax
import jax.numpy as jnp
from jax import lax
from jax.experimental import pallas as pl
from jax.experimental.pallas import tpu as pltpu

F32 = jnp.float32
BF16 = jnp.bfloat16
I32 = jnp.int32

EPS = 1e-5
LANES = 128
VMEM_LIMIT = 56 * 1024 * 1024

SSM_HEAD_DIM = 64
D_STATE = 128
N_BC_GROUPS = 8
CONV_W = 4
SSD_CHUNK = 128
N_Q_HEADS = 16
N_KV_HEADS = 4
HEAD_DIM = 64
WINDOW = 128
N_EXPERT_GROUPS = 4
EXPERTS_PER_GROUP = 8
N_EXPERTS = N_EXPERT_GROUPS * EXPERTS_PER_GROUP
TOP_K = 2
MOE_BLOCK = 256


def _cparams(sem):
    return pltpu.CompilerParams(dimension_semantics=sem, vmem_limit_bytes=VMEM_LIMIT)


def _full(shape):
    n = len(shape)
    return pl.BlockSpec(shape, lambda *_: (0,) * n)


def _resident(shape):
    n = len(shape)
    return pl.BlockSpec(shape, lambda *_: (0,) * n, pipeline_mode=pl.Buffered(1))


def _rms(x, w):
    return x * lax.rsqrt(jnp.mean(x * x, axis=-1, keepdims=True) + EPS) * w


def _silu(x):
    return x / (1.0 + jnp.exp(-x))


def _softplus(x):
    return jnp.maximum(x, 0.0) + jnp.log(1.0 + jnp.exp(-jnp.abs(x)))


def _bdot(a, b):
    return jnp.dot(a.astype(BF16), b.astype(BF16), preferred_element_type=F32)


def _bdot_nt(a, b):
    return lax.dot_general(a.astype(BF16), b.astype(BF16), (((1,), (1,)), ((), ())),
                           preferred_element_type=F32)


def _bdot_tn(a, b):
    return lax.dot_general(a.astype(BF16), b.astype(BF16), (((0,), (0,)), ((), ())),
                           preferred_element_type=F32)


def _fdot(a, b):
    return jnp.dot(a, b, preferred_element_type=F32, precision=lax.Precision.HIGHEST)


def _mamba_prompt_body(x_ref, nw_ref, win_ref, cw_ref, cb_ref, dtb_ref, alog_ref, dsk_ref, gnw_ref,
                       wout_ref, out_ref, conv_ref, ssm_ref, h_buf, xbc_buf, xc_buf, st_buf, y_buf,
                       *, d_inner, n_heads):
    ts = x_ref.shape[0]
    cs = SSD_CHUNK
    hp = SSM_HEAD_DIM
    nst = D_STATE
    gw = d_inner // N_BC_GROUPS
    hpg = n_heads // N_BC_GROUPS
    conv_dim = d_inner + 2 * N_BC_GROUPS * nst
    s = pl.program_id(1)

    @pl.when(s == 0)
    def _():
        xbc_buf[0:8, :] = jnp.zeros((8, conv_dim), F32)
        st_buf[...] = jnp.zeros_like(st_buf)

    h_buf[...] = _rms(x_ref[...], nw_ref[...]).astype(BF16)
    dtr = jnp.dot(h_buf[...], win_ref[:, d_inner + conv_dim:], preferred_element_type=F32)
    ct = 512
    for j in range(conv_dim // ct):
        cols = slice(j * ct, (j + 1) * ct)
        xbc_buf[8:8 + ts, cols] = jnp.dot(h_buf[...], win_ref[:, d_inner + j * ct:d_inner + (j + 1) * ct],
                                          preferred_element_type=F32)
        acc = cb_ref[:, cols] + cw_ref[3:4, cols] * xbc_buf[8:8 + ts, cols]
        acc = acc + cw_ref[2:3, cols] * xbc_buf[7:7 + ts, cols]
        acc = acc + cw_ref[1:2, cols] * xbc_buf[6:6 + ts, cols]
        acc = acc + cw_ref[0:1, cols] * xbc_buf[5:5 + ts, cols]
        xc_buf[:, cols] = _silu(acc)
    last3 = xbc_buf[5 + ts:8 + ts, :]
    xbc_buf[5:8, :] = last3
    conv_ref[0] = last3

    dt = _softplus(dtr + dtb_ref[...])
    da = dt * (-jnp.exp(alog_ref[...]))
    row = lax.broadcasted_iota(I32, (cs, cs), 0)
    col = lax.broadcasted_iota(I32, (cs, cs), 1)
    causal = row >= col
    tril = causal.astype(F32)
    lane = lax.broadcasted_iota(I32, (cs, LANES), 1)
    lo_half = lane < hp

    for c in range(ts // cs):
        rows = slice(c * cs, (c + 1) * cs)
        da_c = da[rows]
        dt_c = dt[rows]
        acum = _fdot(tril, da_c)
        acum_t = acum.T
        dt_t = dt_c.T
        a_last = acum[cs - 1:cs, :]
        to_end = jnp.exp(a_last - acum)
        w_all = dt_c * to_end
        ea = jnp.exp(acum)
        cd = jnp.exp(a_last)
        for g in range(N_BC_GROUPS):
            b_g = xc_buf[rows, d_inner + g * nst:d_inner + (g + 1) * nst]
            c_g = xc_buf[rows, d_inner + (N_BC_GROUPS + g) * nst:d_inner + (N_BC_GROUPS + g + 1) * nst]
            cb = _bdot_nt(c_g, b_g)
            xw_parts = []
            for pr in range(hpg // 2):
                h0 = g * hpg + 2 * pr
                lanes0 = slice(h0 * hp, h0 * hp + 2 * hp)
                x_pair = xc_buf[rows, lanes0]
                st_pair = st_buf[:, lanes0]
                y_pair = jnp.zeros((cs, 2 * hp), F32)
                w_pair = jnp.zeros((cs, 2 * hp), F32)
                for k in range(2):
                    hh = h0 + k
                    seg = acum[:, hh:hh + 1] - acum_t[hh:hh + 1, :]
                    dec = jnp.exp(jnp.where(causal, seg, -jnp.inf))
                    m = cb * dec * dt_t[hh:hh + 1, :]
                    sel = lo_half if k == 0 else jnp.logical_not(lo_half)
                    x_k = jnp.where(sel, x_pair, 0.0)
                    st_k = jnp.where(sel, st_pair, 0.0)
                    lhs = jnp.concatenate([m, c_g * ea[:, hh:hh + 1]], axis=1)
                    rhs = jnp.concatenate([x_k, st_k], axis=0)
                    y_pair = y_pair + _bdot(lhs, rhs)
                    w_pair = jnp.where(sel, w_all[:, hh:hh + 1], w_pair)
                y_buf[rows, lanes0] = y_pair + x_pair * dsk_ref[:, lanes0]
                xw_parts.append(x_pair * w_pair)
            xw = jnp.concatenate(xw_parts, axis=1)
            glanes = slice(g * gw, (g + 1) * gw)
            cd_parts = [jnp.broadcast_to(cd[:, g * hpg + k:g * hpg + k + 1], (1, hp)) for k in range(hpg)]
            cd_g = jnp.concatenate(cd_parts, axis=1)
            st_buf[:, glanes] = st_buf[:, glanes] * cd_g + _bdot_tn(b_g, xw)

    @pl.when(s == pl.num_programs(1) - 1)
    def _():
        for pr in range(n_heads // 2):
            t = st_buf[:, 2 * pr * hp:2 * (pr + 1) * hp].T
            ssm_ref[0, 2 * pr] = t[0:hp]
            ssm_ref[0, 2 * pr + 1] = t[hp:2 * hp]

    acc = x_ref[...]
    for g in range(N_BC_GROUPS):
        glanes = slice(g * gw, (g + 1) * gw)
        z = jnp.dot(h_buf[...], win_ref[:, glanes], preferred_element_type=F32)
        gg = y_buf[:, glanes] * _silu(z)
        gg = gg * lax.rsqrt(jnp.mean(gg * gg, axis=-1, keepdims=True) + EPS) * gnw_ref[:, glanes]
        acc = acc + jnp.dot(gg.astype(BF16), wout_ref[glanes, :], preferred_element_type=F32)
    out_ref[...] = acc


def _mamba_prompt(x, nw, win, cw, cb, dtb, alog, dsk, gnw, wout, *, bsz, seq, ts):
    d = x.shape[1]
    d_inner = wout.shape[0]
    n_heads = d_inner // SSM_HEAD_DIM
    conv_dim = cw.shape[1]
    ns = seq // ts
    body = functools.partial(_mamba_prompt_body, d_inner=d_inner, n_heads=n_heads)
    return pl.pallas_call(
        body,
        grid=(bsz, ns),
        in_specs=[
            pl.BlockSpec((ts, d), lambda b, s: (b * ns + s, 0)),
            _full(nw.shape), _resident(win.shape), _full(cw.shape), _full(cb.shape), _full(dtb.shape),
            _full(alog.shape), _full(dsk.shape), _full(gnw.shape), _resident(wout.shape),
        ],
        out_specs=[
            pl.BlockSpec((ts, d), lambda b, s: (b * ns + s, 0)),
            pl.BlockSpec((1, CONV_W - 1, conv_dim), lambda b, s: (b, 0, 0)),
            pl.BlockSpec((1, n_heads, SSM_HEAD_DIM, D_STATE), lambda b, s: (b, 0, 0, 0)),
        ],
        out_shape=[
            jax.ShapeDtypeStruct((bsz * seq, d), F32),
            jax.ShapeDtypeStruct((bsz, CONV_W - 1, conv_dim), F32),
            jax.ShapeDtypeStruct((bsz, n_heads, SSM_HEAD_DIM, D_STATE), F32),
        ],
        scratch_shapes=[
            pltpu.VMEM((ts, d), BF16),
            pltpu.VMEM((8 + ts, conv_dim), F32),
            pltpu.VMEM((ts, conv_dim), F32),
            pltpu.VMEM((D_STATE, d_inner), F32),
            pltpu.VMEM((ts, d_inner), F32),
        ],
        compiler_params=_cparams(("arbitrary", "arbitrary")),
        name="mamba_prompt",
    )(x, nw, win, cw, cb, dtb, alog, dsk, gnw, wout)


def _mamba_weights(w_in, conv_w, conv_b, dt_bias, a_log, d_skip, norm_w, w_out):
    d_inner = w_out.shape[0]
    n_heads = dt_bias.shape[0]
    pad = LANES - n_heads
    win = jnp.pad(w_in, ((0, 0), (0, pad))).astype(BF16)
    dtb = jnp.pad(dt_bias, (0, pad)).reshape(1, LANES)
    alog = jnp.pad(a_log, (0, pad)).reshape(1, LANES)
    dsk = jnp.repeat(d_skip, SSM_HEAD_DIM).reshape(1, d_inner)
    return (win, conv_w, conv_b.reshape(1, -1), dtb, alog, dsk, norm_w.reshape(1, d_inner),
            w_out.astype(BF16))


def _sink_softmax_pv(s, sink, v):
    m = jnp.maximum(jnp.max(s, axis=-1, keepdims=True), sink)
    p = jnp.exp(s - m)
    denom = jnp.sum(p, axis=-1, keepdims=True) + jnp.exp(sink - m)
    return _bdot(p, v) / denom


def _attn_prompt_body(sink_ref, x_ref, nw_ref, wqkv_ref, bqkv_ref, wo_ref, bo_ref,
                      out_ref, kwin_ref, vwin_ref, kv_buf, q_buf, o_buf):
    blk = WINDOW
    hd = HEAD_DIM
    nq = N_Q_HEADS * hd
    nk = N_KV_HEADS * hd
    qpk = N_Q_HEADS // N_KV_HEADS
    tq = x_ref.shape[0]
    s_id = pl.program_id(1)

    @pl.when(s_id == 0)
    def _():
        kv_buf[0:blk, :] = jnp.zeros((blk, 2 * nk), F32)

    h = _rms(x_ref[...], nw_ref[...]).astype(BF16)
    q_buf[...] = jnp.dot(h, wqkv_ref[:, 0:nq], preferred_element_type=F32) + bqkv_ref[:, 0:nq]
    kv_buf[blk:blk + tq, :] = jnp.dot(h, wqkv_ref[:, nq:], preferred_element_type=F32) + bqkv_ref[:, nq:]
    kwin_ref[0] = kv_buf[tq:tq + blk, 0:nk]
    vwin_ref[0] = kv_buf[tq:tq + blk, nk:]

    row = lax.broadcasted_iota(I32, (blk, 2 * blk), 0)
    col = lax.broadcasted_iota(I32, (blk, 2 * blk), 1)
    diff = row + blk - col
    band = (diff >= 0) & (diff <= WINDOW)
    scale = hd ** -0.5
    for qb in range(tq // blk):
        qrows = slice(qb * blk, (qb + 1) * blk)
        krows = slice(qb * blk, (qb + 2) * blk)
        ok = band & ((col >= blk) | (s_id > 0)) if qb == 0 else band
        for g in range(N_KV_HEADS):
            k_g = kv_buf[krows, g * hd:(g + 1) * hd]
            v_g = kv_buf[krows, nk + g * hd:nk + (g + 1) * hd]
            for j in range(qpk):
                hh = g * qpk + j
                s = _bdot_nt(q_buf[qrows, hh * hd:(hh + 1) * hd], k_g) * scale
                s = jnp.where(ok, s, -jnp.inf)
                o_buf[qrows, hh * hd:(hh + 1) * hd] = _sink_softmax_pv(s, sink_ref[hh], v_g)
    kv_buf[0:blk, :] = kv_buf[tq:tq + blk, :]
    out_ref[...] = (x_ref[...] + jnp.dot(o_buf[...].astype(BF16), wo_ref[...], preferred_element_type=F32)
                    + bo_ref[...])


def _attn_prompt(x, sinks, nw, wqkv, bqkv, wo, bo, *, bsz, seq, tq):
    d = x.shape[1]
    blk = WINDOW
    nb = seq // tq
    nk = N_KV_HEADS * HEAD_DIM
    nq = N_Q_HEADS * HEAD_DIM
    return pl.pallas_call(
        _attn_prompt_body,
        grid=(bsz, nb),
        in_specs=[
            pl.BlockSpec(memory_space=pltpu.SMEM),
            pl.BlockSpec((tq, d), lambda b, s: (b * nb + s, 0)),
            _full(nw.shape), _full(wqkv.shape), _full(bqkv.shape), _full(wo.shape), _full(bo.shape),
        ],
        out_specs=[
            pl.BlockSpec((tq, d), lambda b, s: (b * nb + s, 0)),
            pl.BlockSpec((1, blk, nk), lambda b, s: (b, 0, 0)),
            pl.BlockSpec((1, blk, nk), lambda b, s: (b, 0, 0)),
        ],
        out_shape=[
            jax.ShapeDtypeStruct((bsz * seq, d), F32),
            jax.ShapeDtypeStruct((bsz, blk, nk), F32),
            jax.ShapeDtypeStruct((bsz, blk, nk), F32),
        ],
        scratch_shapes=[
            pltpu.VMEM((blk + tq, 2 * nk), F32),
            pltpu.VMEM((tq, nq), F32),
            pltpu.VMEM((tq, nq), F32),
        ],
        compiler_params=_cparams(("arbitrary", "arbitrary")),
        name="attn_prompt",
    )(sinks, x, nw, wqkv, bqkv, wo, bo)


def _route_body(x_ref, nw_ref, wr_ref, br_ref, info_ref, cnt_ref):
    tm = x_ref.shape[0]
    h = _rms(x_ref[...], nw_ref[...])
    h_hi = h.astype(BF16)
    h_lo = (h - h_hi.astype(F32)).astype(BF16)
    part = jnp.dot(h_hi, wr_ref[...], preferred_element_type=F32)
    logits = (part[:, 0:LANES] + part[:, LANES:] + jnp.dot(h_lo, wr_ref[:, 0:LANES], preferred_element_type=F32)
              + br_ref[...])
    lane_i = lax.broadcasted_iota(I32, (tm, LANES), 1)
    lane = lane_i.astype(F32)
    lane_grp = (lane_i // EXPERTS_PER_GROUP).astype(F32)
    big = float(LANES)
    ninf = -jnp.inf

    def first_argmax(v):
        m = jnp.max(v, axis=-1, keepdims=True)
        return m, jnp.min(jnp.where(v == m, lane, big), axis=-1, keepdims=True)

    gmask = (lane_i >= N_EXPERTS) & (lane_i < N_EXPERTS + N_EXPERT_GROUPS)
    gl = jnp.where(gmask, logits, ninf)
    gmax, gi = first_argmax(gl)
    gi = gi - float(N_EXPERTS)
    pg = 1.0 / jnp.sum(jnp.exp(gl - gmax), axis=-1, keepdims=True)
    emask = (lane_i < N_EXPERTS) & (lane_grp == gi)
    el = jnp.where(emask, logits, ninf)
    m1, i1 = first_argmax(el)
    el2 = jnp.where(lane == i1, ninf, el)
    m2, i2 = first_argmax(el2)
    den = jnp.sum(jnp.exp(el - m1), axis=-1, keepdims=True)
    tp1 = 1.0 / den
    tp2 = jnp.exp(m2 - m1) / den
    g1 = pg * tp1 / (tp1 + tp2)
    g2 = pg * tp2 / (tp1 + tp2)
    hot1 = lane == i1
    hot2 = lane == i2
    onehot = jnp.where(hot1 | hot2, 1.0, 0.0)
    rr = lax.broadcasted_iota(I32, (tm, tm), 0)
    cc = lax.broadcasted_iota(I32, (tm, tm), 1)
    before = jnp.where(rr > cc, 1.0, 0.0)
    cum = _bdot(before, onehot)
    r1 = jnp.sum(jnp.where(hot1, cum, 0.0), axis=-1, keepdims=True)
    r2 = jnp.sum(jnp.where(hot2, cum, 0.0), axis=-1, keepdims=True)
    info = jnp.zeros((tm, LANES), F32)
    for k, v in enumerate((g1, g2, i1, i2, r1, r2)):
        info = jnp.where(lane_i == k, v, info)
    info_ref[...] = info
    cnt_ref[0] = jnp.broadcast_to(jnp.sum(onehot, axis=0, keepdims=True), (8, LANES))


def _route(x, nw, wr, br, *, tm):
    t, d = x.shape
    nt = t // tm
    return pl.pallas_call(
        _route_body,
        grid=(nt,),
        in_specs=[pl.BlockSpec((tm, d), lambda i: (i, 0)), _full(nw.shape), _full(wr.shape), _full(br.shape)],
        out_specs=[pl.BlockSpec((tm, LANES), lambda i: (i, 0)), pl.BlockSpec((1, 8, LANES), lambda i: (i, 0, 0))],
        out_shape=[jax.ShapeDtypeStruct((t, LANES), F32), jax.ShapeDtypeStruct((nt, 8, LANES), F32)],
        compiler_params=_cparams(("arbitrary",)),
        name="moe_route",
    )(x, nw, wr, br)


def _to_tiles(ref, base, val):
    m, rt = val.shape[0], val.shape[1] // LANES
    for j in range(rt):
        ref[pl.ds(base * rt + j, m, stride=rt), :] = val[:, j * LANES:(j + 1) * LANES]


def _from_tiles(ref, base, m, rt):
    return jnp.concatenate([ref[pl.ds(base * rt + j, m, stride=rt), :] for j in range(rt)], axis=1)


SEG_W = 16
WIN_HDR = 2


def _max_windows(tm):
    return N_EXPERTS + TOP_K * tm // SEG_W


def _seg_copy(src, i, dst, j, sem, rt):
    n = SEG_W * rt
    return pltpu.make_async_copy(src.at[pl.ds(pl.multiple_of(i * rt, rt), n), :],
                                 dst.at[pl.ds(pl.multiple_of(j * rt, rt), n), :], sem)


def _tok(ref, p, rt):
    return ref.at[pl.ds(pl.multiple_of(p * rt, rt), rt), :]


def _dispatch_body(lpos_ref, win_ref, x_ref, nw_ref, xb_in_ref, xb_ref, h_buf, s_buf, sem):
    del xb_in_ref
    tm, rt = x_ref.shape[0], x_ref.shape[1] // LANES
    i = pl.program_id(0)
    half = TOP_K * tm + SEG_W
    sbase = (i % 2) * half
    mw = _max_windows(tm)

    @pl.when(i == 0)
    def _():
        for hb in range(2):
            s_buf[(hb * half + TOP_K * tm) * rt:(hb + 1) * half * rt, :] = jnp.zeros((SEG_W * rt, LANES), F32)

    _to_tiles(h_buf, 0, _rms(x_ref[...], nw_ref[...]))

    def move(t, carry):
        v = _tok(h_buf, t, rt)[...]
        for k in range(TOP_K):
            _tok(s_buf, sbase + lpos_ref[0, 0, t * TOP_K + k], rt)[...] = v
        return carry

    lax.fori_loop(0, tm, move, 0, unroll=8)

    def drain(w, carry):
        _seg_copy(s_buf, 0, xb_ref, 0, sem, rt).wait()
        return carry

    @pl.when(i > 0)
    def _():
        lax.fori_loop(0, win_ref[0, 0, 1], drain, 0)

    def issue(w, carry):
        _seg_copy(s_buf, sbase + win_ref[0, 0, WIN_HDR + w], xb_ref, win_ref[0, 0, WIN_HDR + mw + w], sem, rt).start()
        return carry

    lax.fori_loop(0, win_ref[0, 0, 0], issue, 0)

    @pl.when(i == pl.num_programs(0) - 1)
    def _():
        lax.fori_loop(0, win_ref[0, 0, 0], drain, 0)


def _dispatch(x, nw, lpos, win, xb_zero, *, tm):
    t, d = x.shape
    nt = t // tm
    rt = d // LANES
    return pl.pallas_call(
        _dispatch_body,
        grid=(nt,),
        in_specs=[
            pl.BlockSpec((1, 1, lpos.shape[2]), lambda i: (i, 0, 0), memory_space=pltpu.SMEM),
            pl.BlockSpec((1, 1, win.shape[2]), lambda i: (i, 0, 0), memory_space=pltpu.SMEM),
            pl.BlockSpec((tm, d), lambda i: (i, 0)),
            _full(nw.shape),
            pl.BlockSpec(memory_space=pl.ANY),
        ],
        out_specs=pl.BlockSpec(memory_space=pl.ANY),
        out_shape=jax.ShapeDtypeStruct(xb_zero.shape, F32),
        scratch_shapes=[pltpu.VMEM((tm * rt, LANES), F32),
                        pltpu.VMEM((2 * (TOP_K * tm + SEG_W) * rt, LANES), F32),
                        pltpu.SemaphoreType.DMA(())],
        input_output_aliases={4: 0},
        compiler_params=_cparams(("arbitrary",)),
        name="moe_dispatch",
    )(lpos, win, x, nw, xb_zero)


def _expert_body(be_ref, nu_ref, xb_ref, wg_ref, wu_ref, wd_ref, yb_ref, wg_buf, wu_buf, wd_buf):
    b = pl.program_id(0)
    prev = be_ref[jnp.maximum(b - 1, 0)]
    fresh = (b == 0) | (be_ref[b] != prev)

    @pl.when((b < nu_ref[0]) & fresh)
    def _():
        wg_buf[...] = wg_ref[0, 0].astype(BF16)
        wu_buf[...] = wu_ref[0, 0].astype(BF16)
        wd_buf[...] = wd_ref[0, 0].astype(BF16)

    @pl.when(b < nu_ref[0])
    def _():
        xb = _from_tiles(xb_ref, 0, MOE_BLOCK, wg_buf.shape[0] // LANES).astype(BF16)
        gate = jnp.dot(xb, wg_buf[...], preferred_element_type=F32)
        up = jnp.dot(xb, wu_buf[...], preferred_element_type=F32)
        hid = (_silu(gate) * up).astype(BF16)
        _to_tiles(yb_ref, 0, jnp.dot(hid, wd_buf[...], preferred_element_type=F32))

    @pl.when(b >= nu_ref[0])
    def _():
        yb_ref[...] = jnp.zeros_like(yb_ref)


def _experts(blk_exp, n_used, xb, wg, wu, wd, *, layer):
    d, f = wg.shape[2], wg.shape[3]
    rt = d // LANES
    nb = xb.shape[0] // rt // MOE_BLOCK
    blk_rows = MOE_BLOCK * rt

    def xmap(b, be, nu):
        return (jnp.minimum(b, jnp.maximum(nu[0] - 1, 0)), 0)

    def wmap(b, be, nu):
        return (layer, be[b], 0, 0)

    return pl.pallas_call(
        _expert_body,
        grid_spec=pltpu.PrefetchScalarGridSpec(
            num_scalar_prefetch=2,
            grid=(nb,),
            in_specs=[
                pl.BlockSpec((blk_rows, LANES), xmap),
                pl.BlockSpec((1, 1, d, f), wmap), pl.BlockSpec((1, 1, d, f), wmap),
                pl.BlockSpec((1, 1, f, d), wmap),
            ],
            out_specs=pl.BlockSpec((blk_rows, LANES), lambda b, be, nu: (b, 0)),
            scratch_shapes=[pltpu.VMEM((d, f), BF16), pltpu.VMEM((d, f), BF16), pltpu.VMEM((f, d), BF16)],
        ),
        out_shape=jax.ShapeDtypeStruct(xb.shape, F32),
        compiler_params=_cparams(("arbitrary",)),
        name="moe_experts",
    )(blk_exp, n_used, xb, wg, wu, wd)


def _ybuf_tokens(tm):
    return TOP_K * tm + N_EXPERTS * (SEG_W - 1) + SEG_W


def _combine_body(lpos_ref, gate_ref, win_ref, winn_ref, x_ref, fw_ref, yb_ref, out_ref, y_buf, x_buf, sem,
                  *, final_norm):
    tm, rt = x_ref.shape[0], x_ref.shape[1] // LANES
    i = pl.program_id(0)
    slot = i % 2
    half = _ybuf_tokens(tm)
    mw = _max_windows(tm)

    def fetch(tab_ref, sl):
        def issue(w, carry):
            _seg_copy(yb_ref, tab_ref[0, 0, WIN_HDR + w], y_buf, sl * half + tab_ref[0, 0, WIN_HDR + mw + w],
                      sem.at[sl], rt).start()
            return carry

        lax.fori_loop(0, tab_ref[0, 0, 0], issue, 0)

    @pl.when(i == 0)
    def _():
        fetch(win_ref, 0)

    @pl.when(i + 1 < pl.num_programs(0))
    def _():
        fetch(winn_ref, 1 - slot)

    _to_tiles(x_buf, 0, x_ref[...])

    def drain(w, carry):
        _seg_copy(yb_ref, 0, y_buf, 0, sem.at[slot], rt).wait()
        return carry

    lax.fori_loop(0, win_ref[0, 0, 0], drain, 0)
    ybase = slot * half

    def comb(t, carry):
        acc = _tok(x_buf, t, rt)[...]
        for k in range(TOP_K):
            a = t * TOP_K + k
            acc = acc + gate_ref[0, 0, a] * _tok(y_buf, ybase + lpos_ref[0, 0, a], rt)[...]
        _tok(x_buf, t, rt)[...] = acc
        return carry

    lax.fori_loop(0, tm, comb, 0, unroll=8)
    out = _from_tiles(x_buf, 0, tm, rt)
    if final_norm:
        out = _rms(out, fw_ref[...])
    out_ref[...] = out


def _combine(x, lpos, gates, win, yb, fw, *, tm, final_norm):
    t, d = x.shape
    nt = t // tm
    rt = d // LANES

    def smem(arr, imap):
        return pl.BlockSpec((1, 1, arr.shape[2]), imap, memory_space=pltpu.SMEM)

    return pl.pallas_call(
        functools.partial(_combine_body, final_norm=final_norm),
        grid=(nt,),
        in_specs=[
            smem(lpos, lambda i: (i, 0, 0)),
            smem(gates, lambda i: (i, 0, 0)),
            smem(win, lambda i: (i, 0, 0)),
            smem(win, lambda i: (jnp.minimum(i + 1, nt - 1), 0, 0)),
            pl.BlockSpec((tm, d), lambda i: (i, 0)),
            _full(fw.shape),
            pl.BlockSpec(memory_space=pl.ANY),
        ],
        out_specs=pl.BlockSpec((tm, d), lambda i: (i, 0)),
        out_shape=jax.ShapeDtypeStruct((t, d), F32),
        scratch_shapes=[pltpu.VMEM((2 * _ybuf_tokens(tm) * rt, LANES), F32),
                        pltpu.VMEM((tm * rt, LANES), F32),
                        pltpu.SemaphoreType.DMA((2,))],
        compiler_params=_cparams(("arbitrary",)),
        name="moe_combine",
    )(lpos, gates, win, win, x, fw, yb)


def _moe(x, nw, w_group, b_group, w_expert, b_expert, wg, wu, wd, fw, *, layer, final_norm):
    t, d = x.shape
    tm = min(512, t)
    nt = t // tm
    pad = LANES - N_EXPERTS - N_EXPERT_GROUPS
    wr = jnp.pad(jnp.concatenate([w_expert, w_group], axis=1), ((0, 0), (0, pad)))
    wr_hi = wr.astype(BF16)
    wr = jnp.concatenate([wr_hi, (wr - wr_hi.astype(F32)).astype(BF16)], axis=1)
    br = jnp.pad(jnp.concatenate([b_expert, b_group]), (0, pad)).reshape(1, LANES)
    nw2 = nw.reshape(1, d)
    info, cnt = _route(x, nw2, wr, br, tm=tm)

    gates = info[:, 0:TOP_K]
    eid = info[:, 2:2 + TOP_K].astype(I32)
    rank = info[:, 4:4 + TOP_K].astype(I32)
    cnt = cnt[:, 0, :N_EXPERTS].astype(I32)
    total = jnp.sum(cnt, axis=0)
    padded = jnp.where(total > 0, (total + SEG_W + MOE_BLOCK - 2) // MOE_BLOCK * MOE_BLOCK, 0)
    pend = jnp.cumsum(padded)
    gstart = (pend - padded)[None, :] + jnp.cumsum(cnt, axis=0) - cnt
    lstart = jnp.cumsum(cnt, axis=1) - cnt
    nwin_e = (cnt + SEG_W - 1) // SEG_W
    lstart_al = (jnp.cumsum(nwin_e, axis=1) - nwin_e) * SEG_W
    winc = jnp.cumsum(nwin_e, axis=1)
    nwin = winc[:, -1:]
    mw = _max_windows(tm)
    j = jnp.arange(mw, dtype=I32)[None, :]
    e_j = jnp.minimum(jnp.sum(winc[:, None, :] <= j[:, :, None], axis=-1), N_EXPERTS - 1)
    pick = lambda tab: jnp.take_along_axis(tab, e_j, axis=1)
    w_off = (j - pick(winc - nwin_e)) * SEG_W
    live = j < nwin
    src_loc = jnp.where(live, pick(lstart) + w_off, 0)
    slot_g = jnp.where(live, pick(gstart) + w_off, 0)
    dst_loc = jnp.where(live, pick(lstart_al) + w_off, 0)
    nprev = jnp.concatenate([jnp.zeros((1, 1), I32), nwin[:-1]], axis=0)
    win_d = jnp.concatenate([nwin, nprev, src_loc, slot_g], axis=1).reshape(nt, 1, WIN_HDR + 2 * mw)
    win_c = jnp.concatenate([nwin, nprev, slot_g, dst_loc], axis=1).reshape(nt, 1, WIN_HDR + 2 * mw)
    hot = eid[:, :, None] == jnp.arange(N_EXPERTS, dtype=I32)[None, None, :]
    per_tok = lambda tab: (jnp.sum(jnp.where(hot, jnp.repeat(tab, tm, axis=0)[:, None, :], 0), axis=-1)
                           + rank).reshape(nt, 1, tm * TOP_K)
    lpos = per_tok(lstart)
    lpos_al = per_tok(lstart_al)
    n_blocks = -(-(t * TOP_K + N_EXPERTS * (MOE_BLOCK + SEG_W - 2)) // MOE_BLOCK)
    blk_exp = jnp.minimum(jnp.sum(pend[None, :] <= (jnp.arange(n_blocks, dtype=I32) * MOE_BLOCK)[:, None], axis=1),
                          N_EXPERTS - 1).astype(I32)
    n_used = (pend[-1] // MOE_BLOCK).astype(I32).reshape(1)

    xb = _dispatch(x, nw2, lpos, win_d, jnp.zeros((n_blocks * MOE_BLOCK * (d // LANES), LANES), F32), tm=tm)
    yb = _experts(blk_exp, n_used, xb, wg, wu, wd, layer=layer)
    return _combine(x, lpos_al, gates.reshape(nt, 1, tm * TOP_K), win_c, yb, fw.reshape(1, d),
                    tm=tm, final_norm=final_norm)


def _linear_body(x_ref, nw_ref, w_ref, b_ref, r_ref, out_ref, *, norm):
    x = x_ref[...]
    if norm:
        x = _rms(x, nw_ref[...])
    out_ref[...] = _bdot(x, w_ref[...]) + b_ref[...] + r_ref[...]


def _linear(x, w, *, nw=None, bias=None, res=None, tn):
    m, kd = x.shape
    n = w.shape[1]
    norm = nw is not None
    nw = jnp.ones((1, kd), F32) if nw is None else nw
    bias = jnp.zeros((1, n), F32) if bias is None else bias
    res = jnp.zeros((m, n), F32) if res is None else res
    tn = min(tn, n)
    return pl.pallas_call(
        functools.partial(_linear_body, norm=norm),
        grid=(n // tn,),
        in_specs=[
            _full(x.shape), _full(nw.shape),
            pl.BlockSpec((kd, tn), lambda j: (0, j)),
            pl.BlockSpec((1, tn), lambda j: (0, j)),
            pl.BlockSpec((m, tn), lambda j: (0, j)),
        ],
        out_specs=pl.BlockSpec((m, tn), lambda j: (0, j)),
        out_shape=jax.ShapeDtypeStruct((m, n), F32),
        compiler_params=_cparams(("arbitrary",)),
        name="sample_linear",
    )(x, nw, w, bias, res)


def _sample_conv_body(xbc_ref, st_ref, cw_ref, cb_ref, dtr_ref, dtb_ref, xc_ref, stn_ref, dt_ref, cbg_ref,
                      *, d_inner):
    nst = D_STATE
    xbc = xbc_ref[...]
    acc = cb_ref[...] + cw_ref[3:4, :] * xbc
    for k in range(CONV_W - 1):
        acc = acc + cw_ref[k:k + 1, :] * st_ref[k]
    xc = _silu(acc)
    xc_ref[...] = xc
    stn_ref[0] = st_ref[1]
    stn_ref[1] = st_ref[2]
    stn_ref[2] = xbc
    dt_ref[...] = _softplus(dtr_ref[...] + dtb_ref[...])
    lane = lax.broadcasted_iota(I32, (xbc.shape[0], LANES), 1)
    cbg = jnp.zeros((xbc.shape[0], LANES), F32)
    for g in range(N_BC_GROUPS):
        b_g = xc[:, d_inner + g * nst:d_inner + (g + 1) * nst]
        c_g = xc[:, d_inner + (N_BC_GROUPS + g) * nst:d_inner + (N_BC_GROUPS + g + 1) * nst]
        cbg = jnp.where(lane == g, jnp.sum(b_g * c_g, axis=-1, keepdims=True), cbg)
    cbg_ref[...] = cbg


def _sample_conv(xbc, st_t, cw, cb, dtr, dtb, *, d_inner):
    m, cd = xbc.shape
    return pl.pallas_call(
        functools.partial(_sample_conv_body, d_inner=d_inner),
        grid=(1,),
        in_specs=[_full(xbc.shape), _full(st_t.shape), _full(cw.shape), _full(cb.shape), _full(dtr.shape),
                  _full(dtb.shape)],
        out_specs=[_full((m, cd)), _full(st_t.shape), _full((m, LANES)), _full((m, LANES))],
        out_shape=[jax.ShapeDtypeStruct((m, cd), F32), jax.ShapeDtypeStruct(st_t.shape, F32),
                   jax.ShapeDtypeStruct((m, LANES), F32), jax.ShapeDtypeStruct((m, LANES), F32)],
        compiler_params=_cparams(("arbitrary",)),
        name="sample_conv",
    )(xbc, st_t, cw, cb, dtr, dtb)


def _sample_ssd_body(s0_ref, xt_ref, bc_ref, hs_ref, par_ref, sn_ref, yt_ref, *, n_heads):
    hp = SSM_HEAD_DIM
    hpg = n_heads // N_BC_GROUPS
    xt = xt_ref[0]
    dt = hs_ref[0, 0:1, :]
    cbh = hs_ref[0, 1:2, :]
    a = -jnp.exp(par_ref[0:1, :])
    dsk = par_ref[1:2, :]
    dec = jnp.exp(dt * a)
    xdt = xt * dt
    lane = lax.broadcasted_iota(I32, (hp, n_heads), 1)
    yoff = jnp.zeros((hp, n_heads), F32)
    for hh in range(n_heads):
        g = hh // hpg
        b_row = bc_ref[0, g:g + 1, :]
        c_row = bc_ref[0, N_BC_GROUPS + g:N_BC_GROUPS + g + 1, :]
        s0 = s0_ref[0, hh]
        yo = jnp.sum(s0 * c_row, axis=-1, keepdims=True)
        yoff = jnp.where(lane == hh, yo, yoff)
        sn_ref[0, hh] = s0 * dec[:, hh:hh + 1] + xdt[:, hh:hh + 1] * b_row
    yt_ref[0] = cbh * xdt + yoff * dec + xt * dsk


def _sample_ssd(s0, xt, bc, hs, par):
    bsz, n_heads, hp, nst = s0.shape
    return pl.pallas_call(
        functools.partial(_sample_ssd_body, n_heads=n_heads),
        grid=(bsz,),
        in_specs=[
            pl.BlockSpec((1, n_heads, hp, nst), lambda b: (b, 0, 0, 0)),
            pl.BlockSpec((1, hp, n_heads), lambda b: (b, 0, 0)),
            pl.BlockSpec((1,) + bc.shape[1:], lambda b: (b, 0, 0)),
            pl.BlockSpec((1,) + hs.shape[1:], lambda b: (b, 0, 0)),
            _full(par.shape),
        ],
        out_specs=[
            pl.BlockSpec((1, n_heads, hp, nst), lambda b: (b, 0, 0, 0)),
            pl.BlockSpec((1, hp, n_heads), lambda b: (b, 0, 0)),
        ],
        out_shape=[jax.ShapeDtypeStruct(s0.shape, F32), jax.ShapeDtypeStruct((bsz, hp, n_heads), F32)],
        compiler_params=_cparams(("arbitrary",)),
        name="sample_ssd",
    )(s0, xt, bc, hs, par)


def _sample_gnorm_out_body(y_ref, z_ref, gnw_ref, wout_ref, x_ref, out_ref, *, d_inner):
    gw = d_inner // N_BC_GROUPS
    acc = x_ref[...]
    for g in range(N_BC_GROUPS):
        glanes = slice(g * gw, (g + 1) * gw)
        gg = y_ref[:, glanes] * _silu(z_ref[:, glanes])
        gg = gg * lax.rsqrt(jnp.mean(gg * gg, axis=-1, keepdims=True) + EPS) * gnw_ref[:, glanes]
        acc = acc + jnp.dot(gg.astype(BF16), wout_ref[glanes, :], preferred_element_type=F32)
    out_ref[...] = acc


def _sample_gnorm_out(y, z, gnw, wout, x):
    d_inner = y.shape[1]
    return pl.pallas_call(
        functools.partial(_sample_gnorm_out_body, d_inner=d_inner),
        grid=(1,),
        in_specs=[_full(y.shape), _full(z.shape), _full(gnw.shape), _full(wout.shape), _full(x.shape)],
        out_specs=_full(x.shape),
        out_shape=jax.ShapeDtypeStruct(x.shape, F32),
        compiler_params=_cparams(("arbitrary",)),
        name="sample_gnorm_out",
    )(y, z, gnw, wout, x)


def _sample_attn_body(q_ref, kn_ref, vn_ref, kc_ref, vc_ref, sink_ref, o_ref, ko_ref, vo_ref):
    bt = q_ref.shape[0]
    wb = kc_ref.shape[1]
    hd = HEAD_DIM
    qpk = N_Q_HEADS // N_KV_HEADS
    scale = hd ** -0.5
    for b in range(bt):
        kc = kc_ref[b]
        vc = vc_ref[b]
        kn = kn_ref[b]
        vn = vn_ref[b]
        for g in range(N_KV_HEADS):
            cols = slice(g * hd, (g + 1) * hd)
            q_g = q_ref[b, g * qpk:(g + 1) * qpk, :]
            sink = sink_ref[g * qpk:(g + 1) * qpk, :]
            s = _bdot_nt(q_g, kc[:, cols]) * scale
            s_new = jnp.sum(q_g * kn[:, cols], axis=-1, keepdims=True) * scale
            m = jnp.maximum(jnp.maximum(jnp.max(s, axis=-1, keepdims=True), s_new), sink)
            p = jnp.exp(s - m)
            p_new = jnp.exp(s_new - m)
            denom = jnp.sum(p, axis=-1, keepdims=True) + p_new + jnp.exp(sink - m)
            pv = _bdot(p, vc[:, cols]) + p_new * vn[:, cols]
            o_ref[b, g * qpk:(g + 1) * qpk, :] = pv / denom
        ko_ref[b, 0:wb - 1, :] = kc[1:wb, :]
        ko_ref[b, wb - 1:wb, :] = kn
        vo_ref[b, 0:wb - 1, :] = vc[1:wb, :]
        vo_ref[b, wb - 1:wb, :] = vn


def _sample_attn(q3, kn, vn, kc, vc, sinks, *, bt):
    bsz, nqh, hd = q3.shape
    wb, nk = kc.shape[1], kc.shape[2]
    return pl.pallas_call(
        _sample_attn_body,
        grid=(bsz // bt,),
        in_specs=[
            pl.BlockSpec((bt, nqh, hd), lambda i: (i, 0, 0)),
            pl.BlockSpec((bt, 1, nk), lambda i: (i, 0, 0)),
            pl.BlockSpec((bt, 1, nk), lambda i: (i, 0, 0)),
            pl.BlockSpec((bt, wb, nk), lambda i: (i, 0, 0)),
            pl.BlockSpec((bt, wb, nk), lambda i: (i, 0, 0)),
            _full(sinks.shape),
        ],
        out_specs=[
            pl.BlockSpec((bt, nqh, hd), lambda i: (i, 0, 0)),
            pl.BlockSpec((bt, wb, nk), lambda i: (i, 0, 0)),
            pl.BlockSpec((bt, wb, nk), lambda i: (i, 0, 0)),
        ],
        out_shape=[jax.ShapeDtypeStruct(q3.shape, F32), jax.ShapeDtypeStruct(kc.shape, F32),
                   jax.ShapeDtypeStruct(vc.shape, F32)],
        compiler_params=_cparams(("arbitrary",)),
        name="sample_attn",
    )(q3, kn, vn, kc, vc, sinks)


def _mamba_sample(x, nw, mw, state_conv, state_ssm):
    win, cw, cb, dtb, alog, dsk, gnw, wout = mw
    bsz, d = x.shape
    d_inner = wout.shape[0]
    conv_dim = cw.shape[1]
    n_heads = d_inner // SSM_HEAD_DIM
    hp = SSM_HEAD_DIM
    proj = _linear(x, win, nw=nw, tn=896)
    z = proj[:, :d_inner]
    xbc = proj[:, d_inner:d_inner + conv_dim]
    dtr = proj[:, d_inner + conv_dim:]
    st_t = jnp.transpose(state_conv, (1, 0, 2))
    xc, stn_t, dt, cbg = _sample_conv(xbc, st_t, cw, cb, dtr, dtb, d_inner=d_inner)
    conv_new = jnp.transpose(stn_t, (1, 0, 2))
    xt = jnp.transpose(xc[:, :d_inner].reshape(bsz, n_heads, hp), (0, 2, 1))
    bc = xc[:, d_inner:].reshape(bsz, 2 * N_BC_GROUPS, D_STATE)
    cbh = jnp.repeat(cbg[:, :N_BC_GROUPS], n_heads // N_BC_GROUPS, axis=1)
    hs = jnp.stack([dt[:, :n_heads], cbh], axis=1)
    par = jnp.stack([alog[0, :n_heads], dsk.reshape(n_heads, hp)[:, 0]], axis=0)
    ssm_new, yt = _sample_ssd(state_ssm, xt, bc, hs, par)
    y = jnp.transpose(yt, (0, 2, 1)).reshape(bsz, d_inner)
    out = _sample_gnorm_out(y, z, gnw, wout, x)
    return out, conv_new, ssm_new


def _attn_sample(x, nw, wqkv, bqkv, sinks, wo, bo, cache_k, cache_v):
    bsz, d = x.shape
    wb = cache_k.shape[1]
    nq = N_Q_HEADS * HEAD_DIM
    nk = N_KV_HEADS * HEAD_DIM
    qkv = _linear(x, wqkv, nw=nw, bias=bqkv, tn=512)
    q3 = qkv[:, :nq].reshape(bsz, N_Q_HEADS, HEAD_DIM)
    kn = qkv[:, nq:nq + nk].reshape(bsz, 1, nk)
    vn = qkv[:, nq + nk:].reshape(bsz, 1, nk)
    o3, ko, vo = _sample_attn(q3, kn, vn, cache_k.reshape(bsz, wb, nk), cache_v.reshape(bsz, wb, nk),
                              sinks.reshape(N_Q_HEADS, 1), bt=8)
    out = _linear(o3.reshape(bsz, nq), wo, bias=bo, res=x, tn=512)
    return out, ko.reshape(cache_k.shape), vo.reshape(cache_v.shape)


def kernel(x_prompt, x_sample, state_ssm, state_conv, cache_k_win, cache_v_win,
           mamba_w_in, mamba_conv_w, mamba_conv_b, mamba_dt_bias, mamba_a_log, mamba_d,
           mamba_norm_w, mamba_w_out, attn_w_qkv, attn_b_qkv, attn_sinks, attn_w_o, attn_b_o,
           norm_mix, norm_ffn, router_w_group, router_b_group, router_w_expert, router_b_expert,
           expert_w_gate, expert_w_up, expert_w_down, norm_final):
    bsz, seq, d = x_prompt.shape
    dbsz, dseq, _ = x_sample.shape
    assert dseq == 1 and cache_k_win.shape[2] <= WINDOW and seq % WINDOW == 0
    depth = norm_mix.shape[0]
    xp = x_prompt.reshape(bsz * seq, d)
    xs = x_sample.reshape(dbsz, d)
    ssm_p, conv_p, kp_l, vp_l = [], [], [], []
    ssm_s, conv_s, ks_l, vs_l = [], [], [], []
    for i in range(depth):
        j = i // 2
        nw = norm_mix[i].reshape(1, d)
        if i % 2 == 0:
            mw = _mamba_weights(mamba_w_in[j], mamba_conv_w[j], mamba_conv_b[j], mamba_dt_bias[j],
                                mamba_a_log[j], mamba_d[j], mamba_norm_w[j], mamba_w_out[j])
            xp, cp, sp = _mamba_prompt(xp, nw, *mw, bsz=bsz, seq=seq, ts=2 * SSD_CHUNK)
            xs, cs_, ss_ = _mamba_sample(xs, nw, mw, state_conv[j], state_ssm[j])
            ssm_p.append(sp)
            conv_p.append(cp)
            ssm_s.append(ss_)
            conv_s.append(cs_)
        else:
            wqkv = attn_w_qkv[j].astype(BF16)
            bqkv = attn_b_qkv[j].reshape(1, -1)
            wo = attn_w_o[j].astype(BF16)
            bo = attn_b_o[j].reshape(1, d)
            xp, kp, vp = _attn_prompt(xp, attn_sinks[j], nw, wqkv, bqkv, wo, bo, bsz=bsz, seq=seq,
                                      tq=WINDOW)
            xs, ks_, vs_ = _attn_sample(xs, nw, wqkv, bqkv, attn_sinks[j], wo, bo, cache_k_win[j], cache_v_win[j])
            kp_l.append(kp.reshape(bsz, WINDOW, N_KV_HEADS, HEAD_DIM))
            vp_l.append(vp.reshape(bsz, WINDOW, N_KV_HEADS, HEAD_DIM))
            ks_l.append(ks_)
            vs_l.append(vs_)
        last = i == depth - 1
        moe_w = (norm_ffn[i], router_w_group[i], router_b_group[i], router_w_expert[i], router_b_expert[i],
                 expert_w_gate, expert_w_up, expert_w_down, norm_final)
        xp = _moe(xp, *moe_w, layer=i, final_norm=last)
        xs = _moe(xs, *moe_w, layer=i, final_norm=last)
    return (xp.reshape(bsz, seq, d), xs.reshape(dbsz, dseq, d),
            jnp.stack(ssm_p), jnp.stack(conv_p), jnp.stack(kp_l), jnp.stack(vp_l),
            jnp.stack(ssm_s), jnp.stack(conv_s), jnp.stack(ks_l), jnp.stack(vs_l))
```

```python
import functools
import math

import jax
import jax.numpy as jnp
from jax import lax
from jax.experimental import pallas as pl
from jax.experimental.pallas import tpu as pltpu

F32 = jnp.float32
BF16 = jnp.bfloat16
I32 = jnp.int32

EPS = 1e-5
LANES = 128
VMEM_LIMIT = 56 * 1024 * 1024

SSM_HEAD_DIM = 64
D_STATE = 128
N_BC_GROUPS = 8
CONV_W = 4
SSD_CHUNK = 128
N_Q_HEADS = 16
N_KV_HEADS = 4
HEAD_DIM = 64
WINDOW = 128
N_EXPERT_GROUPS = 4
EXPERTS_PER_GROUP = 8
N_EXPERTS = N_EXPERT_GROUPS * EXPERTS_PER_GROUP
TOP_K = 2
MOE_BLOCK = 256
MOE_TILE = 512


def _cparams(sem):
    return pltpu.CompilerParams(dimension_semantics=sem, vmem_limit_bytes=VMEM_LIMIT)


def _full(shape):
    n = len(shape)
    return pl.BlockSpec(shape, lambda *_: (0,) * n)


def _resident(shape):
    n = len(shape)
    return pl.BlockSpec(shape, lambda *_: (0,) * n, pipeline_mode=pl.Buffered(1))


def _rms(x, w):
    return x * lax.rsqrt(jnp.mean(x * x, axis=-1, keepdims=True) + EPS) * w


def _silu(x):
    return x / (1.0 + jnp.exp(-x))


def _softplus(x):
    return jnp.maximum(x, 0.0) + jnp.log(1.0 + jnp.exp(-jnp.abs(x)))


def _bdot(a, b):
    return jnp.dot(a.astype(BF16), b.astype(BF16), preferred_element_type=F32)


def _bdot_nt(a, b):
    return lax.dot_general(a.astype(BF16), b.astype(BF16), (((1,), (1,)), ((), ())),
                           preferred_element_type=F32)


def _bdot_tn(a, b):
    return lax.dot_general(a.astype(BF16), b.astype(BF16), (((0,), (0,)), ((), ())),
                           preferred_element_type=F32)


def _fdot(a, b):
    return jnp.dot(a, b, preferred_element_type=F32, precision=lax.Precision.HIGHEST)


def _mamba_prompt_body(x_ref, nw_ref, win_ref, cw_ref, cb_ref, dtb_ref, alog_ref, dsk_ref, gnw_ref,
                       wout_ref, out_ref, conv_ref, ssm_ref, h_buf, xbc_buf, xc_buf, st_buf, y_buf,
                       *, d_inner, n_heads):
    ts = x_ref.shape[0]
    cs = SSD_CHUNK
    hp = SSM_HEAD_DIM
    nst = D_STATE
    gw = d_inner // N_BC_GROUPS
    hpg = n_heads // N_BC_GROUPS
    conv_dim = d_inner + 2 * N_BC_GROUPS * nst
    s = pl.program_id(1)

    @pl.when(s == 0)
    def _():
        xbc_buf[0:8, :] = jnp.zeros((8, conv_dim), F32)
        st_buf[...] = jnp.zeros_like(st_buf)

    h_buf[...] = _rms(x_ref[...], nw_ref[...]).astype(BF16)
    dtr = jnp.dot(h_buf[...], win_ref[:, d_inner + conv_dim:], preferred_element_type=F32)
    ct = 512
    for j in range(conv_dim // ct):
        cols = slice(j * ct, (j + 1) * ct)
        xbc_buf[8:8 + ts, cols] = jnp.dot(h_buf[...], win_ref[:, d_inner + j * ct:d_inner + (j + 1) * ct],
                                          preferred_element_type=F32)
        acc = cb_ref[:, cols] + cw_ref[3:4, cols] * xbc_buf[8:8 + ts, cols]
        acc = acc + cw_ref[2:3, cols] * xbc_buf[7:7 + ts, cols]
        acc = acc + cw_ref[1:2, cols] * xbc_buf[6:6 + ts, cols]
        acc = acc + cw_ref[0:1, cols] * xbc_buf[5:5 + ts, cols]
        xc_buf[:, cols] = _silu(acc)
    last3 = xbc_buf[5 + ts:8 + ts, :]
    xbc_buf[5:8, :] = last3
    conv_ref[0] = last3

    dt = _softplus(dtr + dtb_ref[...])
    da = dt * (-jnp.exp(alog_ref[...]))
    row = lax.broadcasted_iota(I32, (cs, cs), 0)
    col = lax.broadcasted_iota(I32, (cs, cs), 1)
    causal = row >= col
    tril = causal.astype(F32)
    lane = lax.broadcasted_iota(I32, (cs, LANES), 1)
    lo_half = lane < hp

    for c in range(ts // cs):
        rows = slice(c * cs, (c + 1) * cs)
        da_c = da[rows]
        dt_c = dt[rows]
        acum = _fdot(tril, da_c)
        acum_t = acum.T
        dt_t = dt_c.T
        a_last = acum[cs - 1:cs, :]
        to_end = jnp.exp(a_last - acum)
        w_all = dt_c * to_end
        ea = jnp.exp(acum)
        cd = jnp.exp(a_last)
        for g in range(N_BC_GROUPS):
            b_g = xc_buf[rows, d_inner + g * nst:d_inner + (g + 1) * nst]
            c_g = xc_buf[rows, d_inner + (N_BC_GROUPS + g) * nst:d_inner + (N_BC_GROUPS + g + 1) * nst]
            cb = _bdot_nt(c_g, b_g)
            xw_parts = []
            for pr in range(hpg // 2):
                h0 = g * hpg + 2 * pr
                lanes0 = slice(h0 * hp, h0 * hp + 2 * hp)
                x_pair = xc_buf[rows, lanes0]
                st_pair = st_buf[:, lanes0]
                y_pair = jnp.zeros((cs, 2 * hp), F32)
                w_pair = jnp.zeros((cs, 2 * hp), F32)
                for k in range(2):
                    hh = h0 + k
                    seg = acum[:, hh:hh + 1] - acum_t[hh:hh + 1, :]
                    dec = jnp.exp(jnp.where(causal, seg, -jnp.inf))
                    m = cb * dec * dt_t[hh:hh + 1, :]
                    sel = lo_half if k == 0 else jnp.logical_not(lo_half)
                    x_k = jnp.where(sel, x_pair, 0.0)
                    st_k = jnp.where(sel, st_pair, 0.0)
                    lhs = jnp.concatenate([m, c_g * ea[:, hh:hh + 1]], axis=1)
                    rhs = jnp.concatenate([x_k, st_k], axis=0)
                    y_pair = y_pair + _bdot(lhs, rhs)
                    w_pair = jnp.where(sel, w_all[:, hh:hh + 1], w_pair)
                y_buf[rows, lanes0] = y_pair + x_pair * dsk_ref[:, lanes0]
                xw_parts.append(x_pair * w_pair)
            xw = jnp.concatenate(xw_parts, axis=1)
            glanes = slice(g * gw, (g + 1) * gw)
            cd_parts = [jnp.broadcast_to(cd[:, g * hpg + k:g * hpg + k + 1], (1, hp)) for k in range(hpg)]
            cd_g = jnp.concatenate(cd_parts, axis=1)
            st_buf[:, glanes] = st_buf[:, glanes] * cd_g + _bdot_tn(b_g, xw)

    @pl.when(s == pl.num_programs(1) - 1)
    def _():
        for pr in range(n_heads // 2):
            t = st_buf[:, 2 * pr * hp:2 * (pr + 1) * hp].T
            ssm_ref[0, 2 * pr] = t[0:hp]
            ssm_ref[0, 2 * pr + 1] = t[hp:2 * hp]

    acc = x_ref[...]
    for g in range(N_BC_GROUPS):
        glanes = slice(g * gw, (g + 1) * gw)
        z = jnp.dot(h_buf[...], win_ref[:, glanes], preferred_element_type=F32)
        gg = y_buf[:, glanes] * _silu(z)
        gg = gg * lax.rsqrt(jnp.mean(gg * gg, axis=-1, keepdims=True) + EPS) * gnw_ref[:, glanes]
        acc = acc + jnp.dot(gg.astype(BF16), wout_ref[glanes, :], preferred_element_type=F32)
    out_ref[...] = acc


def _mamba_prompt(x, nw, win, cw, cb, dtb, alog, dsk, gnw, wout, *, bsz, seq, ts):
    d = x.shape[1]
    d_inner = wout.shape[0]
    n_heads = d_inner // SSM_HEAD_DIM
    conv_dim = cw.shape[1]
    ns = seq // ts
    body = functools.partial(_mamba_prompt_body, d_inner=d_inner, n_heads=n_heads)
    return pl.pallas_call(
        body,
        grid=(bsz, ns),
        in_specs=[
            pl.BlockSpec((ts, d), lambda b, s: (b * ns + s, 0)),
            _full(nw.shape), _resident(win.shape), _full(cw.shape), _full(cb.shape), _full(dtb.shape),
            _full(alog.shape), _full(dsk.shape), _full(gnw.shape), _resident(wout.shape),
        ],
        out_specs=[
            pl.BlockSpec((ts, d), lambda b, s: (b * ns + s, 0)),
            pl.BlockSpec((1, CONV_W - 1, conv_dim), lambda b, s: (b, 0, 0)),
            pl.BlockSpec((1, n_heads, SSM_HEAD_DIM, D_STATE), lambda b, s: (b, 0, 0, 0)),
        ],
        out_shape=[
            jax.ShapeDtypeStruct((bsz * seq, d), F32),
            jax.ShapeDtypeStruct((bsz, CONV_W - 1, conv_dim), F32),
            jax.ShapeDtypeStruct((bsz, n_heads, SSM_HEAD_DIM, D_STATE), F32),
        ],
        scratch_shapes=[
            pltpu.VMEM((ts, d), BF16),
            pltpu.VMEM((8 + ts, conv_dim), F32),
            pltpu.VMEM((ts, conv_dim), F32),
            pltpu.VMEM((D_STATE, d_inner), F32),
            pltpu.VMEM((ts, d_inner), F32),
        ],
        compiler_params=_cparams(("arbitrary", "arbitrary")),
        name="mamba_prompt",
    )(x, nw, win, cw, cb, dtb, alog, dsk, gnw, wout)


def _mamba_weights(w_in, conv_w, conv_b, dt_bias, a_log, d_skip, norm_w, w_out):
    d_inner = w_out.shape[0]
    n_heads = dt_bias.shape[0]
    pad = LANES - n_heads
    win = jnp.pad(w_in, ((0, 0), (0, pad))).astype(BF16)
    dtb = jnp.pad(dt_bias, (0, pad)).reshape(1, LANES)
    alog = jnp.pad(a_log, (0, pad)).reshape(1, LANES)
    dsk = jnp.repeat(d_skip, SSM_HEAD_DIM).reshape(1, d_inner)
    return (win, conv_w, conv_b.reshape(1, -1), dtb, alog, dsk, norm_w.reshape(1, d_inner),
            w_out.astype(BF16))


def _sink_softmax_pv(s, sink, v):
    m = jnp.maximum(jnp.max(s, axis=-1, keepdims=True), sink)
    p = jnp.exp(s - m)
    denom = jnp.sum(p, axis=-1, keepdims=True) + jnp.exp(sink - m)
    return _bdot(p, v) / denom


def _attn_prompt_body(sink_ref, x_ref, nw_ref, wqkv_ref, bqkv_ref, wo_ref, bo_ref,
                      out_ref, kwin_ref, vwin_ref, kv_buf, q_buf, o_buf):
    blk = WINDOW
    hd = HEAD_DIM
    nq = N_Q_HEADS * hd
    nk = N_KV_HEADS * hd
    qpk = N_Q_HEADS // N_KV_HEADS
    tq = x_ref.shape[0]
    s_id = pl.program_id(1)

    @pl.when(s_id == 0)
    def _():
        kv_buf[0:blk, :] = jnp.zeros((blk, 2 * nk), F32)

    h = _rms(x_ref[...], nw_ref[...]).astype(BF16)
    q_buf[...] = jnp.dot(h, wqkv_ref[:, 0:nq], preferred_element_type=F32) + bqkv_ref[:, 0:nq]
    kv_buf[blk:blk + tq, :] = jnp.dot(h, wqkv_ref[:, nq:], preferred_element_type=F32) + bqkv_ref[:, nq:]
    kwin_ref[0] = kv_buf[tq:tq + blk, 0:nk]
    vwin_ref[0] = kv_buf[tq:tq + blk, nk:]

    row = lax.broadcasted_iota(I32, (blk, 2 * blk), 0)
    col = lax.broadcasted_iota(I32, (blk, 2 * blk), 1)
    diff = row + blk - col
    band = (diff >= 0) & (diff <= WINDOW)
    scale = hd ** -0.5
    for qb in range(tq // blk):
        qrows = slice(qb * blk, (qb + 1) * blk)
        krows = slice(qb * blk, (qb + 2) * blk)
        ok = band & ((col >= blk) | (s_id > 0)) if qb == 0 else band
        for g in range(N_KV_HEADS):
            k_g = kv_buf[krows, g * hd:(g + 1) * hd]
            v_g = kv_buf[krows, nk + g * hd:nk + (g + 1) * hd]
            for j in range(qpk):
                hh = g * qpk + j
                s = _bdot_nt(q_buf[qrows, hh * hd:(hh + 1) * hd], k_g) * scale
                s = jnp.where(ok, s, -jnp.inf)
                o_buf[qrows, hh * hd:(hh + 1) * hd] = _sink_softmax_pv(s, sink_ref[hh], v_g)
    kv_buf[0:blk, :] = kv_buf[tq:tq + blk, :]
    out_ref[...] = (x_ref[...] + jnp.dot(o_buf[...].astype(BF16), wo_ref[...], preferred_element_type=F32)
                    + bo_ref[...])


def _attn_prompt(x, sinks, nw, wqkv, bqkv, wo, bo, *, bsz, seq, tq):
    d = x.shape[1]
    blk = WINDOW
    nb = seq // tq
    nk = N_KV_HEADS * HEAD_DIM
    nq = N_Q_HEADS * HEAD_DIM
    return pl.pallas_call(
        _attn_prompt_body,
        grid=(bsz, nb),
        in_specs=[
            pl.BlockSpec(memory_space=pltpu.SMEM),
            pl.BlockSpec((tq, d), lambda b, s: (b * nb + s, 0)),
            _full(nw.shape), _full(wqkv.shape), _full(bqkv.shape), _full(wo.shape), _full(bo.shape),
        ],
        out_specs=[
            pl.BlockSpec((tq, d), lambda b, s: (b * nb + s, 0)),
            pl.BlockSpec((1, blk, nk), lambda b, s: (b, 0, 0)),
            pl.BlockSpec((1, blk, nk), lambda b, s: (b, 0, 0)),
        ],
        out_shape=[
            jax.ShapeDtypeStruct((bsz * seq, d), F32),
            jax.ShapeDtypeStruct((bsz, blk, nk), F32),
            jax.ShapeDtypeStruct((bsz, blk, nk), F32),
        ],
        scratch_shapes=[
            pltpu.VMEM((blk + tq, 2 * nk), F32),
            pltpu.VMEM((tq, nq), F32),
            pltpu.VMEM((tq, nq), F32),
        ],
        compiler_params=_cparams(("arbitrary", "arbitrary")),
        name="attn_prompt",
    )(sinks, x, nw, wqkv, bqkv, wo, bo)


def _x_specs(xm, tm):
    ntm, d = xm.shape[0] // tm, xm.shape[1]
    return [pl.BlockSpec((tm, d), lambda i: (jnp.minimum(i, ntm - 1), 0)), pl.BlockSpec((tm, d), lambda i: (0, 0))]


def _x_tile(xm_ref, xt_ref, ntm):
    return jnp.where(pl.program_id(0) < ntm, xm_ref[...], xt_ref[...])


def _route_body(xm_ref, xt_ref, nw_ref, wr_ref, br_ref, info_ref, cnt_ref, *, ntm):
    tm = xm_ref.shape[0]
    h = _rms(_x_tile(xm_ref, xt_ref, ntm), nw_ref[...])
    h_hi = h.astype(BF16)
    h_lo = (h - h_hi.astype(F32)).astype(BF16)
    part = jnp.dot(h_hi, wr_ref[...], preferred_element_type=F32)
    logits = (part[:, 0:LANES] + part[:, LANES:] + jnp.dot(h_lo, wr_ref[:, 0:LANES], preferred_element_type=F32)
              + br_ref[...])
    lane_i = lax.broadcasted_iota(I32, (tm, LANES), 1)
    lane = lane_i.astype(F32)
    lane_grp = (lane_i // EXPERTS_PER_GROUP).astype(F32)
    big = float(LANES)
    ninf = -jnp.inf

    def first_argmax(v):
        m = jnp.max(v, axis=-1, keepdims=True)
        return m, jnp.min(jnp.where(v == m, lane, big), axis=-1, keepdims=True)

    gmask = (lane_i >= N_EXPERTS) & (lane_i < N_EXPERTS + N_EXPERT_GROUPS)
    gl = jnp.where(gmask, logits, ninf)
    gmax, gi = first_argmax(gl)
    gi = gi - float(N_EXPERTS)
    pg = 1.0 / jnp.sum(jnp.exp(gl - gmax), axis=-1, keepdims=True)
    emask = (lane_i < N_EXPERTS) & (lane_grp == gi)
    el = jnp.where(emask, logits, ninf)
    m1, i1 = first_argmax(el)
    el2 = jnp.where(lane == i1, ninf, el)
    m2, i2 = first_argmax(el2)
    den = jnp.sum(jnp.exp(el - m1), axis=-1, keepdims=True)
    tp1 = 1.0 / den
    tp2 = jnp.exp(m2 - m1) / den
    g1 = pg * tp1 / (tp1 + tp2)
    g2 = pg * tp2 / (tp1 + tp2)
    hot1 = lane == i1
    hot2 = lane == i2
    onehot = jnp.where(hot1 | hot2, 1.0, 0.0)
    rr = lax.broadcasted_iota(I32, (tm, tm), 0)
    cc = lax.broadcasted_iota(I32, (tm, tm), 1)
    before = jnp.where(rr > cc, 1.0, 0.0)
    cum = _bdot(before, onehot)
    r1 = jnp.sum(jnp.where(hot1, cum, 0.0), axis=-1, keepdims=True)
    r2 = jnp.sum(jnp.where(hot2, cum, 0.0), axis=-1, keepdims=True)
    cnt_row = jnp.sum(onehot, axis=0, keepdims=True)
    nwin_row = jnp.floor((cnt_row + (SEG_W - 1.0)) * (1.0 / SEG_W))
    er = lax.broadcasted_iota(I32, (LANES, LANES), 0)
    ec = lax.broadcasted_iota(I32, (LANES, LANES), 1)
    earlier = jnp.where(er < ec, 1.0, 0.0)
    both = jnp.concatenate([jnp.broadcast_to(cnt_row, (8, LANES)), jnp.broadcast_to(nwin_row, (8, LANES))], axis=0)
    pre = _fdot(both, earlier)
    start = pre[0:1, :]
    start_al = pre[8:9, :] * float(SEG_W)

    def at(hot, row):
        return jnp.sum(jnp.where(hot, row, 0.0), axis=-1, keepdims=True)

    info = jnp.zeros((tm, LANES), F32)
    for k, v in enumerate((g1, g2, i1, i2, at(hot1, start) + r1, at(hot2, start) + r2,
                           at(hot1, start_al) + r1, at(hot2, start_al) + r2)):
        info = jnp.where(lane_i == k, v, info)
    info_ref[...] = info.T[0:8, :]
    cnt_ref[0] = jnp.broadcast_to(cnt_row, (8, LANES))


ROUTE_ROWS = 8


def _route(xm, xt, nw, wr, br, *, tm):
    ntm = xm.shape[0] // tm
    nt = ntm + 1
    return pl.pallas_call(
        functools.partial(_route_body, ntm=ntm),
        grid=(nt,),
        in_specs=_x_specs(xm, tm) + [_full(nw.shape), _full(wr.shape), _full(br.shape)],
        out_specs=[pl.BlockSpec((ROUTE_ROWS, tm), lambda i: (0, i)), pl.BlockSpec((1, 8, LANES), lambda i: (i, 0, 0))],
        out_shape=[jax.ShapeDtypeStruct((ROUTE_ROWS, nt * tm), F32), jax.ShapeDtypeStruct((nt, 8, LANES), F32)],
        compiler_params=_cparams(("arbitrary",)),
        name="moe_route",
    )(xm, xt, nw, wr, br)


def _to_tiles(ref, base, val):
    m, rt = val.shape[0], val.shape[1] // LANES
    for j in range(rt):
        ref[pl.ds(base * rt + j, m, stride=rt), :] = val[:, j * LANES:(j + 1) * LANES]


def _from_tiles(ref, base, m, rt):
    return jnp.concatenate([ref[pl.ds(base * rt + j, m, stride=rt), :] for j in range(rt)], axis=1)


SEG_W = 16
WIN_HDR = 2


def _max_windows(tm):
    return N_EXPERTS + TOP_K * tm // SEG_W


def _seg_copy(src, i, dst, j, sem, rt):
    n = SEG_W * rt
    return pltpu.make_async_copy(src.at[pl.ds(pl.multiple_of(i * rt, rt), n), :],
                                 dst.at[pl.ds(pl.multiple_of(j * rt, rt), n), :], sem)


def _tok(ref, p, rt):
    return ref.at[pl.ds(pl.multiple_of(p * rt, rt), rt), :]


def _dispatch_body(lpos_ref, win_ref, zwin_ref, xm_ref, xt_ref, nw_ref, xb_ref, h_buf, s_buf, sem, *, ntm):
    tm, rt = xm_ref.shape[0], xm_ref.shape[1] // LANES
    i = pl.program_id(0)
    half = TOP_K * tm + SEG_W
    sbase = (i % 2) * half
    mw = _max_windows(tm)

    @pl.when(i == 0)
    def _():
        for hb in range(2):
            s_buf[(hb * half + TOP_K * tm) * rt:(hb + 1) * half * rt, :] = jnp.zeros((SEG_W * rt, LANES), F32)

        def zissue(w, carry):
            _seg_copy(s_buf, TOP_K * tm, xb_ref, zwin_ref[1 + w], sem, rt).start()
            return carry

        def zdrain(w, carry):
            _seg_copy(s_buf, 0, xb_ref, 0, sem, rt).wait()
            return carry

        lax.fori_loop(0, zwin_ref[0], zissue, 0)
        lax.fori_loop(0, zwin_ref[0], zdrain, 0)

    _to_tiles(h_buf, 0, _rms(_x_tile(xm_ref, xt_ref, ntm), nw_ref[...]))

    def move(t, carry):
        v = _tok(h_buf, t, rt)[...]
        for k in range(TOP_K):
            _tok(s_buf, sbase + lpos_ref[0, 0, k * tm + t], rt)[...] = v
        return carry

    lax.fori_loop(0, tm, move, 0, unroll=8)

    def drain(w, carry):
        _seg_copy(s_buf, 0, xb_ref, 0, sem, rt).wait()
        return carry

    @pl.when(i > 0)
    def _():
        lax.fori_loop(0, win_ref[0, 0, 1], drain, 0)

    def issue(w, carry):
        _seg_copy(s_buf, sbase + win_ref[0, 0, WIN_HDR + w], xb_ref, win_ref[0, 0, WIN_HDR + mw + w], sem, rt).start()
        return carry

    lax.fori_loop(0, win_ref[0, 0, 0], issue, 0)

    @pl.when(i == pl.num_programs(0) - 1)
    def _():
        lax.fori_loop(0, win_ref[0, 0, 0], drain, 0)


def _dispatch(xm, xt, nw, lpos, win, zwin, *, tm, n_slots):
    d = xm.shape[1]
    ntm = xm.shape[0] // tm
    nt = ntm + 1
    rt = d // LANES
    return pl.pallas_call(
        functools.partial(_dispatch_body, ntm=ntm),
        grid=(nt,),
        in_specs=[
            pl.BlockSpec((1, 1, lpos.shape[2]), lambda i: (i, 0, 0), memory_space=pltpu.SMEM),
            pl.BlockSpec((1, 1, win.shape[2]), lambda i: (i, 0, 0), memory_space=pltpu.SMEM),
            pl.BlockSpec(memory_space=pltpu.SMEM),
        ] + _x_specs(xm, tm) + [
            _full(nw.shape),
        ],
        out_specs=pl.BlockSpec(memory_space=pl.ANY),
        out_shape=jax.ShapeDtypeStruct((n_slots * rt, LANES), F32),
        scratch_shapes=[pltpu.VMEM((tm * rt, LANES), F32),
                        pltpu.VMEM((2 * (TOP_K * tm + SEG_W) * rt, LANES), F32),
                        pltpu.SemaphoreType.DMA(())],
        compiler_params=_cparams(("arbitrary",)),
        name="moe_dispatch",
    )(lpos, win, zwin, xm, xt, nw)


def _expert_body(be_ref, nu_ref, xb_ref, wg_ref, wu_ref, wd_ref, yb_ref, wg_buf, wu_buf, wd_buf):
    b = pl.program_id(0)
    prev = be_ref[jnp.maximum(b - 1, 0)]
    fresh = (b == 0) | (be_ref[b] != prev)

    @pl.when((b < nu_ref[0]) & fresh)
    def _():
        wg_buf[...] = wg_ref[0, 0].astype(BF16)
        wu_buf[...] = wu_ref[0, 0].astype(BF16)
        wd_buf[...] = wd_ref[0, 0].astype(BF16)

    @pl.when(b < nu_ref[0])
    def _():
        xb = _from_tiles(xb_ref, 0, MOE_BLOCK, wg_buf.shape[0] // LANES).astype(BF16)
        gate = jnp.dot(xb, wg_buf[...], preferred_element_type=F32)
        up = jnp.dot(xb, wu_buf[...], preferred_element_type=F32)
        hid = (_silu(gate) * up).astype(BF16)
        _to_tiles(yb_ref, 0, jnp.dot(hid, wd_buf[...], preferred_element_type=F32))

    @pl.when(b >= nu_ref[0])
    def _():
        yb_ref[...] = jnp.zeros_like(yb_ref)


def _experts(blk_exp, n_used, xb, wg, wu, wd, *, layer):
    d, f = wg.shape[2], wg.shape[3]
    rt = d // LANES
    nb = xb.shape[0] // rt // MOE_BLOCK
    blk_rows = MOE_BLOCK * rt

    def xmap(b, be, nu):
        return (jnp.minimum(b, jnp.maximum(nu[0] - 1, 0)), 0)

    def wmap(b, be, nu):
        return (layer, be[b], 0, 0)

    return pl.pallas_call(
        _expert_body,
        grid_spec=pltpu.PrefetchScalarGridSpec(
            num_scalar_prefetch=2,
            grid=(nb,),
            in_specs=[
                pl.BlockSpec((blk_rows, LANES), xmap),
                pl.BlockSpec((1, 1, d, f), wmap), pl.BlockSpec((1, 1, d, f), wmap),
                pl.BlockSpec((1, 1, f, d), wmap),
            ],
            out_specs=pl.BlockSpec((blk_rows, LANES), lambda b, be, nu: (b, 0)),
            scratch_shapes=[pltpu.VMEM((d, f), BF16), pltpu.VMEM((d, f), BF16), pltpu.VMEM((f, d), BF16)],
        ),
        out_shape=jax.ShapeDtypeStruct(xb.shape, F32),
        compiler_params=_cparams(("arbitrary",)),
        name="moe_experts",
    )(blk_exp, n_used, xb, wg, wu, wd)


def _ybuf_tokens(tm):
    return TOP_K * tm + N_EXPERTS * (SEG_W - 1) + SEG_W


def _combine_body(lpos_ref, gate_ref, win_ref, winn_ref, xm_ref, xt_ref, fw_ref, yb_ref, om_ref, ot_ref,
                  y_buf, x_buf, sem, *, ntm, final_norm):
    tm, rt = xm_ref.shape[0], xm_ref.shape[1] // LANES
    i = pl.program_id(0)
    slot = i % 2
    half = _ybuf_tokens(tm)
    mw = _max_windows(tm)

    def fetch(tab_ref, sl):
        def issue(w, carry):
            _seg_copy(yb_ref, tab_ref[0, 0, WIN_HDR + w], y_buf, sl * half + tab_ref[0, 0, WIN_HDR + mw + w],
                      sem.at[sl], rt).start()
            return carry

        lax.fori_loop(0, tab_ref[0, 0, 0], issue, 0)

    @pl.when(i == 0)
    def _():
        fetch(win_ref, 0)

    @pl.when(i + 1 < pl.num_programs(0))
    def _():
        fetch(winn_ref, 1 - slot)

    _to_tiles(x_buf, 0, _x_tile(xm_ref, xt_ref, ntm))

    def drain(w, carry):
        _seg_copy(yb_ref, 0, y_buf, 0, sem.at[slot], rt).wait()
        return carry

    lax.fori_loop(0, win_ref[0, 0, 0], drain, 0)
    ybase = slot * half

    def comb(t, carry):
        acc = _tok(x_buf, t, rt)[...]
        for k in range(TOP_K):
            a = k * tm + t
            acc = acc + gate_ref[0, 0, a] * _tok(y_buf, ybase + lpos_ref[0, 0, a], rt)[...]
        _tok(x_buf, t, rt)[...] = acc
        return carry

    lax.fori_loop(0, tm, comb, 0, unroll=8)
    out = _from_tiles(x_buf, 0, tm, rt)
    if final_norm:
        out = _rms(out, fw_ref[...])

    @pl.when(i < ntm)
    def _():
        om_ref[...] = out

    @pl.when(i == ntm)
    def _():
        ot_ref[...] = out


def _combine(xm, xt, lpos, gates, win, yb, fw, *, tm, final_norm):
    d = xm.shape[1]
    ntm = xm.shape[0] // tm
    nt = ntm + 1
    rt = d // LANES

    def smem(arr, imap):
        return pl.BlockSpec((1, 1, arr.shape[2]), imap, memory_space=pltpu.SMEM)

    return pl.pallas_call(
        functools.partial(_combine_body, ntm=ntm, final_norm=final_norm),
        grid=(nt,),
        in_specs=[
            smem(lpos, lambda i: (i, 0, 0)),
            smem(gates, lambda i: (i, 0, 0)),
            smem(win, lambda i: (i, 0, 0)),
            smem(win, lambda i: (jnp.minimum(i + 1, nt - 1), 0, 0)),
        ] + _x_specs(xm, tm) + [
            _full(fw.shape),
            pl.BlockSpec(memory_space=pl.ANY),
        ],
        out_specs=_x_specs(xm, tm),
        out_shape=[jax.ShapeDtypeStruct(xm.shape, F32), jax.ShapeDtypeStruct(xt.shape, F32)],
        scratch_shapes=[pltpu.VMEM((2 * _ybuf_tokens(tm) * rt, LANES), F32),
                        pltpu.VMEM((tm * rt, LANES), F32),
                        pltpu.SemaphoreType.DMA((2,))],
        compiler_params=_cparams(("arbitrary",)),
        name="moe_combine",
    )(lpos, gates, win, win, xm, xt, fw, yb)


def _moe(xm, xt, nw, w_group, b_group, w_expert, b_expert, wg, wu, wd, fw, *, layer, final_norm):
    tm, d = xt.shape
    nt = xm.shape[0] // tm + 1
    t = nt * tm
    pad = LANES - N_EXPERTS - N_EXPERT_GROUPS
    wr = jnp.pad(jnp.concatenate([w_expert, w_group], axis=1), ((0, 0), (0, pad)))
    wr_hi = wr.astype(BF16)
    wr = jnp.concatenate([wr_hi, (wr - wr_hi.astype(F32)).astype(BF16)], axis=1)
    br = jnp.pad(jnp.concatenate([b_expert, b_group]), (0, pad)).reshape(1, LANES)
    nw2 = nw.reshape(1, d)
    info, cnt = _route(xm, xt, nw2, wr, br, tm=tm)

    def per_tile(rows):
        return rows.reshape(TOP_K, nt, tm).transpose(1, 0, 2).reshape(nt, 1, TOP_K * tm)

    gates = per_tile(info[0:TOP_K])
    lpos = per_tile(info[4:4 + TOP_K].astype(I32))
    lpos_al = per_tile(info[6:6 + TOP_K].astype(I32))
    cnt = cnt[:, 0, :N_EXPERTS].astype(I32)
    total = jnp.sum(cnt, axis=0)
    padded = jnp.where(total > 0, (total + SEG_W + MOE_BLOCK - 2) // MOE_BLOCK * MOE_BLOCK, 0)
    pend = jnp.cumsum(padded)
    pstart = pend - padded
    gstart = pstart[None, :] + jnp.cumsum(cnt, axis=0) - cnt
    lstart = jnp.cumsum(cnt, axis=1) - cnt
    nwin_e = (cnt + SEG_W - 1) // SEG_W
    lstart_al = (jnp.cumsum(nwin_e, axis=1) - nwin_e) * SEG_W

    winc = jnp.cumsum(nwin_e, axis=1)
    nwin = winc[:, -1:]
    mw = _max_windows(tm)
    j = jnp.arange(mw, dtype=I32)[None, :]
    owner = (jnp.sum(winc[:, None, :] <= j[:, :, None], axis=-1)[:, :, None]
             == jnp.arange(N_EXPERTS, dtype=I32)[None, None, :])
    pick = lambda tab: jnp.sum(jnp.where(owner, tab[:, None, :], 0), axis=-1)
    w_off = (j - pick(winc - nwin_e)) * SEG_W
    live = j < nwin
    src_loc = jnp.where(live, pick(lstart) + w_off, 0)
    slot_g = jnp.where(live, pick(gstart) + w_off, 0)
    dst_loc = jnp.where(live, pick(lstart_al) + w_off, 0)
    nprev = jnp.concatenate([jnp.zeros((1, 1), I32), nwin[:-1]], axis=0)
    win_d = jnp.concatenate([nwin, nprev, src_loc, slot_g], axis=1).reshape(nt, 1, WIN_HDR + 2 * mw)
    win_c = jnp.concatenate([nwin, nprev, slot_g, dst_loc], axis=1).reshape(nt, 1, WIN_HDR + 2 * mw)
    n_blocks = -(-(t * TOP_K + N_EXPERTS * (MOE_BLOCK + SEG_W - 2)) // MOE_BLOCK)
    n_slots = n_blocks * MOE_BLOCK
    zfirst = jnp.concatenate([pstart + total // SEG_W * SEG_W, pend[-1:]])
    zend = jnp.concatenate([pend, jnp.full((1,), n_slots, I32)])
    nz_e = (zend - zfirst) // SEG_W
    zinc = jnp.cumsum(nz_e)
    mz = N_EXPERTS * ((MOE_BLOCK + 2 * SEG_W) // SEG_W) + n_slots // SEG_W - TOP_K * t // SEG_W
    jz = jnp.arange(mz, dtype=I32)
    zowner = (jnp.sum(zinc[None, :] <= jz[:, None], axis=-1)[:, None]
              == jnp.arange(N_EXPERTS + 1, dtype=I32)[None, :])
    zpick = lambda tab: jnp.sum(jnp.where(zowner, tab[None, :], 0), axis=-1)
    zslot = jnp.where(jz < zinc[-1], zpick(zfirst) + (jz - zpick(zinc - nz_e)) * SEG_W, 0)
    zwin = jnp.concatenate([zinc[-1:], zslot]).astype(I32)
    blk_exp = jnp.minimum(jnp.sum(pend[None, :] <= (jnp.arange(n_blocks, dtype=I32) * MOE_BLOCK)[:, None], axis=1),
                          N_EXPERTS - 1).astype(I32)
    n_used = (pend[-1] // MOE_BLOCK).astype(I32).reshape(1)

    xb = _dispatch(xm, xt, nw2, lpos, win_d, zwin, tm=tm, n_slots=n_slots)
    yb = _experts(blk_exp, n_used, xb, wg, wu, wd, layer=layer)
    return _combine(xm, xt, lpos_al, gates, win_c, yb, fw.reshape(1, d), tm=tm, final_norm=final_norm)


def _linear_body(x_ref, nw_ref, w_ref, b_ref, r_ref, out_ref, *, norm):
    x = x_ref[...]
    if norm:
        x = _rms(x, nw_ref[...])
    out_ref[...] = _bdot(x, w_ref[...]) + b_ref[...] + r_ref[...]


def _linear(x, w, *, nw=None, bias=None, res=None, tn):
    m, kd = x.shape
    n = w.shape[1]
    norm = nw is not None
    nw = jnp.ones((1, kd), F32) if nw is None else nw
    bias = jnp.zeros((1, n), F32) if bias is None else bias
    res = jnp.zeros((m, n), F32) if res is None else res
    tn = min(tn, n)
    return pl.pallas_call(
        functools.partial(_linear_body, norm=norm),
        grid=(n // tn,),
        in_specs=[
            _full(x.shape), _full(nw.shape),
            pl.BlockSpec((kd, tn), lambda j: (0, j)),
            pl.BlockSpec((1, tn), lambda j: (0, j)),
            pl.BlockSpec((m, tn), lambda j: (0, j)),
        ],
        out_specs=pl.BlockSpec((m, tn), lambda j: (0, j)),
        out_shape=jax.ShapeDtypeStruct((m, n), F32),
        compiler_params=_cparams(("arbitrary",)),
        name="sample_linear",
    )(x, nw, w, bias, res)


def _sample_conv_body(xbc_ref, st_ref, cw_ref, cb_ref, dtr_ref, dtb_ref, xc_ref, stn_ref, dt_ref, cbg_ref,
                      *, d_inner):
    nst = D_STATE
    xbc = xbc_ref[...]
    acc = cb_ref[...] + cw_ref[3:4, :] * xbc
    for k in range(CONV_W - 1):
        acc = acc + cw_ref[k:k + 1, :] * st_ref[k]
    xc = _silu(acc)
    xc_ref[...] = xc
    stn_ref[0] = st_ref[1]
    stn_ref[1] = st_ref[2]
    stn_ref[2] = xbc
    dt_ref[...] = _softplus(dtr_ref[...] + dtb_ref[...])
    lane = lax.broadcasted_iota(I32, (xbc.shape[0], LANES), 1)
    cbg = jnp.zeros((xbc.shape[0], LANES), F32)
    for g in range(N_BC_GROUPS):
        b_g = xc[:, d_inner + g * nst:d_inner + (g + 1) * nst]
        c_g = xc[:, d_inner + (N_BC_GROUPS + g) * nst:d_inner + (N_BC_GROUPS + g + 1) * nst]
        cbg = jnp.where(lane == g, jnp.sum(b_g * c_g, axis=-1, keepdims=True), cbg)
    cbg_ref[...] = cbg


def _sample_conv(xbc, st_t, cw, cb, dtr, dtb, *, d_inner):
    m, cd = xbc.shape
    return pl.pallas_call(
        functools.partial(_sample_conv_body, d_inner=d_inner),
        grid=(1,),
        in_specs=[_full(xbc.shape), _full(st_t.shape), _full(cw.shape), _full(cb.shape), _full(dtr.shape),
                  _full(dtb.shape)],
        out_specs=[_full((m, cd)), _full(st_t.shape), _full((m, LANES)), _full((m, LANES))],
        out_shape=[jax.ShapeDtypeStruct((m, cd), F32), jax.ShapeDtypeStruct(st_t.shape, F32),
                   jax.ShapeDtypeStruct((m, LANES), F32), jax.ShapeDtypeStruct((m, LANES), F32)],
        compiler_params=_cparams(("arbitrary",)),
        name="sample_conv",
    )(xbc, st_t, cw, cb, dtr, dtb)


def _sample_ssd_body(s0_ref, xt_ref, bc_ref, hs_ref, par_ref, sn_ref, yt_ref, *, n_heads):
    hp = SSM_HEAD_DIM
    hpg = n_heads // N_BC_GROUPS
    xt = xt_ref[0]
    dt = hs_ref[0, 0:1, :]
    cbh = hs_ref[0, 1:2, :]
    a = -jnp.exp(par_ref[0:1, :])
    dsk = par_ref[1:2, :]
    dec = jnp.exp(dt * a)
    xdt = xt * dt
    lane = lax.broadcasted_iota(I32, (hp, n_heads), 1)
    yoff = jnp.zeros((hp, n_heads), F32)
    for hh in range(n_heads):
        g = hh // hpg
        b_row = bc_ref[0, g:g + 1, :]
        c_row = bc_ref[0, N_BC_GROUPS + g:N_BC_GROUPS + g + 1, :]
        s0 = s0_ref[0, hh]
        yo = jnp.sum(s0 * c_row, axis=-1, keepdims=True)
        yoff = jnp.where(lane == hh, yo, yoff)
        sn_ref[0, hh] = s0 * dec[:, hh:hh + 1] + xdt[:, hh:hh + 1] * b_row
    yt_ref[0] = cbh * xdt + yoff * dec + xt * dsk


def _sample_ssd(s0, xt, bc, hs, par):
    bsz, n_heads, hp, nst = s0.shape
    return pl.pallas_call(
        functools.partial(_sample_ssd_body, n_heads=n_heads),
        grid=(bsz,),
        in_specs=[
            pl.BlockSpec((1, n_heads, hp, nst), lambda b: (b, 0, 0, 0)),
            pl.BlockSpec((1, hp, n_heads), lambda b: (b, 0, 0)),
            pl.BlockSpec((1,) + bc.shape[1:], lambda b: (b, 0, 0)),
            pl.BlockSpec((1,) + hs.shape[1:], lambda b: (b, 0, 0)),
            _full(par.shape),
        ],
        out_specs=[
            pl.BlockSpec((1, n_heads, hp, nst), lambda b: (b, 0, 0, 0)),
            pl.BlockSpec((1, hp, n_heads), lambda b: (b, 0, 0)),
        ],
        out_shape=[jax.ShapeDtypeStruct(s0.shape, F32), jax.ShapeDtypeStruct((bsz, hp, n_heads), F32)],
        compiler_params=_cparams(("arbitrary",)),
        name="sample_ssd",
    )(s0, xt, bc, hs, par)


def _sample_gnorm_out_body(y_ref, z_ref, gnw_ref, wout_ref, x_ref, out_ref, *, d_inner):
    gw = d_inner // N_BC_GROUPS
    acc = x_ref[...]
    for g in range(N_BC_GROUPS):
        glanes = slice(g * gw, (g + 1) * gw)
        gg = y_ref[:, glanes] * _silu(z_ref[:, glanes])
        gg = gg * lax.rsqrt(jnp.mean(gg * gg, axis=-1, keepdims=True) + EPS) * gnw_ref[:, glanes]
        acc = acc + jnp.dot(gg.astype(BF16), wout_ref[glanes, :], preferred_element_type=F32)
    out_ref[...] = acc


def _sample_gnorm_out(y, z, gnw, wout, x):
    d_inner = y.shape[1]
    return pl.pallas_call(
        functools.partial(_sample_gnorm_out_body, d_inner=d_inner),
        grid=(1,),
        in_specs=[_full(y.shape), _full(z.shape), _full(gnw.shape), _full(wout.shape), _full(x.shape)],
        out_specs=_full(x.shape),
        out_shape=jax.ShapeDtypeStruct(x.shape, F32),
        compiler_params=_cparams(("arbitrary",)),
        name="sample_gnorm_out",
    )(y, z, gnw, wout, x)


def _sample_attn_body(q_ref, kn_ref, vn_ref, kc_ref, vc_ref, sink_ref, o_ref, ko_ref, vo_ref, s_buf, sn_buf):
    bt = q_ref.shape[0]
    wb = kc_ref.shape[1]
    hd = HEAD_DIM
    nh = N_Q_HEADS
    qpk = N_Q_HEADS // N_KV_HEADS
    scale = hd ** -0.5
    for b in range(bt):
        kn = kn_ref[b]
        kn_h = jnp.concatenate([jnp.broadcast_to(kn[:, g * hd:(g + 1) * hd], (qpk, hd))
                                for g in range(N_KV_HEADS)], axis=0)
        sn_buf[b * nh:(b + 1) * nh, :] = jnp.sum(q_ref[b] * kn_h, axis=-1, keepdims=True) * scale
        for g in range(N_KV_HEADS):
            rows = slice(b * nh + g * qpk, b * nh + (g + 1) * qpk)
            s_buf[rows, :] = _bdot_nt(q_ref[b, g * qpk:(g + 1) * qpk, :], kc_ref[b, :, g * hd:(g + 1) * hd]) * scale
    s = s_buf[...]
    s_new = sn_buf[...]
    sink = sink_ref[...]
    m = jnp.maximum(jnp.maximum(jnp.max(s, axis=-1, keepdims=True), s_new), sink)
    p = jnp.exp(s - m)
    p_new = jnp.exp(s_new - m)
    inv = 1.0 / (jnp.sum(p, axis=-1, keepdims=True) + p_new + jnp.exp(sink - m))
    s_buf[...] = p * inv
    sn_buf[...] = p_new * inv
    for b in range(bt):
        vn = vn_ref[b]
        for g in range(N_KV_HEADS):
            rows = slice(b * nh + g * qpk, b * nh + (g + 1) * qpk)
            cols = slice(g * hd, (g + 1) * hd)
            o_ref[b, g * qpk:(g + 1) * qpk, :] = (_bdot(s_buf[rows, :], vc_ref[b, :, cols])
                                                   + sn_buf[rows, :] * vn[:, cols])
        ko_ref[b, 0:wb - 1, :] = kc_ref[b, 1:wb, :]
        ko_ref[b, wb - 1:wb, :] = kn_ref[b]
        vo_ref[b, 0:wb - 1, :] = vc_ref[b, 1:wb, :]
        vo_ref[b, wb - 1:wb, :] = vn


def _sample_attn(q3, kn, vn, kc, vc, sinks, *, bt):
    bsz, nqh, hd = q3.shape
    wb, nk = kc.shape[1], kc.shape[2]
    return pl.pallas_call(
        _sample_attn_body,
        grid=(bsz // bt,),
        in_specs=[
            pl.BlockSpec((bt, nqh, hd), lambda i: (i, 0, 0)),
            pl.BlockSpec((bt, 1, nk), lambda i: (i, 0, 0)),
            pl.BlockSpec((bt, 1, nk), lambda i: (i, 0, 0)),
            pl.BlockSpec((bt, wb, nk), lambda i: (i, 0, 0)),
            pl.BlockSpec((bt, wb, nk), lambda i: (i, 0, 0)),
            _full(sinks.shape),
        ],
        out_specs=[
            pl.BlockSpec((bt, nqh, hd), lambda i: (i, 0, 0)),
            pl.BlockSpec((bt, wb, nk), lambda i: (i, 0, 0)),
            pl.BlockSpec((bt, wb, nk), lambda i: (i, 0, 0)),
        ],
        out_shape=[jax.ShapeDtypeStruct(q3.shape, F32), jax.ShapeDtypeStruct(kc.shape, F32),
                   jax.ShapeDtypeStruct(vc.shape, F32)],
        scratch_shapes=[pltpu.VMEM((bt * nqh, wb), F32), pltpu.VMEM((bt * nqh, 1), F32)],
        compiler_params=_cparams(("arbitrary",)),
        name="sample_attn",
    )(q3, kn, vn, kc, vc, sinks)


def _mamba_sample(x, nw, mw, state_conv, state_ssm):
    win, cw, cb, dtb, alog, dsk, gnw, wout = mw
    bsz, d = x.shape
    d_inner = wout.shape[0]
    conv_dim = cw.shape[1]
    n_heads = d_inner // SSM_HEAD_DIM
    hp = SSM_HEAD_DIM
    proj = _linear(x, win, nw=nw, tn=896)
    z = proj[:, :d_inner]
    xbc = proj[:, d_inner:d_inner + conv_dim]
    dtr = proj[:, d_inner + conv_dim:]
    st_t = jnp.transpose(state_conv, (1, 0, 2))
    xc, stn_t, dt, cbg = _sample_conv(xbc, st_t, cw, cb, dtr, dtb, d_inner=d_inner)
    conv_new = jnp.transpose(stn_t, (1, 0, 2))
    xt = jnp.transpose(xc[:, :d_inner].reshape(bsz, n_heads, hp), (0, 2, 1))
    bc = xc[:, d_inner:].reshape(bsz, 2 * N_BC_GROUPS, D_STATE)
    cbh = jnp.repeat(cbg[:, :N_BC_GROUPS], n_heads // N_BC_GROUPS, axis=1)
    hs = jnp.stack([dt[:, :n_heads], cbh], axis=1)
    par = jnp.stack([alog[0, :n_heads], dsk.reshape(n_heads, hp)[:, 0]], axis=0)
    ssm_new, yt = _sample_ssd(state_ssm, xt, bc, hs, par)
    y = jnp.transpose(yt, (0, 2, 1)).reshape(bsz, d_inner)
    out = _sample_gnorm_out(y, z, gnw, wout, x)
    return out, conv_new, ssm_new


def _attn_sample(x, nw, wqkv, bqkv, sinks, wo, bo, cache_k, cache_v):
    bsz, d = x.shape
    wb = cache_k.shape[1]
    nq = N_Q_HEADS * HEAD_DIM
    nk = N_KV_HEADS * HEAD_DIM
    qkv = _linear(x, wqkv, nw=nw, bias=bqkv, tn=512)
    q3 = qkv[:, :nq].reshape(bsz, N_Q_HEADS, HEAD_DIM)
    kn = qkv[:, nq:nq + nk].reshape(bsz, 1, nk)
    vn = qkv[:, nq + nk:].reshape(bsz, 1, nk)
    o3, ko, vo = _sample_attn(q3, kn, vn, cache_k.reshape(bsz, wb, nk), cache_v.reshape(bsz, wb, nk),
                              jnp.tile(sinks.reshape(N_Q_HEADS, 1), (8, 1)), bt=8)
    out = _linear(o3.reshape(bsz, nq), wo, bias=bo, res=x, tn=512)
    return out, ko.reshape(cache_k.shape), vo.reshape(cache_v.shape)


def kernel(x_prompt, x_sample, state_ssm, state_conv, cache_k_win, cache_v_win,
           mamba_w_in, mamba_conv_w, mamba_conv_b, mamba_dt_bias, mamba_a_log, mamba_d,
           mamba_norm_w, mamba_w_out, attn_w_qkv, attn_b_qkv, attn_sinks, attn_w_o, attn_b_o,
           norm_mix, norm_ffn, router_w_group, router_b_group, router_w_expert, router_b_expert,
           expert_w_gate, expert_w_up, expert_w_down, norm_final):
    bsz, seq, d = x_prompt.shape
    dbsz, dseq, _ = x_sample.shape
    assert dseq == 1 and cache_k_win.shape[2] <= WINDOW and seq % WINDOW == 0
    assert dbsz <= MOE_TILE and (bsz * seq) % MOE_TILE == 0
    depth = norm_mix.shape[0]
    xp = x_prompt.reshape(bsz * seq, d)
    xs = x_sample.reshape(dbsz, d)
    ssm_p, conv_p, kp_l, vp_l = [], [], [], []
    ssm_s, conv_s, ks_l, vs_l = [], [], [], []
    for i in range(depth):
        j = i // 2
        nw = norm_mix[i].reshape(1, d)
        if i % 2 == 0:
            mw = _mamba_weights(mamba_w_in[j], mamba_conv_w[j], mamba_conv_b[j], mamba_dt_bias[j],
                                mamba_a_log[j], mamba_d[j], mamba_norm_w[j], mamba_w_out[j])
            xp, cp, sp = _mamba_prompt(xp, nw, *mw, bsz=bsz, seq=seq, ts=2 * SSD_CHUNK)
            xs, cs_, ss_ = _mamba_sample(xs, nw, mw, state_conv[j], state_ssm[j])
            ssm_p.append(sp)
            conv_p.append(cp)
            ssm_s.append(ss_)
            conv_s.append(cs_)
        else:
            wqkv = attn_w_qkv[j].astype(BF16)
            bqkv = attn_b_qkv[j].reshape(1, -1)
            wo = attn_w_o[j].astype(BF16)
            bo = attn_b_o[j].reshape(1, d)
            xp, kp, vp = _attn_prompt(xp, attn_sinks[j], nw, wqkv, bqkv, wo, bo, bsz=bsz, seq=seq,
                                      tq=WINDOW)
            xs, ks_, vs_ = _attn_sample(xs, nw, wqkv, bqkv, attn_sinks[j], wo, bo, cache_k_win[j], cache_v_win[j])
            kp_l.append(kp.reshape(bsz, WINDOW, N_KV_HEADS, HEAD_DIM))
            vp_l.append(vp.reshape(bsz, WINDOW, N_KV_HEADS, HEAD_DIM))
            ks_l.append(ks_)
            vs_l.append(vs_)
        last = i == depth - 1
        moe_w = (norm_ffn[i], router_w_group[i], router_b_group[i], router_w_expert[i], router_b_expert[i],
                 expert_w_gate, expert_w_up, expert_w_down, norm_final)
        xs_tile = jnp.pad(xs, ((0, MOE_TILE - dbsz), (0, 0)))
        xp, xs_tile = _moe(xp, xs_tile, *moe_w, layer=i, final_norm=last)
        xs = xs_tile[:dbsz]
    return (xp.reshape(bsz, seq, d), xs.reshape(dbsz, dseq, d),
            jnp.stack(ssm_p), jnp.stack(conv_p), jnp.stack(kp_l), jnp.stack(vp_l),
            jnp.stack(ssm_s), jnp.stack(conv_s), jnp.stack(ks_l), jnp.stack(vs_l))
```

```python
import functools
import math

import jax
import jax.numpy as jnp
from jax import lax
from jax.experimental import pallas as pl
from jax.experimental.pallas import tpu as pltpu

F32 = jnp.float32
BF16 = jnp.bfloat16
I32 = jnp.int32

EPS = 1e-5
LANES = 128
VMEM_LIMIT = 56 * 1024 * 1024

SSM_HEAD_DIM = 64
D_STATE = 128
N_BC_GROUPS = 8
CONV_W = 4
SSD_CHUNK = 128
N_Q_HEADS = 16
N_KV_HEADS = 4
HEAD_DIM = 64
WINDOW = 128
N_EXPERT_GROUPS = 4
EXPERTS_PER_GROUP = 8
N_EXPERTS = N_EXPERT_GROUPS * EXPERTS_PER_GROUP
TOP_K = 2
MOE_BLOCK = 256
MOE_TILE = 512


def _cparams(sem):
    return pltpu.CompilerParams(dimension_semantics=sem, vmem_limit_bytes=VMEM_LIMIT)


def _full(shape):
    n = len(shape)
    return pl.BlockSpec(shape, lambda *_: (0,) * n)


def _resident(shape):
    n = len(shape)
    return pl.BlockSpec(shape, lambda *_: (0,) * n, pipeline_mode=pl.Buffered(1))


def _rms(x, w):
    return x * lax.rsqrt(jnp.mean(x * x, axis=-1, keepdims=True) + EPS) * w


def _silu(x):
    return x / (1.0 + jnp.exp(-x))


def _softplus(x):
    return jnp.maximum(x, 0.0) + jnp.log(1.0 + jnp.exp(-jnp.abs(x)))


def _bdot(a, b):
    return jnp.dot(a.astype(BF16), b.astype(BF16), preferred_element_type=F32)


def _bdot_nt(a, b):
    return lax.dot_general(a.astype(BF16), b.astype(BF16), (((1,), (1,)), ((), ())),
                           preferred_element_type=F32)


def _bdot_tn(a, b):
    return lax.dot_general(a.astype(BF16), b.astype(BF16), (((0,), (0,)), ((), ())),
                           preferred_element_type=F32)


def _fdot(a, b):
    return jnp.dot(a, b, preferred_element_type=F32, precision=lax.Precision.HIGHEST)


def _mamba_prompt_body(x_ref, nw_ref, win_ref, cw_ref, cb_ref, dtb_ref, alog_ref, dsk_ref, gnw_ref,
                       wout_ref, out_ref, conv_ref, ssm_ref, h_buf, xbc_buf, xc_buf, st_buf, y_buf,
                       *, d_inner, n_heads):
    ts = x_ref.shape[0]
    cs = SSD_CHUNK
    hp = SSM_HEAD_DIM
    nst = D_STATE
    gw = d_inner // N_BC_GROUPS
    hpg = n_heads // N_BC_GROUPS
    conv_dim = d_inner + 2 * N_BC_GROUPS * nst
    s = pl.program_id(1)

    @pl.when(s == 0)
    def _():
        xbc_buf[:, 0:8, :] = jnp.zeros((conv_dim // LANES, 8, LANES), F32)
        st_buf[...] = jnp.zeros_like(st_buf)

    h_buf[...] = _rms(x_ref[...], nw_ref[...]).astype(BF16)
    dtr = jnp.dot(h_buf[...], win_ref[:, d_inner + conv_dim:], preferred_element_type=F32)
    ct = 512
    spp = ct // LANES
    for j in range(conv_dim // ct):
        cols = slice(j * ct, (j + 1) * ct)
        piece = jnp.dot(h_buf[...], win_ref[:, d_inner + j * ct:d_inner + (j + 1) * ct],
                        preferred_element_type=F32)
        for q in range(spp):
            xbc_buf[j * spp + q, 8:8 + ts, :] = piece[:, q * LANES:(q + 1) * LANES]

        def back(k):
            return jnp.concatenate([xbc_buf[j * spp + q, pl.ds(8 - k, ts), :] for q in range(spp)], axis=1)

        acc = cb_ref[:, cols] + cw_ref[3:4, cols] * piece
        acc = acc + cw_ref[2:3, cols] * back(1)
        acc = acc + cw_ref[1:2, cols] * back(2)
        acc = acc + cw_ref[0:1, cols] * back(3)
        xc_buf[:, cols] = _silu(acc)
    for c in range(conv_dim // LANES):
        last3 = xbc_buf[c, 5 + ts:8 + ts, :]
        xbc_buf[c, 5:8, :] = last3
        conv_ref[0, :, c * LANES:(c + 1) * LANES] = last3

    dt = _softplus(dtr + dtb_ref[...])
    da = dt * (-jnp.exp(alog_ref[...]))
    row = lax.broadcasted_iota(I32, (cs, cs), 0)
    col = lax.broadcasted_iota(I32, (cs, cs), 1)
    causal = row >= col
    tril = causal.astype(F32)
    lane = lax.broadcasted_iota(I32, (cs, LANES), 1)
    lo_half = lane < hp

    for c in range(ts // cs):
        rows = slice(c * cs, (c + 1) * cs)
        da_c = da[rows]
        dt_c = dt[rows]
        acum = _fdot(tril, da_c)
        acum_t = acum.T
        dt_t = dt_c.T
        a_last = acum[cs - 1:cs, :]
        to_end = jnp.exp(a_last - acum)
        w_all = dt_c * to_end
        ea = jnp.exp(acum)
        cd = jnp.exp(a_last)
        for g in range(N_BC_GROUPS):
            b_g = xc_buf[rows, d_inner + g * nst:d_inner + (g + 1) * nst]
            c_g = xc_buf[rows, d_inner + (N_BC_GROUPS + g) * nst:d_inner + (N_BC_GROUPS + g + 1) * nst]
            cb = _bdot_nt(c_g, b_g)
            xw_parts = []
            for pr in range(hpg // 2):
                h0 = g * hpg + 2 * pr
                lanes0 = slice(h0 * hp, h0 * hp + 2 * hp)
                x_pair = xc_buf[rows, lanes0]
                st_pair = st_buf[:, lanes0]
                y_pair = jnp.zeros((cs, 2 * hp), F32)
                w_pair = jnp.zeros((cs, 2 * hp), F32)
                for k in range(2):
                    hh = h0 + k
                    seg = acum[:, hh:hh + 1] - acum_t[hh:hh + 1, :]
                    dec = jnp.exp(jnp.where(causal, seg, -jnp.inf))
                    m = cb * dec * dt_t[hh:hh + 1, :]
                    sel = lo_half if k == 0 else jnp.logical_not(lo_half)
                    x_k = jnp.where(sel, x_pair, 0.0)
                    st_k = jnp.where(sel, st_pair, 0.0)
                    lhs = jnp.concatenate([m, c_g * ea[:, hh:hh + 1]], axis=1)
                    rhs = jnp.concatenate([x_k, st_k], axis=0)
                    y_pair = y_pair + _bdot(lhs, rhs)
                    w_pair = jnp.where(sel, w_all[:, hh:hh + 1], w_pair)
                y_buf[rows, lanes0] = y_pair + x_pair * dsk_ref[:, lanes0]
                xw_parts.append(x_pair * w_pair)
            xw = jnp.concatenate(xw_parts, axis=1)
            glanes = slice(g * gw, (g + 1) * gw)
            cd_parts = [jnp.broadcast_to(cd[:, g * hpg + k:g * hpg + k + 1], (1, hp)) for k in range(hpg)]
            cd_g = jnp.concatenate(cd_parts, axis=1)
            st_buf[:, glanes] = st_buf[:, glanes] * cd_g + _bdot_tn(b_g, xw)

    @pl.when(s == pl.num_programs(1) - 1)
    def _():
        for pr in range(n_heads // 2):
            t = st_buf[:, 2 * pr * hp:2 * (pr + 1) * hp].T
            ssm_ref[0, 2 * pr] = t[0:hp]
            ssm_ref[0, 2 * pr + 1] = t[hp:2 * hp]

    acc = x_ref[...]
    for g in range(N_BC_GROUPS):
        glanes = slice(g * gw, (g + 1) * gw)
        z = jnp.dot(h_buf[...], win_ref[:, glanes], preferred_element_type=F32)
        gg = y_buf[:, glanes] * _silu(z)
        gg = gg * lax.rsqrt(jnp.mean(gg * gg, axis=-1, keepdims=True) + EPS) * gnw_ref[:, glanes]
        acc = acc + jnp.dot(gg.astype(BF16), wout_ref[glanes, :], preferred_element_type=F32)
    out_ref[...] = acc


def _mamba_prompt(x, nw, win, cw, cb, dtb, alog, dsk, gnw, wout, *, bsz, seq, ts):
    d = x.shape[1]
    d_inner = wout.shape[0]
    n_heads = d_inner // SSM_HEAD_DIM
    conv_dim = cw.shape[1]
    ns = seq // ts
    body = functools.partial(_mamba_prompt_body, d_inner=d_inner, n_heads=n_heads)
    return pl.pallas_call(
        body,
        grid=(bsz, ns),
        in_specs=[
            pl.BlockSpec((ts, d), lambda b, s: (b * ns + s, 0)),
            _full(nw.shape), _resident(win.shape), _full(cw.shape), _full(cb.shape), _full(dtb.shape),
            _full(alog.shape), _full(dsk.shape), _full(gnw.shape), _resident(wout.shape),
        ],
        out_specs=[
            pl.BlockSpec((ts, d), lambda b, s: (b * ns + s, 0)),
            pl.BlockSpec((1, CONV_W - 1, conv_dim), lambda b, s: (b, 0, 0)),
            pl.BlockSpec((1, n_heads, SSM_HEAD_DIM, D_STATE), lambda b, s: (b, 0, 0, 0)),
        ],
        out_shape=[
            jax.ShapeDtypeStruct((bsz * seq, d), F32),
            jax.ShapeDtypeStruct((bsz, CONV_W - 1, conv_dim), F32),
            jax.ShapeDtypeStruct((bsz, n_heads, SSM_HEAD_DIM, D_STATE), F32),
        ],
        scratch_shapes=[
            pltpu.VMEM((ts, d), BF16),
            pltpu.VMEM((conv_dim // LANES, 8 + ts, LANES), F32),
            pltpu.VMEM((ts, conv_dim), F32),
            pltpu.VMEM((D_STATE, d_inner), F32),
            pltpu.VMEM((ts, d_inner), F32),
        ],
        compiler_params=_cparams(("arbitrary", "arbitrary")),
        name="mamba_prompt",
    )(x, nw, win, cw, cb, dtb, alog, dsk, gnw, wout)


def _mamba_weights(w_in, conv_w, conv_b, dt_bias, a_log, d_skip, norm_w, w_out):
    d_inner = w_out.shape[0]
    n_heads = dt_bias.shape[0]
    pad = LANES - n_heads
    win = jnp.pad(w_in, ((0, 0), (0, pad))).astype(BF16)
    dtb = jnp.pad(dt_bias, (0, pad)).reshape(1, LANES)
    alog = jnp.pad(a_log, (0, pad)).reshape(1, LANES)
    dsk = jnp.repeat(d_skip, SSM_HEAD_DIM).reshape(1, d_inner)
    return (win, conv_w, conv_b.reshape(1, -1), dtb, alog, dsk, norm_w.reshape(1, d_inner),
            w_out.astype(BF16))


def _sink_softmax_pv(s, sink, v):
    m = jnp.maximum(jnp.max(s, axis=-1, keepdims=True), sink)
    p = jnp.exp(s - m)
    denom = jnp.sum(p, axis=-1, keepdims=True) + jnp.exp(sink - m)
    return _bdot(p, v) / denom


def _attn_prompt_body(sink_ref, x_ref, nw_ref, wqkv_ref, bqkv_ref, wo_ref, bo_ref,
                      out_ref, kwin_ref, vwin_ref, kv_buf, q_buf, o_buf):
    blk = WINDOW
    hd = HEAD_DIM
    nq = N_Q_HEADS * hd
    nk = N_KV_HEADS * hd
    qpk = N_Q_HEADS // N_KV_HEADS
    tq = x_ref.shape[0]
    s_id = pl.program_id(1)

    @pl.when(s_id == 0)
    def _():
        kv_buf[0:blk, :] = jnp.zeros((blk, 2 * nk), F32)

    h = _rms(x_ref[...], nw_ref[...]).astype(BF16)
    q_buf[...] = jnp.dot(h, wqkv_ref[:, 0:nq], preferred_element_type=F32) + bqkv_ref[:, 0:nq]
    kv_buf[blk:blk + tq, :] = jnp.dot(h, wqkv_ref[:, nq:], preferred_element_type=F32) + bqkv_ref[:, nq:]
    kwin_ref[0] = kv_buf[tq:tq + blk, 0:nk]
    vwin_ref[0] = kv_buf[tq:tq + blk, nk:]

    row = lax.broadcasted_iota(I32, (blk, 2 * blk), 0)
    col = lax.broadcasted_iota(I32, (blk, 2 * blk), 1)
    diff = row + blk - col
    band = (diff >= 0) & (diff <= WINDOW)
    scale = hd ** -0.5
    for qb in range(tq // blk):
        qrows = slice(qb * blk, (qb + 1) * blk)
        krows = slice(qb * blk, (qb + 2) * blk)
        ok = band & ((col >= blk) | (s_id > 0)) if qb == 0 else band
        for g in range(N_KV_HEADS):
            k_g = kv_buf[krows, g * hd:(g + 1) * hd]
            v_g = kv_buf[krows, nk + g * hd:nk + (g + 1) * hd]
            for j in range(qpk):
                hh = g * qpk + j
                s = _bdot_nt(q_buf[qrows, hh * hd:(hh + 1) * hd], k_g) * scale
                s = jnp.where(ok, s, -jnp.inf)
                o_buf[qrows, hh * hd:(hh + 1) * hd] = _sink_softmax_pv(s, sink_ref[hh], v_g)
    kv_buf[0:blk, :] = kv_buf[tq:tq + blk, :]
    out_ref[...] = (x_ref[...] + jnp.dot(o_buf[...].astype(BF16), wo_ref[...], preferred_element_type=F32)
                    + bo_ref[...])


def _attn_prompt(x, sinks, nw, wqkv, bqkv, wo, bo, *, bsz, seq, tq):
    d = x.shape[1]
    blk = WINDOW
    nb = seq // tq
    nk = N_KV_HEADS * HEAD_DIM
    nq = N_Q_HEADS * HEAD_DIM
    return pl.pallas_call(
        _attn_prompt_body,
        grid=(bsz, nb),
        in_specs=[
            pl.BlockSpec(memory_space=pltpu.SMEM),
            pl.BlockSpec((tq, d), lambda b, s: (b * nb + s, 0)),
            _full(nw.shape), _full(wqkv.shape), _full(bqkv.shape), _full(wo.shape), _full(bo.shape),
        ],
        out_specs=[
            pl.BlockSpec((tq, d), lambda b, s: (b * nb + s, 0)),
            pl.BlockSpec((1, blk, nk), lambda b, s: (b, 0, 0)),
            pl.BlockSpec((1, blk, nk), lambda b, s: (b, 0, 0)),
        ],
        out_shape=[
            jax.ShapeDtypeStruct((bsz * seq, d), F32),
            jax.ShapeDtypeStruct((bsz, blk, nk), F32),
            jax.ShapeDtypeStruct((bsz, blk, nk), F32),
        ],
        scratch_shapes=[
            pltpu.VMEM((blk + tq, 2 * nk), F32),
            pltpu.VMEM((tq, nq), F32),
            pltpu.VMEM((tq, nq), F32),
        ],
        compiler_params=_cparams(("arbitrary", "arbitrary")),
        name="attn_prompt",
    )(sinks, x, nw, wqkv, bqkv, wo, bo)


def _x_specs(xm, tm):
    ntm, d = xm.shape[0] // tm, xm.shape[1]
    return [pl.BlockSpec((tm, d), lambda i: (jnp.minimum(i, ntm - 1), 0)), pl.BlockSpec((tm, d), lambda i: (0, 0))]


def _x_tile(xm_ref, xt_ref, ntm):
    return jnp.where(pl.program_id(0) < ntm, xm_ref[...], xt_ref[...])


def _route_body(xm_ref, xt_ref, nw_ref, wr_ref, br_ref, info_ref, cnt_ref, *, ntm):
    tm = xm_ref.shape[0]
    h = _rms(_x_tile(xm_ref, xt_ref, ntm), nw_ref[...])
    h_hi = h.astype(BF16)
    h_lo = (h - h_hi.astype(F32)).astype(BF16)
    part = jnp.dot(h_hi, wr_ref[...], preferred_element_type=F32)
    logits = (part[:, 0:LANES] + part[:, LANES:] + jnp.dot(h_lo, wr_ref[:, 0:LANES], preferred_element_type=F32)
              + br_ref[...])
    lane_i = lax.broadcasted_iota(I32, (tm, LANES), 1)
    lane = lane_i.astype(F32)
    lane_grp = (lane_i // EXPERTS_PER_GROUP).astype(F32)
    big = float(LANES)
    ninf = -jnp.inf

    def first_argmax(v):
        m = jnp.max(v, axis=-1, keepdims=True)
        return m, jnp.min(jnp.where(v == m, lane, big), axis=-1, keepdims=True)

    gmask = (lane_i >= N_EXPERTS) & (lane_i < N_EXPERTS + N_EXPERT_GROUPS)
    gl = jnp.where(gmask, logits, ninf)
    gmax, gi = first_argmax(gl)
    gi = gi - float(N_EXPERTS)
    pg = 1.0 / jnp.sum(jnp.exp(gl - gmax), axis=-1, keepdims=True)
    emask = (lane_i < N_EXPERTS) & (lane_grp == gi)
    el = jnp.where(emask, logits, ninf)
    m1, i1 = first_argmax(el)
    el2 = jnp.where(lane == i1, ninf, el)
    m2, i2 = first_argmax(el2)
    den = jnp.sum(jnp.exp(el - m1), axis=-1, keepdims=True)
    tp1 = 1.0 / den
    tp2 = jnp.exp(m2 - m1) / den
    g1 = pg * tp1 / (tp1 + tp2)
    g2 = pg * tp2 / (tp1 + tp2)
    hot1 = lane == i1
    hot2 = lane == i2
    onehot = jnp.where(hot1 | hot2, 1.0, 0.0)
    rr = lax.broadcasted_iota(I32, (tm, tm), 0)
    cc = lax.broadcasted_iota(I32, (tm, tm), 1)
    before = jnp.where(rr > cc, 1.0, 0.0)
    cum = _bdot(before, onehot)
    r1 = jnp.sum(jnp.where(hot1, cum, 0.0), axis=-1, keepdims=True)
    r2 = jnp.sum(jnp.where(hot2, cum, 0.0), axis=-1, keepdims=True)
    cnt_row = jnp.sum(onehot, axis=0, keepdims=True)
    nwin_row = jnp.floor((cnt_row + (SEG_W - 1.0)) * (1.0 / SEG_W))
    er = lax.broadcasted_iota(I32, (LANES, LANES), 0)
    ec = lax.broadcasted_iota(I32, (LANES, LANES), 1)
    earlier = jnp.where(er < ec, 1.0, 0.0)
    both = jnp.concatenate([jnp.broadcast_to(cnt_row, (8, LANES)), jnp.broadcast_to(nwin_row, (8, LANES))], axis=0)
    pre = _fdot(both, earlier)
    start = pre[0:1, :]
    start_al = pre[8:9, :] * float(SEG_W)

    def at(hot, row):
        return jnp.sum(jnp.where(hot, row, 0.0), axis=-1, keepdims=True)

    info = jnp.zeros((tm, LANES), F32)
    for k, v in enumerate((g1, g2, i1, i2, at(hot1, start) + r1, at(hot2, start) + r2,
                           at(hot1, start_al) + r1, at(hot2, start_al) + r2)):
        info = jnp.where(lane_i == k, v, info)
    info_ref[...] = info.T[0:8, :]
    cnt_ref[0] = jnp.broadcast_to(cnt_row, (8, LANES))


ROUTE_ROWS = 8


def _route(xm, xt, nw, wr, br, *, tm):
    ntm = xm.shape[0] // tm
    nt = ntm + 1
    return pl.pallas_call(
        functools.partial(_route_body, ntm=ntm),
        grid=(nt,),
        in_specs=_x_specs(xm, tm) + [_full(nw.shape), _full(wr.shape), _full(br.shape)],
        out_specs=[pl.BlockSpec((ROUTE_ROWS, tm), lambda i: (0, i)), pl.BlockSpec((1, 8, LANES), lambda i: (i, 0, 0))],
        out_shape=[jax.ShapeDtypeStruct((ROUTE_ROWS, nt * tm), F32), jax.ShapeDtypeStruct((nt, 8, LANES), F32)],
        compiler_params=_cparams(("arbitrary",)),
        name="moe_route",
    )(xm, xt, nw, wr, br)


def _to_tiles(ref, base, val):
    m, rt = val.shape[0], val.shape[1] // LANES
    for j in range(rt):
        ref[pl.ds(base * rt + j, m, stride=rt), :] = val[:, j * LANES:(j + 1) * LANES]


def _from_tiles(ref, base, m, rt):
    return jnp.concatenate([ref[pl.ds(base * rt + j, m, stride=rt), :] for j in range(rt)], axis=1)


SEG_W = 16
WIN_HDR = 2


def _max_windows(tm):
    return N_EXPERTS + TOP_K * tm // SEG_W


def _seg_copy(src, i, dst, j, sem, rt):
    n = SEG_W * rt
    return pltpu.make_async_copy(src.at[pl.ds(pl.multiple_of(i * rt, rt), n), :],
                                 dst.at[pl.ds(pl.multiple_of(j * rt, rt), n), :], sem)


def _tok(ref, p, rt):
    return ref.at[pl.ds(pl.multiple_of(p * rt, rt), rt), :]


def _dispatch_body(lpos_ref, win_ref, zwin_ref, xm_ref, xt_ref, nw_ref, xb_ref, h_buf, s_buf, sem, *, ntm):
    tm, rt = xm_ref.shape[0], xm_ref.shape[1] // LANES
    i = pl.program_id(0)
    half = TOP_K * tm + SEG_W
    sbase = (i % 2) * half
    mw = _max_windows(tm)

    @pl.when(i == 0)
    def _():
        for hb in range(2):
            s_buf[(hb * half + TOP_K * tm) * rt:(hb + 1) * half * rt, :] = jnp.zeros((SEG_W * rt, LANES), F32)

        def zissue(w, carry):
            _seg_copy(s_buf, TOP_K * tm, xb_ref, zwin_ref[1 + w], sem, rt).start()
            return carry

        def zdrain(w, carry):
            _seg_copy(s_buf, 0, xb_ref, 0, sem, rt).wait()
            return carry

        lax.fori_loop(0, zwin_ref[0], zissue, 0)
        lax.fori_loop(0, zwin_ref[0], zdrain, 0)

    _to_tiles(h_buf, 0, _rms(_x_tile(xm_ref, xt_ref, ntm), nw_ref[...]))

    def move(t, carry):
        v = _tok(h_buf, t, rt)[...]
        for k in range(TOP_K):
            _tok(s_buf, sbase + lpos_ref[0, 0, k * tm + t], rt)[...] = v
        return carry

    lax.fori_loop(0, tm, move, 0, unroll=8)

    def drain(w, carry):
        _seg_copy(s_buf, 0, xb_ref, 0, sem, rt).wait()
        return carry

    @pl.when(i > 0)
    def _():
        lax.fori_loop(0, win_ref[0, 0, 1], drain, 0)

    def issue(w, carry):
        _seg_copy(s_buf, sbase + win_ref[0, 0, WIN_HDR + w], xb_ref, win_ref[0, 0, WIN_HDR + mw + w], sem, rt).start()
        return carry

    lax.fori_loop(0, win_ref[0, 0, 0], issue, 0)

    @pl.when(i == pl.num_programs(0) - 1)
    def _():
        lax.fori_loop(0, win_ref[0, 0, 0], drain, 0)


def _dispatch(xm, xt, nw, lpos, win, zwin, *, tm, n_slots):
    d = xm.shape[1]
    ntm = xm.shape[0] // tm
    nt = ntm + 1
    rt = d // LANES
    return pl.pallas_call(
        functools.partial(_dispatch_body, ntm=ntm),
        grid=(nt,),
        in_specs=[
            pl.BlockSpec((1, 1, lpos.shape[2]), lambda i: (i, 0, 0), memory_space=pltpu.SMEM),
            pl.BlockSpec((1, 1, win.shape[2]), lambda i: (i, 0, 0), memory_space=pltpu.SMEM),
            pl.BlockSpec(memory_space=pltpu.SMEM),
        ] + _x_specs(xm, tm) + [
            _full(nw.shape),
        ],
        out_specs=pl.BlockSpec(memory_space=pl.ANY),
        out_shape=jax.ShapeDtypeStruct((n_slots * rt, LANES), F32),
        scratch_shapes=[pltpu.VMEM((tm * rt, LANES), F32),
                        pltpu.VMEM((2 * (TOP_K * tm + SEG_W) * rt, LANES), F32),
                        pltpu.SemaphoreType.DMA(())],
        compiler_params=_cparams(("arbitrary",)),
        name="moe_dispatch",
    )(lpos, win, zwin, xm, xt, nw)


def _expert_body(be_ref, nu_ref, xb_ref, wg_ref, wu_ref, wd_ref, yb_ref, wg_buf, wu_buf, wd_buf):
    b = pl.program_id(0)
    prev = be_ref[jnp.maximum(b - 1, 0)]
    fresh = (b == 0) | (be_ref[b] != prev)

    @pl.when((b < nu_ref[0]) & fresh)
    def _():
        wg_buf[...] = wg_ref[0, 0].astype(BF16)
        wu_buf[...] = wu_ref[0, 0].astype(BF16)
        wd_buf[...] = wd_ref[0, 0].astype(BF16)

    @pl.when(b < nu_ref[0])
    def _():
        xb = _from_tiles(xb_ref, 0, MOE_BLOCK, wg_buf.shape[0] // LANES).astype(BF16)
        gate = jnp.dot(xb, wg_buf[...], preferred_element_type=F32)
        up = jnp.dot(xb, wu_buf[...], preferred_element_type=F32)
        hid = (_silu(gate) * up).astype(BF16)
        _to_tiles(yb_ref, 0, jnp.dot(hid, wd_buf[...], preferred_element_type=F32))

    @pl.when(b >= nu_ref[0])
    def _():
        yb_ref[...] = jnp.zeros_like(yb_ref)


def _experts(blk_exp, n_used, xb, wg, wu, wd, *, layer):
    d, f = wg.shape[2], wg.shape[3]
    rt = d // LANES
    nb = xb.shape[0] // rt // MOE_BLOCK
    blk_rows = MOE_BLOCK * rt

    def xmap(b, be, nu):
        return (jnp.minimum(b, jnp.maximum(nu[0] - 1, 0)), 0)

    def wmap(b, be, nu):
        return (layer, be[b], 0, 0)

    return pl.pallas_call(
        _expert_body,
        grid_spec=pltpu.PrefetchScalarGridSpec(
            num_scalar_prefetch=2,
            grid=(nb,),
            in_specs=[
                pl.BlockSpec((blk_rows, LANES), xmap),
                pl.BlockSpec((1, 1, d, f), wmap), pl.BlockSpec((1, 1, d, f), wmap),
                pl.BlockSpec((1, 1, f, d), wmap),
            ],
            out_specs=pl.BlockSpec((blk_rows, LANES), lambda b, be, nu: (b, 0)),
            scratch_shapes=[pltpu.VMEM((d, f), BF16), pltpu.VMEM((d, f), BF16), pltpu.VMEM((f, d), BF16)],
        ),
        out_shape=jax.ShapeDtypeStruct(xb.shape, F32),
        compiler_params=_cparams(("arbitrary",)),
        name="moe_experts",
    )(blk_exp, n_used, xb, wg, wu, wd)


def _ybuf_tokens(tm):
    return TOP_K * tm + N_EXPERTS * (SEG_W - 1) + SEG_W


def _combine_body(lpos_ref, gate_ref, win_ref, winn_ref, xm_ref, xt_ref, fw_ref, yb_ref, om_ref, ot_ref,
                  y_buf, x_buf, sem, *, ntm, final_norm):
    tm, rt = xm_ref.shape[0], xm_ref.shape[1] // LANES
    i = pl.program_id(0)
    slot = i % 2
    half = _ybuf_tokens(tm)
    mw = _max_windows(tm)

    def fetch(tab_ref, sl):
        def issue(w, carry):
            _seg_copy(yb_ref, tab_ref[0, 0, WIN_HDR + w], y_buf, sl * half + tab_ref[0, 0, WIN_HDR + mw + w],
                      sem.at[sl], rt).start()
            return carry

        lax.fori_loop(0, tab_ref[0, 0, 0], issue, 0)

    @pl.when(i == 0)
    def _():
        fetch(win_ref, 0)

    @pl.when(i + 1 < pl.num_programs(0))
    def _():
        fetch(winn_ref, 1 - slot)

    _to_tiles(x_buf, 0, _x_tile(xm_ref, xt_ref, ntm))

    def drain(w, carry):
        _seg_copy(yb_ref, 0, y_buf, 0, sem.at[slot], rt).wait()
        return carry

    lax.fori_loop(0, win_ref[0, 0, 0], drain, 0)
    ybase = slot * half

    def comb(t, carry):
        acc = _tok(x_buf, t, rt)[...]
        for k in range(TOP_K):
            a = k * tm + t
            acc = acc + gate_ref[0, 0, a] * _tok(y_buf, ybase + lpos_ref[0, 0, a], rt)[...]
        _tok(x_buf, t, rt)[...] = acc
        return carry

    lax.fori_loop(0, tm, comb, 0, unroll=8)
    out = _from_tiles(x_buf, 0, tm, rt)
    if final_norm:
        out = _rms(out, fw_ref[...])

    @pl.when(i < ntm)
    def _():
        om_ref[...] = out

    @pl.when(i == ntm)
    def _():
        ot_ref[...] = out


def _combine(xm, xt, lpos, gates, win, yb, fw, *, tm, final_norm):
    d = xm.shape[1]
    ntm = xm.shape[0] // tm
    nt = ntm + 1
    rt = d // LANES

    def smem(arr, imap):
        return pl.BlockSpec((1, 1, arr.shape[2]), imap, memory_space=pltpu.SMEM)

    return pl.pallas_call(
        functools.partial(_combine_body, ntm=ntm, final_norm=final_norm),
        grid=(nt,),
        in_specs=[
            smem(lpos, lambda i: (i, 0, 0)),
            smem(gates, lambda i: (i, 0, 0)),
            smem(win, lambda i: (i, 0, 0)),
            smem(win, lambda i: (jnp.minimum(i + 1, nt - 1), 0, 0)),
        ] + _x_specs(xm, tm) + [
            _full(fw.shape),
            pl.BlockSpec(memory_space=pl.ANY),
        ],
        out_specs=_x_specs(xm, tm),
        out_shape=[jax.ShapeDtypeStruct(xm.shape, F32), jax.ShapeDtypeStruct(xt.shape, F32)],
        scratch_shapes=[pltpu.VMEM((2 * _ybuf_tokens(tm) * rt, LANES), F32),
                        pltpu.VMEM((tm * rt, LANES), F32),
                        pltpu.SemaphoreType.DMA((2,))],
        compiler_params=_cparams(("arbitrary",)),
        name="moe_combine",
    )(lpos, gates, win, win, xm, xt, fw, yb)


def _moe(xm, xt, nw, w_group, b_group, w_expert, b_expert, wg, wu, wd, fw, *, layer, final_norm):
    tm, d = xt.shape
    nt = xm.shape[0] // tm + 1
    t = nt * tm
    pad = LANES - N_EXPERTS - N_EXPERT_GROUPS
    wr = jnp.pad(jnp.concatenate([w_expert, w_group], axis=1), ((0, 0), (0, pad)))
    wr_hi = wr.astype(BF16)
    wr = jnp.concatenate([wr_hi, (wr - wr_hi.astype(F32)).astype(BF16)], axis=1)
    br = jnp.pad(jnp.concatenate([b_expert, b_group]), (0, pad)).reshape(1, LANES)
    nw2 = nw.reshape(1, d)
    info, cnt = _route(xm, xt, nw2, wr, br, tm=tm)

    def per_tile(rows):
        return rows.reshape(TOP_K, nt, tm).transpose(1, 0, 2).reshape(nt, 1, TOP_K * tm)

    gates = per_tile(info[0:TOP_K])
    lpos = per_tile(info[4:4 + TOP_K].astype(I32))
    lpos_al = per_tile(info[6:6 + TOP_K].astype(I32))
    cnt = cnt[:, 0, :N_EXPERTS].astype(I32)
    total = jnp.sum(cnt, axis=0)
    padded = jnp.where(total > 0, (total + SEG_W + MOE_BLOCK - 2) // MOE_BLOCK * MOE_BLOCK, 0)
    pend = jnp.cumsum(padded)
    pstart = pend - padded
    gstart = pstart[None, :] + jnp.cumsum(cnt, axis=0) - cnt
    lstart = jnp.cumsum(cnt, axis=1) - cnt
    nwin_e = (cnt + SEG_W - 1) // SEG_W
    lstart_al = (jnp.cumsum(nwin_e, axis=1) - nwin_e) * SEG_W

    winc = jnp.cumsum(nwin_e, axis=1)
    nwin = winc[:, -1:]
    mw = _max_windows(tm)
    j = jnp.arange(mw, dtype=I32)[None, :]
    owner = (jnp.sum(winc[:, None, :] <= j[:, :, None], axis=-1)[:, :, None]
             == jnp.arange(N_EXPERTS, dtype=I32)[None, None, :])
    pick = lambda tab: jnp.sum(jnp.where(owner, tab[:, None, :], 0), axis=-1)
    w_off = (j - pick(winc - nwin_e)) * SEG_W
    live = j < nwin
    src_loc = jnp.where(live, pick(lstart) + w_off, 0)
    slot_g = jnp.where(live, pick(gstart) + w_off, 0)
    dst_loc = jnp.where(live, pick(lstart_al) + w_off, 0)
    nprev = jnp.concatenate([jnp.zeros((1, 1), I32), nwin[:-1]], axis=0)
    win_d = jnp.concatenate([nwin, nprev, src_loc, slot_g], axis=1).reshape(nt, 1, WIN_HDR + 2 * mw)
    win_c = jnp.concatenate([nwin, nprev, slot_g, dst_loc], axis=1).reshape(nt, 1, WIN_HDR + 2 * mw)
    n_blocks = -(-(t * TOP_K + N_EXPERTS * (MOE_BLOCK + SEG_W - 2)) // MOE_BLOCK)
    n_slots = n_blocks * MOE_BLOCK
    zfirst = jnp.concatenate([pstart + total // SEG_W * SEG_W, pend[-1:]])
    zend = jnp.concatenate([pend, jnp.full((1,), n_slots, I32)])
    nz_e = (zend - zfirst) // SEG_W
    zinc = jnp.cumsum(nz_e)
    mz = N_EXPERTS * ((MOE_BLOCK + 2 * SEG_W) // SEG_W) + n_slots // SEG_W - TOP_K * t // SEG_W
    jz = jnp.arange(mz, dtype=I32)
    zowner = (jnp.sum(zinc[None, :] <= jz[:, None], axis=-1)[:, None]
              == jnp.arange(N_EXPERTS + 1, dtype=I32)[None, :])
    zpick = lambda tab: jnp.sum(jnp.where(zowner, tab[None, :], 0), axis=-1)
    zslot = jnp.where(jz < zinc[-1], zpick(zfirst) + (jz - zpick(zinc - nz_e)) * SEG_W, 0)
    zwin = jnp.concatenate([zinc[-1:], zslot]).astype(I32)
    blk_exp = jnp.minimum(jnp.sum(pend[None, :] <= (jnp.arange(n_blocks, dtype=I32) * MOE_BLOCK)[:, None], axis=1),
                          N_EXPERTS - 1).astype(I32)
    n_used = (pend[-1] // MOE_BLOCK).astype(I32).reshape(1)

    xb = _dispatch(xm, xt, nw2, lpos, win_d, zwin, tm=tm, n_slots=n_slots)
    yb = _experts(blk_exp, n_used, xb, wg, wu, wd, layer=layer)
    return _combine(xm, xt, lpos_al, gates, win_c, yb, fw.reshape(1, d), tm=tm, final_norm=final_norm)


def _linear_body(x_ref, nw_ref, w_ref, b_ref, r_ref, out_ref, *, norm):
    x = x_ref[...]
    if norm:
        x = _rms(x, nw_ref[...])
    out_ref[...] = _bdot(x, w_ref[...]) + b_ref[...] + r_ref[...]


def _linear(x, w, *, nw=None, bias=None, res=None, tn):
    m, kd = x.shape
    n = w.shape[1]
    norm = nw is not None
    nw = jnp.ones((1, kd), F32) if nw is None else nw
    bias = jnp.zeros((1, n), F32) if bias is None else bias
    res = jnp.zeros((m, n), F32) if res is None else res
    tn = min(tn, n)
    return pl.pallas_call(
        functools.partial(_linear_body, norm=norm),
        grid=(n // tn,),
        in_specs=[
            _full(x.shape), _full(nw.shape),
            pl.BlockSpec((kd, tn), lambda j: (0, j)),
            pl.BlockSpec((1, tn), lambda j: (0, j)),
            pl.BlockSpec((m, tn), lambda j: (0, j)),
        ],
        out_specs=pl.BlockSpec((m, tn), lambda j: (0, j)),
        out_shape=jax.ShapeDtypeStruct((m, n), F32),
        compiler_params=_cparams(("arbitrary",)),
        name="sample_linear",
    )(x, nw, w, bias, res)


def _sample_conv_body(xbc_ref, st_ref, cw_ref, cb_ref, dtr_ref, dtb_ref, xc_ref, stn_ref, dt_ref, cbg_ref,
                      *, d_inner):
    nst = D_STATE
    xbc = xbc_ref[...]
    acc = cb_ref[...] + cw_ref[3:4, :] * xbc
    for k in range(CONV_W - 1):
        acc = acc + cw_ref[k:k + 1, :] * st_ref[k]
    xc = _silu(acc)
    xc_ref[...] = xc
    stn_ref[0] = st_ref[1]
    stn_ref[1] = st_ref[2]
    stn_ref[2] = xbc
    dt_ref[...] = _softplus(dtr_ref[...] + dtb_ref[...])
    lane = lax.broadcasted_iota(I32, (xbc.shape[0], LANES), 1)
    cbg = jnp.zeros((xbc.shape[0], LANES), F32)
    for g in range(N_BC_GROUPS):
        b_g = xc[:, d_inner + g * nst:d_inner + (g + 1) * nst]
        c_g = xc[:, d_inner + (N_BC_GROUPS + g) * nst:d_inner + (N_BC_GROUPS + g + 1) * nst]
        cbg = jnp.where(lane == g, jnp.sum(b_g * c_g, axis=-1, keepdims=True), cbg)
    cbg_ref[...] = cbg


def _sample_conv(xbc, st_t, cw, cb, dtr, dtb, *, d_inner):
    m, cd = xbc.shape
    return pl.pallas_call(
        functools.partial(_sample_conv_body, d_inner=d_inner),
        grid=(1,),
        in_specs=[_full(xbc.shape), _full(st_t.shape), _full(cw.shape), _full(cb.shape), _full(dtr.shape),
                  _full(dtb.shape)],
        out_specs=[_full((m, cd)), _full(st_t.shape), _full((m, LANES)), _full((m, LANES))],
        out_shape=[jax.ShapeDtypeStruct((m, cd), F32), jax.ShapeDtypeStruct(st_t.shape, F32),
                   jax.ShapeDtypeStruct((m, LANES), F32), jax.ShapeDtypeStruct((m, LANES), F32)],
        compiler_params=_cparams(("arbitrary",)),
        name="sample_conv",
    )(xbc, st_t, cw, cb, dtr, dtb)


SSD_REQS_PER_STEP = 4


def _sample_ssd_body(s0_ref, xt_ref, bc_ref, hs_ref, par_ref, sn_ref, yt_ref, *, n_heads):
    hp = SSM_HEAD_DIM
    hpg = n_heads // N_BC_GROUPS
    a = -jnp.exp(par_ref[0:1, :])
    dsk = par_ref[1:2, :]
    head_row = lax.broadcasted_iota(I32, (n_heads, s0_ref.shape[3]), 0)
    for r in range(s0_ref.shape[0]):
        xt = xt_ref[r]
        dt = hs_ref[r, 0:1, :]
        cbh = hs_ref[r, 1:2, :]
        dec = jnp.exp(dt * a)
        xdt = xt * dt
        xdt_b = xdt.astype(BF16)
        yoff = jnp.zeros((hp, n_heads), F32)
        for hh in range(n_heads):
            g = hh // hpg
            b_row = bc_ref[r, g:g + 1, :]
            c_row = bc_ref[r, N_BC_GROUPS + g:N_BC_GROUPS + g + 1, :]
            s0 = s0_ref[r, hh]
            yoff = yoff + _bdot_nt(s0, jnp.where(head_row == hh, c_row, 0.0))
            b_sel = jnp.where(head_row == hh, b_row, 0.0).astype(BF16)
            sn_ref[r, hh] = s0 * dec[:, hh:hh + 1] + jnp.dot(xdt_b, b_sel, preferred_element_type=F32)
        yt_ref[r] = cbh * xdt + yoff * dec + xt * dsk


def _sample_ssd(s0, xt, bc, hs, par):
    bsz, n_heads, hp, nst = s0.shape
    rb = SSD_REQS_PER_STEP if bsz % SSD_REQS_PER_STEP == 0 else 1
    return pl.pallas_call(
        functools.partial(_sample_ssd_body, n_heads=n_heads),
        grid=(bsz // rb,),
        in_specs=[
            pl.BlockSpec((rb, n_heads, hp, nst), lambda b: (b, 0, 0, 0)),
            pl.BlockSpec((rb, hp, n_heads), lambda b: (b, 0, 0)),
            pl.BlockSpec((rb,) + bc.shape[1:], lambda b: (b, 0, 0)),
            pl.BlockSpec((rb,) + hs.shape[1:], lambda b: (b, 0, 0)),
            _full(par.shape),
        ],
        out_specs=[
            pl.BlockSpec((rb, n_heads, hp, nst), lambda b: (b, 0, 0, 0)),
            pl.BlockSpec((rb, hp, n_heads), lambda b: (b, 0, 0)),
        ],
        out_shape=[jax.ShapeDtypeStruct(s0.shape, F32), jax.ShapeDtypeStruct((bsz, hp, n_heads), F32)],
        compiler_params=_cparams(("arbitrary",)),
        name="sample_ssd",
    )(s0, xt, bc, hs, par)


def _sample_gnorm_out_body(y_ref, z_ref, gnw_ref, wout_ref, x_ref, out_ref, *, d_inner):
    gw = d_inner // N_BC_GROUPS
    acc = x_ref[...]
    for g in range(N_BC_GROUPS):
        glanes = slice(g * gw, (g + 1) * gw)
        gg = y_ref[:, glanes] * _silu(z_ref[:, glanes])
        gg = gg * lax.rsqrt(jnp.mean(gg * gg, axis=-1, keepdims=True) + EPS) * gnw_ref[:, glanes]
        acc = acc + jnp.dot(gg.astype(BF16), wout_ref[glanes, :], preferred_element_type=F32)
    out_ref[...] = acc


def _sample_gnorm_out(y, z, gnw, wout, x):
    d_inner = y.shape[1]
    return pl.pallas_call(
        functools.partial(_sample_gnorm_out_body, d_inner=d_inner),
        grid=(1,),
        in_specs=[_full(y.shape), _full(z.shape), _full(gnw.shape), _full(wout.shape), _full(x.shape)],
        out_specs=_full(x.shape),
        out_shape=jax.ShapeDtypeStruct(x.shape, F32),
        compiler_params=_cparams(("arbitrary",)),
        name="sample_gnorm_out",
    )(y, z, gnw, wout, x)


def _sample_attn_body(q_ref, kn_ref, vn_ref, kc_ref, vc_ref, sink_ref, o_ref, ko_ref, vo_ref, s_buf, sn_buf):
    bt = q_ref.shape[0]
    wb = kc_ref.shape[1]
    hd = HEAD_DIM
    nh = N_Q_HEADS
    qpk = N_Q_HEADS // N_KV_HEADS
    scale = hd ** -0.5
    for b in range(bt):
        kn = kn_ref[b]
        kn_h = jnp.concatenate([jnp.broadcast_to(kn[:, g * hd:(g + 1) * hd], (qpk, hd))
                                for g in range(N_KV_HEADS)], axis=0)
        sn_buf[b * nh:(b + 1) * nh, :] = jnp.sum(q_ref[b] * kn_h, axis=-1, keepdims=True) * scale
        for g in range(N_KV_HEADS):
            rows = slice(b * nh + g * qpk, b * nh + (g + 1) * qpk)
            s_buf[rows, :] = _bdot_nt(q_ref[b, g * qpk:(g + 1) * qpk, :], kc_ref[b, :, g * hd:(g + 1) * hd]) * scale
    s = s_buf[...]
    s_new = sn_buf[...]
    sink = sink_ref[...]
    m = jnp.maximum(jnp.maximum(jnp.max(s, axis=-1, keepdims=True), s_new), sink)
    p = jnp.exp(s - m)
    p_new = jnp.exp(s_new - m)
    inv = 1.0 / (jnp.sum(p, axis=-1, keepdims=True) + p_new + jnp.exp(sink - m))
    s_buf[...] = p * inv
    sn_buf[...] = p_new * inv
    for b in range(bt):
        vn = vn_ref[b]
        for g in range(N_KV_HEADS):
            rows = slice(b * nh + g * qpk, b * nh + (g + 1) * qpk)
            cols = slice(g * hd, (g + 1) * hd)
            o_ref[b, g * qpk:(g + 1) * qpk, :] = (_bdot(s_buf[rows, :], vc_ref[b, :, cols])
                                                   + sn_buf[rows, :] * vn[:, cols])
        ko_ref[b, 0:wb - 1, :] = kc_ref[b, 1:wb, :]
        ko_ref[b, wb - 1:wb, :] = kn_ref[b]
        vo_ref[b, 0:wb - 1, :] = vc_ref[b, 1:wb, :]
        vo_ref[b, wb - 1:wb, :] = vn


def _sample_attn(q3, kn, vn, kc, vc, sinks, *, bt):
    bsz, nqh, hd = q3.shape
    wb, nk = kc.shape[1], kc.shape[2]
    return pl.pallas_call(
        _sample_attn_body,
        grid=(bsz // bt,),
        in_specs=[
            pl.BlockSpec((bt, nqh, hd), lambda i: (i, 0, 0)),
            pl.BlockSpec((bt, 1, nk), lambda i: (i, 0, 0)),
            pl.BlockSpec((bt, 1, nk), lambda i: (i, 0, 0)),
            pl.BlockSpec((bt, wb, nk), lambda i: (i, 0, 0)),
            pl.BlockSpec((bt, wb, nk), lambda i: (i, 0, 0)),
            _full(sinks.shape),
        ],
        out_specs=[
            pl.BlockSpec((bt, nqh, hd), lambda i: (i, 0, 0)),
            pl.BlockSpec((bt, wb, nk), lambda i: (i, 0, 0)),
            pl.BlockSpec((bt, wb, nk), lambda i: (i, 0, 0)),
        ],
        out_shape=[jax.ShapeDtypeStruct(q3.shape, F32), jax.ShapeDtypeStruct(kc.shape, F32),
                   jax.ShapeDtypeStruct(vc.shape, F32)],
        scratch_shapes=[pltpu.VMEM((bt * nqh, wb), F32), pltpu.VMEM((bt * nqh, 1), F32)],
        compiler_params=_cparams(("arbitrary",)),
        name="sample_attn",
    )(q3, kn, vn, kc, vc, sinks)


def _mamba_sample(x, nw, mw, state_conv, state_ssm):
    win, cw, cb, dtb, alog, dsk, gnw, wout = mw
    bsz, d = x.shape
    d_inner = wout.shape[0]
    conv_dim = cw.shape[1]
    n_heads = d_inner // SSM_HEAD_DIM
    hp = SSM_HEAD_DIM
    proj = _linear(x, win, nw=nw, tn=896)
    z = proj[:, :d_inner]
    xbc = proj[:, d_inner:d_inner + conv_dim]
    dtr = proj[:, d_inner + conv_dim:]
    st_t = jnp.transpose(state_conv, (1, 0, 2))
    xc, stn_t, dt, cbg = _sample_conv(xbc, st_t, cw, cb, dtr, dtb, d_inner=d_inner)
    conv_new = jnp.transpose(stn_t, (1, 0, 2))
    xt = jnp.transpose(xc[:, :d_inner].reshape(bsz, n_heads, hp), (0, 2, 1))
    bc = xc[:, d_inner:].reshape(bsz, 2 * N_BC_GROUPS, D_STATE)
    cbh = jnp.repeat(cbg[:, :N_BC_GROUPS], n_heads // N_BC_GROUPS, axis=1)
    hs = jnp.stack([dt[:, :n_heads], cbh], axis=1)
    par = jnp.stack([alog[0, :n_heads], dsk.reshape(n_heads, hp)[:, 0]], axis=0)
    ssm_new, yt = _sample_ssd(state_ssm, xt, bc, hs, par)
    y = jnp.transpose(yt, (0, 2, 1)).reshape(bsz, d_inner)
    out = _sample_gnorm_out(y, z, gnw, wout, x)
    return out, conv_new, ssm_new


def _attn_sample(x, nw, wqkv, bqkv, sinks, wo, bo, cache_k, cache_v):
    bsz, d = x.shape
    wb = cache_k.shape[1]
    nq = N_Q_HEADS * HEAD_DIM
    nk = N_KV_HEADS * HEAD_DIM
    qkv = _linear(x, wqkv, nw=nw, bias=bqkv, tn=512)
    q3 = qkv[:, :nq].reshape(bsz, N_Q_HEADS, HEAD_DIM)
    kn = qkv[:, nq:nq + nk].reshape(bsz, 1, nk)
    vn = qkv[:, nq + nk:].reshape(bsz, 1, nk)
    o3, ko, vo = _sample_attn(q3, kn, vn, cache_k.reshape(bsz, wb, nk), cache_v.reshape(bsz, wb, nk),
                              jnp.tile(sinks.reshape(N_Q_HEADS, 1), (8, 1)), bt=8)
    out = _linear(o3.reshape(bsz, nq), wo, bias=bo, res=x, tn=512)
    return out, ko.reshape(cache_k.shape), vo.reshape(cache_v.shape)


def kernel(x_prompt, x_sample, state_ssm, state_conv, cache_k_win, cache_v_win,
           mamba_w_in, mamba_conv_w, mamba_conv_b, mamba_dt_bias, mamba_a_log, mamba_d,
           mamba_norm_w, mamba_w_out, attn_w_qkv, attn_b_qkv, attn_sinks, attn_w_o, attn_b_o,
           norm_mix, norm_ffn, router_w_group, router_b_group, router_w_expert, router_b_expert,
           expert_w_gate, expert_w_up, expert_w_down, norm_final):
    bsz, seq, d = x_prompt.shape
    dbsz, dseq, _ = x_sample.shape
    assert dseq == 1 and cache_k_win.shape[2] <= WINDOW and seq % WINDOW == 0
    assert dbsz <= MOE_TILE and (bsz * seq) % MOE_TILE == 0
    depth = norm_mix.shape[0]
    xp = x_prompt.reshape(bsz * seq, d)
    xs = x_sample.reshape(dbsz, d)
    ssm_p, conv_p, kp_l, vp_l = [], [], [], []
    ssm_s, conv_s, ks_l, vs_l = [], [], [], []
    for i in range(depth):
        j = i // 2
        nw = norm_mix[i].reshape(1, d)
        if i % 2 == 0:
            mw = _mamba_weights(mamba_w_in[j], mamba_conv_w[j], mamba_conv_b[j], mamba_dt_bias[j],
                                mamba_a_log[j], mamba_d[j], mamba_norm_w[j], mamba_w_out[j])
            xp, cp, sp = _mamba_prompt(xp, nw, *mw, bsz=bsz, seq=seq, ts=2 * SSD_CHUNK)
            xs, cs_, ss_ = _mamba_sample(xs, nw, mw, state_conv[j], state_ssm[j])
            ssm_p.append(sp)
            conv_p.append(cp)
            ssm_s.append(ss_)
            conv_s.append(cs_)
        else:
            wqkv = attn_w_qkv[j].astype(BF16)
            bqkv = attn_b_qkv[j].reshape(1, -1)
            wo = attn_w_o[j].astype(BF16)
            bo = attn_b_o[j].reshape(1, d)
            xp, kp, vp = _attn_prompt(xp, attn_sinks[j], nw, wqkv, bqkv, wo, bo, bsz=bsz, seq=seq,
                                      tq=WINDOW)
            xs, ks_, vs_ = _attn_sample(xs, nw, wqkv, bqkv, attn_sinks[j], wo, bo, cache_k_win[j], cache_v_win[j])
            kp_l.append(kp.reshape(bsz, WINDOW, N_KV_HEADS, HEAD_DIM))
            vp_l.append(vp.reshape(bsz, WINDOW, N_KV_HEADS, HEAD_DIM))
            ks_l.append(ks_)
            vs_l.append(vs_)
        last = i == depth - 1
        moe_w = (norm_ffn[i], router_w_group[i], router_b_group[i], router_w_expert[i], router_b_expert[i],
                 expert_w_gate, expert_w_up, expert_w_down, norm_final)
        xs_tile = jnp.pad(xs, ((0, MOE_TILE - dbsz), (0, 0)))
        xp, xs_tile = _moe(xp, xs_tile, *moe_w, layer=i, final_norm=last)
        xs = xs_tile[:dbsz]
    return (xp.reshape(bsz, seq, d), xs.reshape(dbsz, dseq, d),
            jnp.stack(ssm_p), jnp.stack(conv_p), jnp.stack(kp_l), jnp.stack(vp_l),
            jnp.stack(ssm_s), jnp.stack(conv_s), jnp.stack(ks_l), jnp.stack(vs_l))
```

```python
import functools
import math

import jax
import jax.numpy as jnp
from jax import lax
from jax.experimental import pallas as pl
from jax.experimental.pallas import tpu as pltpu

F32 = jnp.float32
BF16 = jnp.bfloat16
I32 = jnp.int32

EPS = 1e-5
LANES = 128
VMEM_LIMIT = 56 * 1024 * 1024

SSM_HEAD_DIM = 64
D_STATE = 128
N_BC_GROUPS = 8
CONV_W = 4
SSD_CHUNK = 128
N_Q_HEADS = 16
N_KV_HEADS = 4
HEAD_DIM = 64
WINDOW = 128
N_EXPERT_GROUPS = 4
EXPERTS_PER_GROUP = 8
N_EXPERTS = N_EXPERT_GROUPS * EXPERTS_PER_GROUP
TOP_K = 2
MOE_BLOCK = 256
MOE_TILE = 512


def _cparams(sem):
    return pltpu.CompilerParams(dimension_semantics=sem, vmem_limit_bytes=VMEM_LIMIT)


def _full(shape):
    n = len(shape)
    return pl.BlockSpec(shape, lambda *_: (0,) * n)


def _resident(shape):
    n = len(shape)
    return pl.BlockSpec(shape, lambda *_: (0,) * n, pipeline_mode=pl.Buffered(1))


def _rms(x, w):
    return x * lax.rsqrt(jnp.mean(x * x, axis=-1, keepdims=True) + EPS) * w


def _silu(x):
    return x / (1.0 + jnp.exp(-x))


def _softplus(x):
    return jnp.maximum(x, 0.0) + jnp.log(1.0 + jnp.exp(-jnp.abs(x)))


def _bdot(a, b):
    return jnp.dot(a.astype(BF16), b.astype(BF16), preferred_element_type=F32)


def _bdot_nt(a, b):
    return lax.dot_general(a.astype(BF16), b.astype(BF16), (((1,), (1,)), ((), ())),
                           preferred_element_type=F32)


def _bdot_tn(a, b):
    return lax.dot_general(a.astype(BF16), b.astype(BF16), (((0,), (0,)), ((), ())),
                           preferred_element_type=F32)


def _fdot(a, b):
    return jnp.dot(a, b, preferred_element_type=F32, precision=lax.Precision.HIGHEST)


def _spread(v, onehot_ref):
    hi = v.astype(BF16)
    lo = (v - hi.astype(F32)).astype(BF16)
    return jnp.dot(jnp.concatenate([hi, lo], axis=1), onehot_ref[...], preferred_element_type=F32)


def _mamba_prompt_body(x_ref, nw_ref, win_ref, cw_ref, cb_ref, dtb_ref, alog_ref, dsk_ref, gnw_ref,
                       wout_ref, hexp_ref, out_ref, conv_ref, ssm_ref, h_buf, xbc_buf, xc_buf, st_buf, y_buf,
                       wexp_buf, eexp_buf, *, d_inner, n_heads):
    ts = x_ref.shape[0]
    cs = SSD_CHUNK
    hp = SSM_HEAD_DIM
    nst = D_STATE
    gw = d_inner // N_BC_GROUPS
    hpg = n_heads // N_BC_GROUPS
    conv_dim = d_inner + 2 * N_BC_GROUPS * nst
    s = pl.program_id(1)

    @pl.when(s == 0)
    def _():
        xbc_buf[:, 0:8, :] = jnp.zeros((conv_dim // LANES, 8, LANES), F32)
        st_buf[...] = jnp.zeros_like(st_buf)

    h_buf[...] = _rms(x_ref[...], nw_ref[...]).astype(BF16)
    dtr = jnp.dot(h_buf[...], win_ref[:, d_inner + conv_dim:], preferred_element_type=F32)
    ct = 512
    spp = ct // LANES
    for j in range(conv_dim // ct):
        cols = slice(j * ct, (j + 1) * ct)
        piece = jnp.dot(h_buf[...], win_ref[:, d_inner + j * ct:d_inner + (j + 1) * ct],
                        preferred_element_type=F32)
        for q in range(spp):
            xbc_buf[j * spp + q, 8:8 + ts, :] = piece[:, q * LANES:(q + 1) * LANES]

        def back(k):
            return jnp.concatenate([xbc_buf[j * spp + q, pl.ds(8 - k, ts), :] for q in range(spp)], axis=1)

        acc = cb_ref[:, cols] + cw_ref[3:4, cols] * piece
        acc = acc + cw_ref[2:3, cols] * back(1)
        acc = acc + cw_ref[1:2, cols] * back(2)
        acc = acc + cw_ref[0:1, cols] * back(3)
        xc_buf[:, cols] = _silu(acc)
    for c in range(conv_dim // LANES):
        last3 = xbc_buf[c, 5 + ts:8 + ts, :]
        xbc_buf[c, 5:8, :] = last3
        conv_ref[0, :, c * LANES:(c + 1) * LANES] = last3

    dt = _softplus(dtr + dtb_ref[...])
    da = dt * (-jnp.exp(alog_ref[...]))
    row = lax.broadcasted_iota(I32, (cs, cs), 0)
    col = lax.broadcasted_iota(I32, (cs, cs), 1)
    causal = row >= col
    tril = causal.astype(F32)
    lane = lax.broadcasted_iota(I32, (cs, LANES), 1)
    lo_half = lane < hp

    for c in range(ts // cs):
        rows = slice(c * cs, (c + 1) * cs)
        da_c = da[rows]
        dt_c = dt[rows]
        acum = _fdot(tril, da_c)
        acum_t = acum.T
        dt_t = dt_c.T
        a_last = acum[cs - 1:cs, :]
        to_end = jnp.exp(a_last - acum)
        w_all = dt_c * to_end
        ea = jnp.exp(acum)
        cd = jnp.exp(a_last)
        both = _spread(jnp.concatenate([w_all, ea], axis=0), hexp_ref)
        wexp_buf[...] = both[0:cs]
        eexp_buf[...] = both[cs:2 * cs]
        for g in range(N_BC_GROUPS):
            glanes = slice(g * gw, (g + 1) * gw)
            b_g = xc_buf[rows, d_inner + g * nst:d_inner + (g + 1) * nst]
            c_g = xc_buf[rows, d_inner + (N_BC_GROUPS + g) * nst:d_inner + (N_BC_GROUPS + g + 1) * nst]
            cb = _bdot_nt(c_g, b_g)
            y_off = _bdot(c_g, st_buf[:, glanes])
            xw_parts = []
            for pr in range(hpg // 2):
                h0 = g * hpg + 2 * pr
                lanes0 = slice(h0 * hp, h0 * hp + 2 * hp)
                x_pair = xc_buf[rows, lanes0]
                y_pair = eexp_buf[:, lanes0] * y_off[:, 2 * pr * hp:2 * (pr + 1) * hp]
                for k in range(2):
                    hh = h0 + k
                    seg = acum[:, hh:hh + 1] - acum_t[hh:hh + 1, :]
                    dec = jnp.exp(jnp.where(causal, seg, -jnp.inf))
                    m = cb * dec * dt_t[hh:hh + 1, :]
                    sel = lo_half if k == 0 else jnp.logical_not(lo_half)
                    y_pair = y_pair + _bdot(m, jnp.where(sel, x_pair, 0.0))
                y_buf[rows, lanes0] = y_pair + x_pair * dsk_ref[:, lanes0]
                xw_parts.append(x_pair * wexp_buf[:, lanes0])
            xw = jnp.concatenate(xw_parts, axis=1)
            cd_parts = [jnp.broadcast_to(cd[:, g * hpg + k:g * hpg + k + 1], (1, hp)) for k in range(hpg)]
            cd_g = jnp.concatenate(cd_parts, axis=1)
            st_buf[:, glanes] = st_buf[:, glanes] * cd_g + _bdot_tn(b_g, xw)

    @pl.when(s == pl.num_programs(1) - 1)
    def _():
        for pr in range(n_heads // 2):
            t = st_buf[:, 2 * pr * hp:2 * (pr + 1) * hp].T
            ssm_ref[0, 2 * pr] = t[0:hp]
            ssm_ref[0, 2 * pr + 1] = t[hp:2 * hp]

    acc = x_ref[...]
    for g in range(N_BC_GROUPS):
        glanes = slice(g * gw, (g + 1) * gw)
        z = jnp.dot(h_buf[...], win_ref[:, glanes], preferred_element_type=F32)
        gg = y_buf[:, glanes] * _silu(z)
        gg = gg * lax.rsqrt(jnp.mean(gg * gg, axis=-1, keepdims=True) + EPS) * gnw_ref[:, glanes]
        acc = acc + jnp.dot(gg.astype(BF16), wout_ref[glanes, :], preferred_element_type=F32)
    out_ref[...] = acc


def _mamba_prompt(x, nw, win, cw, cb, dtb, alog, dsk, gnw, wout, *, bsz, seq, ts):
    d = x.shape[1]
    d_inner = wout.shape[0]
    n_heads = d_inner // SSM_HEAD_DIM
    conv_dim = cw.shape[1]
    ns = seq // ts
    body = functools.partial(_mamba_prompt_body, d_inner=d_inner, n_heads=n_heads)
    hexp = (jnp.arange(2 * LANES, dtype=I32)[:, None] % LANES
            == jnp.arange(d_inner, dtype=I32)[None, :] // SSM_HEAD_DIM).astype(BF16)
    return pl.pallas_call(
        body,
        grid=(bsz, ns),
        in_specs=[
            pl.BlockSpec((ts, d), lambda b, s: (b * ns + s, 0)),
            _full(nw.shape), _resident(win.shape), _full(cw.shape), _full(cb.shape), _full(dtb.shape),
            _full(alog.shape), _full(dsk.shape), _full(gnw.shape), _resident(wout.shape), _resident(hexp.shape),
        ],
        out_specs=[
            pl.BlockSpec((ts, d), lambda b, s: (b * ns + s, 0)),
            pl.BlockSpec((1, CONV_W - 1, conv_dim), lambda b, s: (b, 0, 0)),
            pl.BlockSpec((1, n_heads, SSM_HEAD_DIM, D_STATE), lambda b, s: (b, 0, 0, 0)),
        ],
        out_shape=[
            jax.ShapeDtypeStruct((bsz * seq, d), F32),
            jax.ShapeDtypeStruct((bsz, CONV_W - 1, conv_dim), F32),
            jax.ShapeDtypeStruct((bsz, n_heads, SSM_HEAD_DIM, D_STATE), F32),
        ],
        scratch_shapes=[
            pltpu.VMEM((ts, d), BF16),
            pltpu.VMEM((conv_dim // LANES, 8 + ts, LANES), F32),
            pltpu.VMEM((ts, conv_dim), F32),
            pltpu.VMEM((D_STATE, d_inner), F32),
            pltpu.VMEM((ts, d_inner), F32),
            pltpu.VMEM((SSD_CHUNK, d_inner), F32),
            pltpu.VMEM((SSD_CHUNK, d_inner), F32),
        ],
        compiler_params=_cparams(("arbitrary", "arbitrary")),
        name="mamba_prompt",
    )(x, nw, win, cw, cb, dtb, alog, dsk, gnw, wout, hexp)


def _mamba_weights(w_in, conv_w, conv_b, dt_bias, a_log, d_skip, norm_w, w_out):
    d_inner = w_out.shape[0]
    n_heads = dt_bias.shape[0]
    pad = LANES - n_heads
    win = jnp.pad(w_in, ((0, 0), (0, pad))).astype(BF16)
    dtb = jnp.pad(dt_bias, (0, pad)).reshape(1, LANES)
    alog = jnp.pad(a_log, (0, pad)).reshape(1, LANES)
    dsk = jnp.repeat(d_skip, SSM_HEAD_DIM).reshape(1, d_inner)
    return (win, conv_w, conv_b.reshape(1, -1), dtb, alog, dsk, norm_w.reshape(1, d_inner),
            w_out.astype(BF16))


def _sink_softmax_pv(s, sink, v):
    m = jnp.maximum(jnp.max(s, axis=-1, keepdims=True), sink)
    p = jnp.exp(s - m)
    denom = jnp.sum(p, axis=-1, keepdims=True) + jnp.exp(sink - m)
    return _bdot(p, v) / denom


def _attn_prompt_body(sink_ref, x_ref, nw_ref, wqkv_ref, bqkv_ref, wo_ref, bo_ref,
                      out_ref, kwin_ref, vwin_ref, kv_buf, q_buf, o_buf):
    blk = WINDOW
    hd = HEAD_DIM
    nq = N_Q_HEADS * hd
    nk = N_KV_HEADS * hd
    qpk = N_Q_HEADS // N_KV_HEADS
    tq = x_ref.shape[0]
    s_id = pl.program_id(1)

    @pl.when(s_id == 0)
    def _():
        kv_buf[0:blk, :] = jnp.zeros((blk, 2 * nk), F32)

    h = _rms(x_ref[...], nw_ref[...]).astype(BF16)
    q_buf[...] = jnp.dot(h, wqkv_ref[:, 0:nq], preferred_element_type=F32) + bqkv_ref[:, 0:nq]
    kv_buf[blk:blk + tq, :] = jnp.dot(h, wqkv_ref[:, nq:], preferred_element_type=F32) + bqkv_ref[:, nq:]
    kwin_ref[0] = kv_buf[tq:tq + blk, 0:nk]
    vwin_ref[0] = kv_buf[tq:tq + blk, nk:]

    row = lax.broadcasted_iota(I32, (blk, 2 * blk), 0)
    col = lax.broadcasted_iota(I32, (blk, 2 * blk), 1)
    diff = row + blk - col
    band = (diff >= 0) & (diff <= WINDOW)
    scale = hd ** -0.5
    for qb in range(tq // blk):
        qrows = slice(qb * blk, (qb + 1) * blk)
        krows = slice(qb * blk, (qb + 2) * blk)
        ok = band & ((col >= blk) | (s_id > 0)) if qb == 0 else band
        for g in range(N_KV_HEADS):
            k_g = kv_buf[krows, g * hd:(g + 1) * hd]
            v_g = kv_buf[krows, nk + g * hd:nk + (g + 1) * hd]
            for j in range(qpk):
                hh = g * qpk + j
                s = _bdot_nt(q_buf[qrows, hh * hd:(hh + 1) * hd], k_g) * scale
                s = jnp.where(ok, s, -jnp.inf)
                o_buf[qrows, hh * hd:(hh + 1) * hd] = _sink_softmax_pv(s, sink_ref[hh], v_g)
    kv_buf[0:blk, :] = kv_buf[tq:tq + blk, :]
    out_ref[...] = (x_ref[...] + jnp.dot(o_buf[...].astype(BF16), wo_ref[...], preferred_element_type=F32)
                    + bo_ref[...])


def _attn_prompt(x, sinks, nw, wqkv, bqkv, wo, bo, *, bsz, seq, tq):
    d = x.shape[1]
    blk = WINDOW
    nb = seq // tq
    nk = N_KV_HEADS * HEAD_DIM
    nq = N_Q_HEADS * HEAD_DIM
    return pl.pallas_call(
        _attn_prompt_body,
        grid=(bsz, nb),
        in_specs=[
            pl.BlockSpec(memory_space=pltpu.SMEM),
            pl.BlockSpec((tq, d), lambda b, s: (b * nb + s, 0)),
            _full(nw.shape), _full(wqkv.shape), _full(bqkv.shape), _full(wo.shape), _full(bo.shape),
        ],
        out_specs=[
            pl.BlockSpec((tq, d), lambda b, s: (b * nb + s, 0)),
            pl.BlockSpec((1, blk, nk), lambda b, s: (b, 0, 0)),
            pl.BlockSpec((1, blk, nk), lambda b, s: (b, 0, 0)),
        ],
        out_shape=[
            jax.ShapeDtypeStruct((bsz * seq, d), F32),
            jax.ShapeDtypeStruct((bsz, blk, nk), F32),
            jax.ShapeDtypeStruct((bsz, blk, nk), F32),
        ],
        scratch_shapes=[
            pltpu.VMEM((blk + tq, 2 * nk), F32),
            pltpu.VMEM((tq, nq), F32),
            pltpu.VMEM((tq, nq), F32),
        ],
        compiler_params=_cparams(("arbitrary", "arbitrary")),
        name="attn_prompt",
    )(sinks, x, nw, wqkv, bqkv, wo, bo)


def _x_specs(xm, tm):
    ntm, d = xm.shape[0] // tm, xm.shape[1]
    return [pl.BlockSpec((tm, d), lambda i: (jnp.minimum(i, ntm - 1), 0)), pl.BlockSpec((tm, d), lambda i: (0, 0))]


def _x_tile(xm_ref, xt_ref, ntm):
    return jnp.where(pl.program_id(0) < ntm, xm_ref[...], xt_ref[...])


def _route_body(xm_ref, xt_ref, nw_ref, wr_ref, br_ref, info_ref, cnt_ref, *, ntm):
    tm = xm_ref.shape[0]
    h = _rms(_x_tile(xm_ref, xt_ref, ntm), nw_ref[...])
    h_hi = h.astype(BF16)
    h_lo = (h - h_hi.astype(F32)).astype(BF16)
    part = jnp.dot(h_hi, wr_ref[...], preferred_element_type=F32)
    logits = (part[:, 0:LANES] + part[:, LANES:] + jnp.dot(h_lo, wr_ref[:, 0:LANES], preferred_element_type=F32)
              + br_ref[...])
    lane_i = lax.broadcasted_iota(I32, (tm, LANES), 1)
    lane = lane_i.astype(F32)
    lane_grp = (lane_i // EXPERTS_PER_GROUP).astype(F32)
    big = float(LANES)
    ninf = -jnp.inf

    def first_argmax(v):
        m = jnp.max(v, axis=-1, keepdims=True)
        return m, jnp.min(jnp.where(v == m, lane, big), axis=-1, keepdims=True)

    gmask = (lane_i >= N_EXPERTS) & (lane_i < N_EXPERTS + N_EXPERT_GROUPS)
    gl = jnp.where(gmask, logits, ninf)
    gmax, gi = first_argmax(gl)
    gi = gi - float(N_EXPERTS)
    pg = 1.0 / jnp.sum(jnp.exp(gl - gmax), axis=-1, keepdims=True)
    emask = (lane_i < N_EXPERTS) & (lane_grp == gi)
    el = jnp.where(emask, logits, ninf)
    m1, i1 = first_argmax(el)
    el2 = jnp.where(lane == i1, ninf, el)
    m2, i2 = first_argmax(el2)
    den = jnp.sum(jnp.exp(el - m1), axis=-1, keepdims=True)
    tp1 = 1.0 / den
    tp2 = jnp.exp(m2 - m1) / den
    g1 = pg * tp1 / (tp1 + tp2)
    g2 = pg * tp2 / (tp1 + tp2)
    hot1 = lane == i1
    hot2 = lane == i2
    onehot = jnp.where(hot1 | hot2, 1.0, 0.0)
    rr = lax.broadcasted_iota(I32, (tm, tm), 0)
    cc = lax.broadcasted_iota(I32, (tm, tm), 1)
    before = jnp.where(rr > cc, 1.0, 0.0)
    cum = _bdot(before, onehot)
    r1 = jnp.sum(jnp.where(hot1, cum, 0.0), axis=-1, keepdims=True)
    r2 = jnp.sum(jnp.where(hot2, cum, 0.0), axis=-1, keepdims=True)
    cnt_row = jnp.sum(onehot, axis=0, keepdims=True)
    nwin_row = jnp.floor((cnt_row + (SEG_W - 1.0)) * (1.0 / SEG_W))
    er = lax.broadcasted_iota(I32, (LANES, LANES), 0)
    ec = lax.broadcasted_iota(I32, (LANES, LANES), 1)
    earlier = jnp.where(er < ec, 1.0, 0.0)
    both = jnp.concatenate([jnp.broadcast_to(cnt_row, (8, LANES)), jnp.broadcast_to(nwin_row, (8, LANES))], axis=0)
    pre = _fdot(both, earlier)
    start = pre[0:1, :]
    start_al = pre[8:9, :] * float(SEG_W)

    def at(hot, row):
        return jnp.sum(jnp.where(hot, row, 0.0), axis=-1, keepdims=True)

    info = jnp.zeros((tm, LANES), F32)
    for k, v in enumerate((g1, g2, i1, i2, at(hot1, start) + r1, at(hot2, start) + r2,
                           at(hot1, start_al) + r1, at(hot2, start_al) + r2)):
        info = jnp.where(lane_i == k, v, info)
    info_ref[...] = info.T[0:8, :]
    cnt_ref[0] = jnp.broadcast_to(cnt_row, (8, LANES))


ROUTE_ROWS = 8


def _route(xm, xt, nw, wr, br, *, tm):
    ntm = xm.shape[0] // tm
    nt = ntm + 1
    return pl.pallas_call(
        functools.partial(_route_body, ntm=ntm),
        grid=(nt,),
        in_specs=_x_specs(xm, tm) + [_full(nw.shape), _full(wr.shape), _full(br.shape)],
        out_specs=[pl.BlockSpec((ROUTE_ROWS, tm), lambda i: (0, i)), pl.BlockSpec((1, 8, LANES), lambda i: (i, 0, 0))],
        out_shape=[jax.ShapeDtypeStruct((ROUTE_ROWS, nt * tm), F32), jax.ShapeDtypeStruct((nt, 8, LANES), F32)],
        compiler_params=_cparams(("arbitrary",)),
        name="moe_route",
    )(xm, xt, nw, wr, br)


def _to_tiles(ref, base, val):
    m, rt = val.shape[0], val.shape[1] // LANES
    for j in range(rt):
        ref[pl.ds(base * rt + j, m, stride=rt), :] = val[:, j * LANES:(j + 1) * LANES]


def _from_tiles(ref, base, m, rt):
    return jnp.concatenate([ref[pl.ds(base * rt + j, m, stride=rt), :] for j in range(rt)], axis=1)


SEG_W = 16
WIN_HDR = 2


def _max_windows(tm):
    return N_EXPERTS + TOP_K * tm // SEG_W


def _seg_copy(src, i, dst, j, sem, rt):
    n = SEG_W * rt
    return pltpu.make_async_copy(src.at[pl.ds(pl.multiple_of(i * rt, rt), n), :],
                                 dst.at[pl.ds(pl.multiple_of(j * rt, rt), n), :], sem)


def _tok(ref, p, rt):
    return ref.at[pl.ds(pl.multiple_of(p * rt, rt), rt), :]


def _dispatch_body(lpos_ref, win_ref, zwin_ref, xm_ref, xt_ref, nw_ref, xb_ref, h_buf, s_buf, sem, *, ntm):
    tm, rt = xm_ref.shape[0], xm_ref.shape[1] // LANES
    i = pl.program_id(0)
    half = TOP_K * tm + SEG_W
    sbase = (i % 2) * half
    mw = _max_windows(tm)

    @pl.when(i == 0)
    def _():
        for hb in range(2):
            s_buf[(hb * half + TOP_K * tm) * rt:(hb + 1) * half * rt, :] = jnp.zeros((SEG_W * rt, LANES), F32)

        def zissue(w, carry):
            _seg_copy(s_buf, TOP_K * tm, xb_ref, zwin_ref[1 + w], sem, rt).start()
            return carry

        def zdrain(w, carry):
            _seg_copy(s_buf, 0, xb_ref, 0, sem, rt).wait()
            return carry

        lax.fori_loop(0, zwin_ref[0], zissue, 0)
        lax.fori_loop(0, zwin_ref[0], zdrain, 0)

    _to_tiles(h_buf, 0, _rms(_x_tile(xm_ref, xt_ref, ntm), nw_ref[...]))

    def move(t, carry):
        v = _tok(h_buf, t, rt)[...]
        for k in range(TOP_K):
            _tok(s_buf, sbase + lpos_ref[0, 0, k * tm + t], rt)[...] = v
        return carry

    lax.fori_loop(0, tm, move, 0, unroll=8)

    def drain(w, carry):
        _seg_copy(s_buf, 0, xb_ref, 0, sem, rt).wait()
        return carry

    @pl.when(i > 0)
    def _():
        lax.fori_loop(0, win_ref[0, 0, 1], drain, 0)

    def issue(w, carry):
        _seg_copy(s_buf, sbase + win_ref[0, 0, WIN_HDR + w], xb_ref, win_ref[0, 0, WIN_HDR + mw + w], sem, rt).start()
        return carry

    lax.fori_loop(0, win_ref[0, 0, 0], issue, 0)

    @pl.when(i == pl.num_programs(0) - 1)
    def _():
        lax.fori_loop(0, win_ref[0, 0, 0], drain, 0)


def _dispatch(xm, xt, nw, lpos, win, zwin, *, tm, n_slots):
    d = xm.shape[1]
    ntm = xm.shape[0] // tm
    nt = ntm + 1
    rt = d // LANES
    return pl.pallas_call(
        functools.partial(_dispatch_body, ntm=ntm),
        grid=(nt,),
        in_specs=[
            pl.BlockSpec((1, 1, lpos.shape[2]), lambda i: (i, 0, 0), memory_space=pltpu.SMEM),
            pl.BlockSpec((1, 1, win.shape[2]), lambda i: (i, 0, 0), memory_space=pltpu.SMEM),
            pl.BlockSpec(memory_space=pltpu.SMEM),
        ] + _x_specs(xm, tm) + [
            _full(nw.shape),
        ],
        out_specs=pl.BlockSpec(memory_space=pl.ANY),
        out_shape=jax.ShapeDtypeStruct((n_slots * rt, LANES), F32),
        scratch_shapes=[pltpu.VMEM((tm * rt, LANES), F32),
                        pltpu.VMEM((2 * (TOP_K * tm + SEG_W) * rt, LANES), F32),
                        pltpu.SemaphoreType.DMA(())],
        compiler_params=_cparams(("arbitrary",)),
        name="moe_dispatch",
    )(lpos, win, zwin, xm, xt, nw)


def _expert_body(be_ref, nu_ref, xb_ref, wg_ref, wu_ref, wd_ref, yb_ref, wg_buf, wu_buf, wd_buf):
    b = pl.program_id(0)
    prev = be_ref[jnp.maximum(b - 1, 0)]
    fresh = (b == 0) | (be_ref[b] != prev)

    @pl.when((b < nu_ref[0]) & fresh)
    def _():
        wg_buf[...] = wg_ref[0, 0].astype(BF16)
        wu_buf[...] = wu_ref[0, 0].astype(BF16)
        wd_buf[...] = wd_ref[0, 0].astype(BF16)

    @pl.when(b < nu_ref[0])
    def _():
        xb = _from_tiles(xb_ref, 0, MOE_BLOCK, wg_buf.shape[0] // LANES).astype(BF16)
        gate = jnp.dot(xb, wg_buf[...], preferred_element_type=F32)
        up = jnp.dot(xb, wu_buf[...], preferred_element_type=F32)
        hid = (_silu(gate) * up).astype(BF16)
        _to_tiles(yb_ref, 0, jnp.dot(hid, wd_buf[...], preferred_element_type=F32))

    @pl.when(b >= nu_ref[0])
    def _():
        yb_ref[...] = jnp.zeros_like(yb_ref)


def _experts(blk_exp, n_used, xb, wg, wu, wd, *, layer):
    d, f = wg.shape[2], wg.shape[3]
    rt = d // LANES
    nb = xb.shape[0] // rt // MOE_BLOCK
    blk_rows = MOE_BLOCK * rt

    def xmap(b, be, nu):
        return (jnp.minimum(b, jnp.maximum(nu[0] - 1, 0)), 0)

    def wmap(b, be, nu):
        return (layer, be[b], 0, 0)

    return pl.pallas_call(
        _expert_body,
        grid_spec=pltpu.PrefetchScalarGridSpec(
            num_scalar_prefetch=2,
            grid=(nb,),
            in_specs=[
                pl.BlockSpec((blk_rows, LANES), xmap),
                pl.BlockSpec((1, 1, d, f), wmap), pl.BlockSpec((1, 1, d, f), wmap),
                pl.BlockSpec((1, 1, f, d), wmap),
            ],
            out_specs=pl.BlockSpec((blk_rows, LANES), lambda b, be, nu: (b, 0)),
            scratch_shapes=[pltpu.VMEM((d, f), BF16), pltpu.VMEM((d, f), BF16), pltpu.VMEM((f, d), BF16)],
        ),
        out_shape=jax.ShapeDtypeStruct(xb.shape, F32),
        compiler_params=_cparams(("arbitrary",)),
        name="moe_experts",
    )(blk_exp, n_used, xb, wg, wu, wd)


def _ybuf_tokens(tm):
    return TOP_K * tm + N_EXPERTS * (SEG_W - 1) + SEG_W


def _combine_body(lpos_ref, gate_ref, win_ref, winn_ref, xm_ref, xt_ref, fw_ref, yb_ref, om_ref, ot_ref,
                  y_buf, x_buf, sem, *, ntm, final_norm):
    tm, rt = xm_ref.shape[0], xm_ref.shape[1] // LANES
    i = pl.program_id(0)
    slot = i % 2
    half = _ybuf_tokens(tm)
    mw = _max_windows(tm)

    def fetch(tab_ref, sl):
        def issue(w, carry):
            _seg_copy(yb_ref, tab_ref[0, 0, WIN_HDR + w], y_buf, sl * half + tab_ref[0, 0, WIN_HDR + mw + w],
                      sem.at[sl], rt).start()
            return carry

        lax.fori_loop(0, tab_ref[0, 0, 0], issue, 0)

    @pl.when(i == 0)
    def _():
        fetch(win_ref, 0)

    @pl.when(i + 1 < pl.num_programs(0))
    def _():
        fetch(winn_ref, 1 - slot)

    _to_tiles(x_buf, 0, _x_tile(xm_ref, xt_ref, ntm))

    def drain(w, carry):
        _seg_copy(yb_ref, 0, y_buf, 0, sem.at[slot], rt).wait()
        return carry

    lax.fori_loop(0, win_ref[0, 0, 0], drain, 0)
    ybase = slot * half

    def comb(t, carry):
        acc = _tok(x_buf, t, rt)[...]
        for k in range(TOP_K):
            a = k * tm + t
            acc = acc + gate_ref[0, 0, a] * _tok(y_buf, ybase + lpos_ref[0, 0, a], rt)[...]
        _tok(x_buf, t, rt)[...] = acc
        return carry

    lax.fori_loop(0, tm, comb, 0, unroll=8)
    out = _from_tiles(x_buf, 0, tm, rt)
    if final_norm:
        out = _rms(out, fw_ref[...])

    @pl.when(i < ntm)
    def _():
        om_ref[...] = out

    @pl.when(i == ntm)
    def _():
        ot_ref[...] = out


def _combine(xm, xt, lpos, gates, win, yb, fw, *, tm, final_norm):
    d = xm.shape[1]
    ntm = xm.shape[0] // tm
    nt = ntm + 1
    rt = d // LANES

    def smem(arr, imap):
        return pl.BlockSpec((1, 1, arr.shape[2]), imap, memory_space=pltpu.SMEM)

    return pl.pallas_call(
        functools.partial(_combine_body, ntm=ntm, final_norm=final_norm),
        grid=(nt,),
        in_specs=[
            smem(lpos, lambda i: (i, 0, 0)),
            smem(gates, lambda i: (i, 0, 0)),
            smem(win, lambda i: (i, 0, 0)),
            smem(win, lambda i: (jnp.minimum(i + 1, nt - 1), 0, 0)),
        ] + _x_specs(xm, tm) + [
            _full(fw.shape),
            pl.BlockSpec(memory_space=pl.ANY),
        ],
        out_specs=_x_specs(xm, tm),
        out_shape=[jax.ShapeDtypeStruct(xm.shape, F32), jax.ShapeDtypeStruct(xt.shape, F32)],
        scratch_shapes=[pltpu.VMEM((2 * _ybuf_tokens(tm) * rt, LANES), F32),
                        pltpu.VMEM((tm * rt, LANES), F32),
                        pltpu.SemaphoreType.DMA((2,))],
        compiler_params=_cparams(("arbitrary",)),
        name="moe_combine",
    )(lpos, gates, win, win, xm, xt, fw, yb)


def _moe(xm, xt, nw, w_group, b_group, w_expert, b_expert, wg, wu, wd, fw, *, layer, final_norm):
    tm, d = xt.shape
    nt = xm.shape[0] // tm + 1
    t = nt * tm
    pad = LANES - N_EXPERTS - N_EXPERT_GROUPS
    wr = jnp.pad(jnp.concatenate([w_expert, w_group], axis=1), ((0, 0), (0, pad)))
    wr_hi = wr.astype(BF16)
    wr = jnp.concatenate([wr_hi, (wr - wr_hi.astype(F32)).astype(BF16)], axis=1)
    br = jnp.pad(jnp.concatenate([b_expert, b_group]), (0, pad)).reshape(1, LANES)
    nw2 = nw.reshape(1, d)
    info, cnt = _route(xm, xt, nw2, wr, br, tm=tm)

    def per_tile(rows):
        return rows.reshape(TOP_K, nt, tm).transpose(1, 0, 2).reshape(nt, 1, TOP_K * tm)

    gates = per_tile(info[0:TOP_K])
    lpos = per_tile(info[4:4 + TOP_K].astype(I32))
    lpos_al = per_tile(info[6:6 + TOP_K].astype(I32))
    cnt = cnt[:, 0, :N_EXPERTS].astype(I32)
    total = jnp.sum(cnt, axis=0)
    padded = jnp.where(total > 0, (total + SEG_W + MOE_BLOCK - 2) // MOE_BLOCK * MOE_BLOCK, 0)
    pend = jnp.cumsum(padded)
    pstart = pend - padded
    gstart = pstart[None, :] + jnp.cumsum(cnt, axis=0) - cnt
    lstart = jnp.cumsum(cnt, axis=1) - cnt
    nwin_e = (cnt + SEG_W - 1) // SEG_W
    lstart_al = (jnp.cumsum(nwin_e, axis=1) - nwin_e) * SEG_W

    winc = jnp.cumsum(nwin_e, axis=1)
    nwin = winc[:, -1:]
    mw = _max_windows(tm)
    j = jnp.arange(mw, dtype=I32)[None, :]
    owner = (jnp.sum(winc[:, None, :] <= j[:, :, None], axis=-1)[:, :, None]
             == jnp.arange(N_EXPERTS, dtype=I32)[None, None, :])
    pick = lambda tab: jnp.sum(jnp.where(owner, tab[:, None, :], 0), axis=-1)
    w_off = (j - pick(winc - nwin_e)) * SEG_W
    live = j < nwin
    src_loc = jnp.where(live, pick(lstart) + w_off, 0)
    slot_g = jnp.where(live, pick(gstart) + w_off, 0)
    dst_loc = jnp.where(live, pick(lstart_al) + w_off, 0)
    nprev = jnp.concatenate([jnp.zeros((1, 1), I32), nwin[:-1]], axis=0)
    win_d = jnp.concatenate([nwin, nprev, src_loc, slot_g], axis=1).reshape(nt, 1, WIN_HDR + 2 * mw)
    win_c = jnp.concatenate([nwin, nprev, slot_g, dst_loc], axis=1).reshape(nt, 1, WIN_HDR + 2 * mw)
    n_blocks = -(-(t * TOP_K + N_EXPERTS * (MOE_BLOCK + SEG_W - 2)) // MOE_BLOCK)
    n_slots = n_blocks * MOE_BLOCK
    zfirst = jnp.concatenate([pstart + total // SEG_W * SEG_W, pend[-1:]])
    zend = jnp.concatenate([pend, jnp.full((1,), n_slots, I32)])
    nz_e = (zend - zfirst) // SEG_W
    zinc = jnp.cumsum(nz_e)
    mz = N_EXPERTS * ((MOE_BLOCK + 2 * SEG_W) // SEG_W) + n_slots // SEG_W - TOP_K * t // SEG_W
    jz = jnp.arange(mz, dtype=I32)
    zowner = (jnp.sum(zinc[None, :] <= jz[:, None], axis=-1)[:, None]
              == jnp.arange(N_EXPERTS + 1, dtype=I32)[None, :])
    zpick = lambda tab: jnp.sum(jnp.where(zowner, tab[None, :], 0), axis=-1)
    zslot = jnp.where(jz < zinc[-1], zpick(zfirst) + (jz - zpick(zinc - nz_e)) * SEG_W, 0)
    zwin = jnp.concatenate([zinc[-1:], zslot]).astype(I32)
    blk_exp = jnp.minimum(jnp.sum(pend[None, :] <= (jnp.arange(n_blocks, dtype=I32) * MOE_BLOCK)[:, None], axis=1),
                          N_EXPERTS - 1).astype(I32)
    n_used = (pend[-1] // MOE_BLOCK).astype(I32).reshape(1)

    xb = _dispatch(xm, xt, nw2, lpos, win_d, zwin, tm=tm, n_slots=n_slots)
    yb = _experts(blk_exp, n_used, xb, wg, wu, wd, layer=layer)
    return _combine(xm, xt, lpos_al, gates, win_c, yb, fw.reshape(1, d), tm=tm, final_norm=final_norm)


def _linear_body(x_ref, nw_ref, w_ref, b_ref, r_ref, out_ref, *, norm):
    x = x_ref[...]
    if norm:
        x = _rms(x, nw_ref[...])
    out_ref[...] = _bdot(x, w_ref[...]) + b_ref[...] + r_ref[...]


def _linear(x, w, *, nw=None, bias=None, res=None, tn):
    m, kd = x.shape
    n = w.shape[1]
    norm = nw is not None
    nw = jnp.ones((1, kd), F32) if nw is None else nw
    bias = jnp.zeros((1, n), F32) if bias is None else bias
    res = jnp.zeros((m, n), F32) if res is None else res
    tn = min(tn, n)
    return pl.pallas_call(
        functools.partial(_linear_body, norm=norm),
        grid=(n // tn,),
        in_specs=[
            _full(x.shape), _full(nw.shape),
            pl.BlockSpec((kd, tn), lambda j: (0, j)),
            pl.BlockSpec((1, tn), lambda j: (0, j)),
            pl.BlockSpec((m, tn), lambda j: (0, j)),
        ],
        out_specs=pl.BlockSpec((m, tn), lambda j: (0, j)),
        out_shape=jax.ShapeDtypeStruct((m, n), F32),
        compiler_params=_cparams(("arbitrary",)),
        name="sample_linear",
    )(x, nw, w, bias, res)


def _sample_conv_body(xbc_ref, st_ref, cw_ref, cb_ref, dtr_ref, dtb_ref, xc_ref, stn_ref, dt_ref, cbg_ref,
                      *, d_inner):
    nst = D_STATE
    xbc = xbc_ref[...]
    acc = cb_ref[...] + cw_ref[3:4, :] * xbc
    for k in range(CONV_W - 1):
        acc = acc + cw_ref[k:k + 1, :] * st_ref[k]
    xc = _silu(acc)
    xc_ref[...] = xc
    stn_ref[0] = st_ref[1]
    stn_ref[1] = st_ref[2]
    stn_ref[2] = xbc
    dt_ref[...] = _softplus(dtr_ref[...] + dtb_ref[...])
    lane = lax.broadcasted_iota(I32, (xbc.shape[0], LANES), 1)
    cbg = jnp.zeros((xbc.shape[0], LANES), F32)
    for g in range(N_BC_GROUPS):
        b_g = xc[:, d_inner + g * nst:d_inner + (g + 1) * nst]
        c_g = xc[:, d_inner + (N_BC_GROUPS + g) * nst:d_inner + (N_BC_GROUPS + g + 1) * nst]
        cbg = jnp.where(lane == g, jnp.sum(b_g * c_g, axis=-1, keepdims=True), cbg)
    cbg_ref[...] = cbg


def _sample_conv(xbc, st_t, cw, cb, dtr, dtb, *, d_inner):
    m, cd = xbc.shape
    return pl.pallas_call(
        functools.partial(_sample_conv_body, d_inner=d_inner),
        grid=(1,),
        in_specs=[_full(xbc.shape), _full(st_t.shape), _full(cw.shape), _full(cb.shape), _full(dtr.shape),
                  _full(dtb.shape)],
        out_specs=[_full((m, cd)), _full(st_t.shape), _full((m, LANES)), _full((m, LANES))],
        out_shape=[jax.ShapeDtypeStruct((m, cd), F32), jax.ShapeDtypeStruct(st_t.shape, F32),
                   jax.ShapeDtypeStruct((m, LANES), F32), jax.ShapeDtypeStruct((m, LANES), F32)],
        compiler_params=_cparams(("arbitrary",)),
        name="sample_conv",
    )(xbc, st_t, cw, cb, dtr, dtb)


SSD_REQS_PER_STEP = 4


def _sample_ssd_body(s0_ref, xt_ref, bc_ref, hs_ref, par_ref, sn_ref, yt_ref, *, n_heads):
    hp = SSM_HEAD_DIM
    hpg = n_heads // N_BC_GROUPS
    a = -jnp.exp(par_ref[0:1, :])
    dsk = par_ref[1:2, :]
    head_row = lax.broadcasted_iota(I32, (n_heads, s0_ref.shape[3]), 0)
    for r in range(s0_ref.shape[0]):
        xt = xt_ref[r]
        dt = hs_ref[r, 0:1, :]
        cbh = hs_ref[r, 1:2, :]
        dec = jnp.exp(dt * a)
        xdt = xt * dt
        xdt_b = xdt.astype(BF16)
        yoff = jnp.zeros((hp, n_heads), F32)
        for hh in range(n_heads):
            g = hh // hpg
            b_row = bc_ref[r, g:g + 1, :]
            c_row = bc_ref[r, N_BC_GROUPS + g:N_BC_GROUPS + g + 1, :]
            s0 = s0_ref[r, hh]
            yoff = yoff + _bdot_nt(s0, jnp.where(head_row == hh, c_row, 0.0))
            b_sel = jnp.where(head_row == hh, b_row, 0.0).astype(BF16)
            sn_ref[r, hh] = s0 * dec[:, hh:hh + 1] + jnp.dot(xdt_b, b_sel, preferred_element_type=F32)
        yt_ref[r] = cbh * xdt + yoff * dec + xt * dsk


def _sample_ssd(s0, xt, bc, hs, par):
    bsz, n_heads, hp, nst = s0.shape
    rb = SSD_REQS_PER_STEP if bsz % SSD_REQS_PER_STEP == 0 else 1
    return pl.pallas_call(
        functools.partial(_sample_ssd_body, n_heads=n_heads),
        grid=(bsz // rb,),
        in_specs=[
            pl.BlockSpec((rb, n_heads, hp, nst), lambda b: (b, 0, 0, 0)),
            pl.BlockSpec((rb, hp, n_heads), lambda b: (b, 0, 0)),
            pl.BlockSpec((rb,) + bc.shape[1:], lambda b: (b, 0, 0)),
            pl.BlockSpec((rb,) + hs.shape[1:], lambda b: (b, 0, 0)),
            _full(par.shape),
        ],
        out_specs=[
            pl.BlockSpec((rb, n_heads, hp, nst), lambda b: (b, 0, 0, 0)),
            pl.BlockSpec((rb, hp, n_heads), lambda b: (b, 0, 0)),
        ],
        out_shape=[jax.ShapeDtypeStruct(s0.shape, F32), jax.ShapeDtypeStruct((bsz, hp, n_heads), F32)],
        compiler_params=_cparams(("arbitrary",)),
        name="sample_ssd",
    )(s0, xt, bc, hs, par)


def _sample_gnorm_out_body(y_ref, z_ref, gnw_ref, wout_ref, x_ref, out_ref, *, d_inner):
    gw = d_inner // N_BC_GROUPS
    acc = x_ref[...]
    for g in range(N_BC_GROUPS):
        glanes = slice(g * gw, (g + 1) * gw)
        gg = y_ref[:, glanes] * _silu(z_ref[:, glanes])
        gg = gg * lax.rsqrt(jnp.mean(gg * gg, axis=-1, keepdims=True) + EPS) * gnw_ref[:, glanes]
        acc = acc + jnp.dot(gg.astype(BF16), wout_ref[glanes, :], preferred_element_type=F32)
    out_ref[...] = acc


def _sample_gnorm_out(y, z, gnw, wout, x):
    d_inner = y.shape[1]
    return pl.pallas_call(
        functools.partial(_sample_gnorm_out_body, d_inner=d_inner),
        grid=(1,),
        in_specs=[_full(y.shape), _full(z.shape), _full(gnw.shape), _full(wout.shape), _full(x.shape)],
        out_specs=_full(x.shape),
        out_shape=jax.ShapeDtypeStruct(x.shape, F32),
        compiler_params=_cparams(("arbitrary",)),
        name="sample_gnorm_out",
    )(y, z, gnw, wout, x)


def _sample_attn_body(q_ref, kn_ref, vn_ref, kc_ref, vc_ref, sink_ref, o_ref, ko_ref, vo_ref, s_buf, sn_buf):
    bt = q_ref.shape[0]
    wb = kc_ref.shape[1]
    hd = HEAD_DIM
    nh = N_Q_HEADS
    qpk = N_Q_HEADS // N_KV_HEADS
    scale = hd ** -0.5
    for b in range(bt):
        kn = kn_ref[b]
        kn_h = jnp.concatenate([jnp.broadcast_to(kn[:, g * hd:(g + 1) * hd], (qpk, hd))
                                for g in range(N_KV_HEADS)], axis=0)
        sn_buf[b * nh:(b + 1) * nh, :] = jnp.sum(q_ref[b] * kn_h, axis=-1, keepdims=True) * scale
        for g in range(N_KV_HEADS):
            rows = slice(b * nh + g * qpk, b * nh + (g + 1) * qpk)
            s_buf[rows, :] = _bdot_nt(q_ref[b, g * qpk:(g + 1) * qpk, :], kc_ref[b, :, g * hd:(g + 1) * hd]) * scale
    s = s_buf[...]
    s_new = sn_buf[...]
    sink = sink_ref[...]
    m = jnp.maximum(jnp.maximum(jnp.max(s, axis=-1, keepdims=True), s_new), sink)
    p = jnp.exp(s - m)
    p_new = jnp.exp(s_new - m)
    inv = 1.0 / (jnp.sum(p, axis=-1, keepdims=True) + p_new + jnp.exp(sink - m))
    s_buf[...] = p * inv
    sn_buf[...] = p_new * inv
    for b in range(bt):
        vn = vn_ref[b]
        for g in range(N_KV_HEADS):
            rows = slice(b * nh + g * qpk, b * nh + (g + 1) * qpk)
            cols = slice(g * hd, (g + 1) * hd)
            o_ref[b, g * qpk:(g + 1) * qpk, :] = (_bdot(s_buf[rows, :], vc_ref[b, :, cols])
                                                   + sn_buf[rows, :] * vn[:, cols])
        ko_ref[b, 0:wb - 1, :] = kc_ref[b, 1:wb, :]
        ko_ref[b, wb - 1:wb, :] = kn_ref[b]
        vo_ref[b, 0:wb - 1, :] = vc_ref[b, 1:wb, :]
        vo_ref[b, wb - 1:wb, :] = vn


def _sample_attn(q3, kn, vn, kc, vc, sinks, *, bt):
    bsz, nqh, hd = q3.shape
    wb, nk = kc.shape[1], kc.shape[2]
    return pl.pallas_call(
        _sample_attn_body,
        grid=(bsz // bt,),
        in_specs=[
            pl.BlockSpec((bt, nqh, hd), lambda i: (i, 0, 0)),
            pl.BlockSpec((bt, 1, nk), lambda i: (i, 0, 0)),
            pl.BlockSpec((bt, 1, nk), lambda i: (i, 0, 0)),
            pl.BlockSpec((bt, wb, nk), lambda i: (i, 0, 0)),
            pl.BlockSpec((bt, wb, nk), lambda i: (i, 0, 0)),
            _full(sinks.shape),
        ],
        out_specs=[
            pl.BlockSpec((bt, nqh, hd), lambda i: (i, 0, 0)),
            pl.BlockSpec((bt, wb, nk), lambda i: (i, 0, 0)),
            pl.BlockSpec((bt, wb, nk), lambda i: (i, 0, 0)),
        ],
        out_shape=[jax.ShapeDtypeStruct(q3.shape, F32), jax.ShapeDtypeStruct(kc.shape, F32),
                   jax.ShapeDtypeStruct(vc.shape, F32)],
        scratch_shapes=[pltpu.VMEM((bt * nqh, wb), F32), pltpu.VMEM((bt * nqh, 1), F32)],
        compiler_params=_cparams(("arbitrary",)),
        name="sample_attn",
    )(q3, kn, vn, kc, vc, sinks)


def _mamba_sample(x, nw, mw, state_conv, state_ssm):
    win, cw, cb, dtb, alog, dsk, gnw, wout = mw
    bsz, d = x.shape
    d_inner = wout.shape[0]
    conv_dim = cw.shape[1]
    n_heads = d_inner // SSM_HEAD_DIM
    hp = SSM_HEAD_DIM
    proj = _linear(x, win, nw=nw, tn=896)
    z = proj[:, :d_inner]
    xbc = proj[:, d_inner:d_inner + conv_dim]
    dtr = proj[:, d_inner + conv_dim:]
    st_t = jnp.transpose(state_conv, (1, 0, 2))
    xc, stn_t, dt, cbg = _sample_conv(xbc, st_t, cw, cb, dtr, dtb, d_inner=d_inner)
    conv_new = jnp.transpose(stn_t, (1, 0, 2))
    xt = jnp.transpose(xc[:, :d_inner].reshape(bsz, n_heads, hp), (0, 2, 1))
    bc = xc[:, d_inner:].reshape(bsz, 2 * N_BC_GROUPS, D_STATE)
    cbh = jnp.repeat(cbg[:, :N_BC_GROUPS], n_heads // N_BC_GROUPS, axis=1)
    hs = jnp.stack([dt[:, :n_heads], cbh], axis=1)
    par = jnp.stack([alog[0, :n_heads], dsk.reshape(n_heads, hp)[:, 0]], axis=0)
    ssm_new, yt = _sample_ssd(state_ssm, xt, bc, hs, par)
    y = jnp.transpose(yt, (0, 2, 1)).reshape(bsz, d_inner)
    out = _sample_gnorm_out(y, z, gnw, wout, x)
    return out, conv_new, ssm_new


def _attn_sample(x, nw, wqkv, bqkv, sinks, wo, bo, cache_k, cache_v):
    bsz, d = x.shape
    wb = cache_k.shape[1]
    nq = N_Q_HEADS * HEAD_DIM
    nk = N_KV_HEADS * HEAD_DIM
    qkv = _linear(x, wqkv, nw=nw, bias=bqkv, tn=512)
    q3 = qkv[:, :nq].reshape(bsz, N_Q_HEADS, HEAD_DIM)
    kn = qkv[:, nq:nq + nk].reshape(bsz, 1, nk)
    vn = qkv[:, nq + nk:].reshape(bsz, 1, nk)
    o3, ko, vo = _sample_attn(q3, kn, vn, cache_k.reshape(bsz, wb, nk), cache_v.reshape(bsz, wb, nk),
                              jnp.tile(sinks.reshape(N_Q_HEADS, 1), (8, 1)), bt=8)
    out = _linear(o3.reshape(bsz, nq), wo, bias=bo, res=x, tn=512)
    return out, ko.reshape(cache_k.shape), vo.reshape(cache_v.shape)


def kernel(x_prompt, x_sample, state_ssm, state_conv, cache_k_win, cache_v_win,
           mamba_w_in, mamba_conv_w, mamba_conv_b, mamba_dt_bias, mamba_a_log, mamba_d,
           mamba_norm_w, mamba_w_out, attn_w_qkv, attn_b_qkv, attn_sinks, attn_w_o, attn_b_o,
           norm_mix, norm_ffn, router_w_group, router_b_group, router_w_expert, router_b_expert,
           expert_w_gate, expert_w_up, expert_w_down, norm_final):
    bsz, seq, d = x_prompt.shape
    dbsz, dseq, _ = x_sample.shape
    assert dseq == 1 and cache_k_win.shape[2] <= WINDOW and seq % WINDOW == 0
    assert dbsz <= MOE_TILE and (bsz * seq) % MOE_TILE == 0
    depth = norm_mix.shape[0]
    xp = x_prompt.reshape(bsz * seq, d)
    xs = x_sample.reshape(dbsz, d)
    ssm_p, conv_p, kp_l, vp_l = [], [], [], []
    ssm_s, conv_s, ks_l, vs_l = [], [], [], []
    for i in range(depth):
        j = i // 2
        nw = norm_mix[i].reshape(1, d)
        if i % 2 == 0:
            mw = _mamba_weights(mamba_w_in[j], mamba_conv_w[j], mamba_conv_b[j], mamba_dt_bias[j],
                                mamba_a_log[j], mamba_d[j], mamba_norm_w[j], mamba_w_out[j])
            xp, cp, sp = _mamba_prompt(xp, nw, *mw, bsz=bsz, seq=seq, ts=2 * SSD_CHUNK)
            xs, cs_, ss_ = _mamba_sample(xs, nw, mw, state_conv[j], state_ssm[j])
            ssm_p.append(sp)
            conv_p.append(cp)
            ssm_s.append(ss_)
            conv_s.append(cs_)
        else:
            wqkv = attn_w_qkv[j].astype(BF16)
            bqkv = attn_b_qkv[j].reshape(1, -1)
            wo = attn_w_o[j].astype(BF16)
            bo = attn_b_o[j].reshape(1, d)
            xp, kp, vp = _attn_prompt(xp, attn_sinks[j], nw, wqkv, bqkv, wo, bo, bsz=bsz, seq=seq,
                                      tq=WINDOW)
            xs, ks_, vs_ = _attn_sample(xs, nw, wqkv, bqkv, attn_sinks[j], wo, bo, cache_k_win[j], cache_v_win[j])
            kp_l.append(kp.reshape(bsz, WINDOW, N_KV_HEADS, HEAD_DIM))
            vp_l.append(vp.reshape(bsz, WINDOW, N_KV_HEADS, HEAD_DIM))
            ks_l.append(ks_)
            vs_l.append(vs_)
        last = i == depth - 1
        moe_w = (norm_ffn[i], router_w_group[i], router_b_group[i], router_w_expert[i], router_b_expert[i],
                 expert_w_gate, expert_w_up, expert_w_down, norm_final)
        xs_tile = jnp.pad(xs, ((0, MOE_TILE - dbsz), (0, 0)))
        xp, xs_tile = _moe(xp, xs_tile, *moe_w, layer=i, final_norm=last)
        xs = xs_tile[:dbsz]
    return (xp.reshape(bsz, seq, d), xs.reshape(dbsz, dseq, d),
            jnp.stack(ssm_p), jnp.stack(conv_p), jnp.stack(kp_l), jnp.stack(vp_l),
            jnp.stack(ssm_s), jnp.stack(conv_s), jnp.stack(ks_l), jnp.stack(vs_l))
```

```python
import functools
import math

import jax
import jax.numpy as jnp
from jax import lax
from jax.experimental import pallas as pl
from jax.experimental.pallas import tpu as pltpu

F32 = jnp.float32
BF16 = jnp.bfloat16
I32 = jnp.int32

EPS = 1e-5
LANES = 128
VMEM_LIMIT = 56 * 1024 * 1024

SSM_HEAD_DIM = 64
D_STATE = 128
N_BC_GROUPS = 8
CONV_W = 4
SSD_CHUNK = 128
N_Q_HEADS = 16
N_KV_HEADS = 4
HEAD_DIM = 64
WINDOW = 128
N_EXPERT_GROUPS = 4
EXPERTS_PER_GROUP = 8
N_EXPERTS = N_EXPERT_GROUPS * EXPERTS_PER_GROUP
TOP_K = 2
MOE_BLOCK = 256
MOE_TILE = 512


def _cparams(sem):
    return pltpu.CompilerParams(dimension_semantics=sem, vmem_limit_bytes=VMEM_LIMIT)


def _full(shape):
    n = len(shape)
    return pl.BlockSpec(shape, lambda *_: (0,) * n)


def _resident(shape):
    n = len(shape)
    return pl.BlockSpec(shape, lambda *_: (0,) * n, pipeline_mode=pl.Buffered(1))


def _rms(x, w):
    return x * lax.rsqrt(jnp.mean(x * x, axis=-1, keepdims=True) + EPS) * w


def _silu(x):
    return x / (1.0 + jnp.exp(-x))


def _softplus(x):
    return jnp.maximum(x, 0.0) + jnp.log(1.0 + jnp.exp(-jnp.abs(x)))


def _bdot(a, b):
    return jnp.dot(a.astype(BF16), b.astype(BF16), preferred_element_type=F32)


def _bdot_nt(a, b):
    return lax.dot_general(a.astype(BF16), b.astype(BF16), (((1,), (1,)), ((), ())),
                           preferred_element_type=F32)


def _bdot_tn(a, b):
    return lax.dot_general(a.astype(BF16), b.astype(BF16), (((0,), (0,)), ((), ())),
                           preferred_element_type=F32)


def _fdot(a, b):
    return jnp.dot(a, b, preferred_element_type=F32, precision=lax.Precision.HIGHEST)


def _spread(v, onehot_ref):
    hi = v.astype(BF16)
    lo = (v - hi.astype(F32)).astype(BF16)
    return jnp.dot(jnp.concatenate([hi, lo], axis=1), onehot_ref[...], preferred_element_type=F32)


def _mamba_prompt_body(x_ref, nw_ref, win_ref, cw_ref, cb_ref, dtb_ref, alog_ref, dsk_ref, gnw_ref,
                       wout_ref, hexp_ref, out_ref, conv_ref, ssm_ref, h_buf, xbc_buf, xc_buf, st_buf, y_buf,
                       wexp_buf, eexp_buf, *, d_inner, n_heads):
    ts = x_ref.shape[0]
    cs = SSD_CHUNK
    hp = SSM_HEAD_DIM
    nst = D_STATE
    gw = d_inner // N_BC_GROUPS
    hpg = n_heads // N_BC_GROUPS
    conv_dim = d_inner + 2 * N_BC_GROUPS * nst
    s = pl.program_id(1)

    @pl.when(s == 0)
    def _():
        xbc_buf[:, 0:8, :] = jnp.zeros((conv_dim // LANES, 8, LANES), F32)
        st_buf[...] = jnp.zeros_like(st_buf)

    h_buf[...] = _rms(x_ref[...], nw_ref[...]).astype(BF16)
    dtr = jnp.dot(h_buf[...], win_ref[:, d_inner + conv_dim:], preferred_element_type=F32)
    ct = 512
    spp = ct // LANES
    for j in range(conv_dim // ct):
        cols = slice(j * ct, (j + 1) * ct)
        piece = jnp.dot(h_buf[...], win_ref[:, d_inner + j * ct:d_inner + (j + 1) * ct],
                        preferred_element_type=F32)
        for q in range(spp):
            xbc_buf[j * spp + q, 8:8 + ts, :] = piece[:, q * LANES:(q + 1) * LANES]

        def back(k):
            return jnp.concatenate([xbc_buf[j * spp + q, pl.ds(8 - k, ts), :] for q in range(spp)], axis=1)

        acc = cb_ref[:, cols] + cw_ref[3:4, cols] * piece
        acc = acc + cw_ref[2:3, cols] * back(1)
        acc = acc + cw_ref[1:2, cols] * back(2)
        acc = acc + cw_ref[0:1, cols] * back(3)
        xc_buf[:, cols] = _silu(acc)
    for c in range(conv_dim // LANES):
        last3 = xbc_buf[c, 5 + ts:8 + ts, :]
        xbc_buf[c, 5:8, :] = last3
        conv_ref[0, :, c * LANES:(c + 1) * LANES] = last3

    dt = _softplus(dtr + dtb_ref[...])
    da = dt * (-jnp.exp(alog_ref[...]))
    row = lax.broadcasted_iota(I32, (cs, cs), 0)
    col = lax.broadcasted_iota(I32, (cs, cs), 1)
    causal = row >= col
    tril = causal.astype(F32)
    lane = lax.broadcasted_iota(I32, (cs, LANES), 1)
    lo_half = lane < hp

    for c in range(ts // cs):
        rows = slice(c * cs, (c + 1) * cs)
        da_c = da[rows]
        dt_c = dt[rows]
        acum = _fdot(tril, da_c)
        acum_t = acum.T
        dt_t = dt_c.T
        a_last = acum[cs - 1:cs, :]
        to_end = jnp.exp(a_last - acum)
        w_all = dt_c * to_end
        ea = jnp.exp(acum)
        cd = jnp.exp(a_last)
        both = _spread(jnp.concatenate([w_all, ea], axis=0), hexp_ref)
        wexp_buf[...] = both[0:cs]
        eexp_buf[...] = both[cs:2 * cs]
        for g in range(N_BC_GROUPS):
            glanes = slice(g * gw, (g + 1) * gw)
            b_g = xc_buf[rows, d_inner + g * nst:d_inner + (g + 1) * nst]
            c_g = xc_buf[rows, d_inner + (N_BC_GROUPS + g) * nst:d_inner + (N_BC_GROUPS + g + 1) * nst]
            cb = _bdot_nt(c_g, b_g)
            y_off = _bdot(c_g, st_buf[:, glanes])
            xw_parts = []
            for pr in range(hpg // 2):
                h0 = g * hpg + 2 * pr
                lanes0 = slice(h0 * hp, h0 * hp + 2 * hp)
                x_pair = xc_buf[rows, lanes0]
                ms = []
                for k in range(2):
                    hh = h0 + k
                    seg = acum[:, hh:hh + 1] - acum_t[hh:hh + 1, :]
                    dec = jnp.exp(jnp.where(causal, seg, -jnp.inf))
                    ms.append(cb * dec * dt_t[hh:hh + 1, :])
                x_ab = jnp.concatenate([jnp.where(lo_half, x_pair, 0.0), jnp.where(lo_half, 0.0, x_pair)], axis=0)
                y_pair = (eexp_buf[:, lanes0] * y_off[:, 2 * pr * hp:2 * (pr + 1) * hp]
                          + _bdot(jnp.concatenate(ms, axis=1), x_ab))
                y_buf[rows, lanes0] = y_pair + x_pair * dsk_ref[:, lanes0]
                xw_parts.append(x_pair * wexp_buf[:, lanes0])
            xw = jnp.concatenate(xw_parts, axis=1)
            cd_parts = [jnp.broadcast_to(cd[:, g * hpg + k:g * hpg + k + 1], (1, hp)) for k in range(hpg)]
            cd_g = jnp.concatenate(cd_parts, axis=1)
            st_buf[:, glanes] = st_buf[:, glanes] * cd_g + _bdot_tn(b_g, xw)

    @pl.when(s == pl.num_programs(1) - 1)
    def _():
        for pr in range(n_heads // 2):
            t = st_buf[:, 2 * pr * hp:2 * (pr + 1) * hp].T
            ssm_ref[0, 2 * pr] = t[0:hp]
            ssm_ref[0, 2 * pr + 1] = t[hp:2 * hp]

    acc = x_ref[...]
    for g in range(N_BC_GROUPS):
        glanes = slice(g * gw, (g + 1) * gw)
        z = jnp.dot(h_buf[...], win_ref[:, glanes], preferred_element_type=F32)
        gg = y_buf[:, glanes] * _silu(z)
        gg = gg * lax.rsqrt(jnp.mean(gg * gg, axis=-1, keepdims=True) + EPS) * gnw_ref[:, glanes]
        acc = acc + jnp.dot(gg.astype(BF16), wout_ref[glanes, :], preferred_element_type=F32)
    out_ref[...] = acc


def _mamba_prompt(x, nw, win, cw, cb, dtb, alog, dsk, gnw, wout, *, bsz, seq, ts):
    d = x.shape[1]
    d_inner = wout.shape[0]
    n_heads = d_inner // SSM_HEAD_DIM
    conv_dim = cw.shape[1]
    ns = seq // ts
    body = functools.partial(_mamba_prompt_body, d_inner=d_inner, n_heads=n_heads)
    hexp = (jnp.arange(2 * LANES, dtype=I32)[:, None] % LANES
            == jnp.arange(d_inner, dtype=I32)[None, :] // SSM_HEAD_DIM).astype(BF16)
    return pl.pallas_call(
        body,
        grid=(bsz, ns),
        in_specs=[
            pl.BlockSpec((ts, d), lambda b, s: (b * ns + s, 0)),
            _full(nw.shape), _resident(win.shape), _full(cw.shape), _full(cb.shape), _full(dtb.shape),
            _full(alog.shape), _full(dsk.shape), _full(gnw.shape), _resident(wout.shape), _resident(hexp.shape),
        ],
        out_specs=[
            pl.BlockSpec((ts, d), lambda b, s: (b * ns + s, 0)),
            pl.BlockSpec((1, CONV_W - 1, conv_dim), lambda b, s: (b, 0, 0)),
            pl.BlockSpec((1, n_heads, SSM_HEAD_DIM, D_STATE), lambda b, s: (b, 0, 0, 0)),
        ],
        out_shape=[
            jax.ShapeDtypeStruct((bsz * seq, d), F32),
            jax.ShapeDtypeStruct((bsz, CONV_W - 1, conv_dim), F32),
            jax.ShapeDtypeStruct((bsz, n_heads, SSM_HEAD_DIM, D_STATE), F32),
        ],
        scratch_shapes=[
            pltpu.VMEM((ts, d), BF16),
            pltpu.VMEM((conv_dim // LANES, 8 + ts, LANES), F32),
            pltpu.VMEM((ts, conv_dim), F32),
            pltpu.VMEM((D_STATE, d_inner), F32),
            pltpu.VMEM((ts, d_inner), F32),
            pltpu.VMEM((SSD_CHUNK, d_inner), F32),
            pltpu.VMEM((SSD_CHUNK, d_inner), F32),
        ],
        compiler_params=_cparams(("arbitrary", "arbitrary")),
        name="mamba_prompt",
    )(x, nw, win, cw, cb, dtb, alog, dsk, gnw, wout, hexp)


def _mamba_weights(w_in, conv_w, conv_b, dt_bias, a_log, d_skip, norm_w, w_out):
    d_inner = w_out.shape[0]
    n_heads = dt_bias.shape[0]
    pad = LANES - n_heads
    win = jnp.pad(w_in, ((0, 0), (0, pad))).astype(BF16)
    dtb = jnp.pad(dt_bias, (0, pad)).reshape(1, LANES)
    alog = jnp.pad(a_log, (0, pad)).reshape(1, LANES)
    dsk = jnp.repeat(d_skip, SSM_HEAD_DIM).reshape(1, d_inner)
    return (win, conv_w, conv_b.reshape(1, -1), dtb, alog, dsk, norm_w.reshape(1, d_inner),
            w_out.astype(BF16))


def _sink_softmax_pv(s, sink, v):
    m = jnp.maximum(jnp.max(s, axis=-1, keepdims=True), sink)
    p = jnp.exp(s - m)
    denom = jnp.sum(p, axis=-1, keepdims=True) + jnp.exp(sink - m)
    return _bdot(p, v) / denom


def _attn_prompt_body(sink_ref, x_ref, nw_ref, wqkv_ref, bqkv_ref, wo_ref, bo_ref,
                      out_ref, kwin_ref, vwin_ref, kv_buf, q_buf, o_buf):
    blk = WINDOW
    hd = HEAD_DIM
    nq = N_Q_HEADS * hd
    nk = N_KV_HEADS * hd
    qpk = N_Q_HEADS // N_KV_HEADS
    tq = x_ref.shape[0]
    s_id = pl.program_id(1)

    @pl.when(s_id == 0)
    def _():
        kv_buf[0:blk, :] = jnp.zeros((blk, 2 * nk), F32)

    h = _rms(x_ref[...], nw_ref[...]).astype(BF16)
    q_buf[...] = jnp.dot(h, wqkv_ref[:, 0:nq], preferred_element_type=F32) + bqkv_ref[:, 0:nq]
    kv_buf[blk:blk + tq, :] = jnp.dot(h, wqkv_ref[:, nq:], preferred_element_type=F32) + bqkv_ref[:, nq:]
    kwin_ref[0] = kv_buf[tq:tq + blk, 0:nk]
    vwin_ref[0] = kv_buf[tq:tq + blk, nk:]

    row = lax.broadcasted_iota(I32, (blk, 2 * blk), 0)
    col = lax.broadcasted_iota(I32, (blk, 2 * blk), 1)
    diff = row + blk - col
    band = (diff >= 0) & (diff <= WINDOW)
    scale = hd ** -0.5
    for qb in range(tq // blk):
        qrows = slice(qb * blk, (qb + 1) * blk)
        krows = slice(qb * blk, (qb + 2) * blk)
        ok = band & ((col >= blk) | (s_id > 0)) if qb == 0 else band
        for g in range(N_KV_HEADS):
            k_g = kv_buf[krows, g * hd:(g + 1) * hd]
            v_g = kv_buf[krows, nk + g * hd:nk + (g + 1) * hd]
            for j in range(qpk):
                hh = g * qpk + j
                s = _bdot_nt(q_buf[qrows, hh * hd:(hh + 1) * hd], k_g) * scale
                s = jnp.where(ok, s, -jnp.inf)
                o_buf[qrows, hh * hd:(hh + 1) * hd] = _sink_softmax_pv(s, sink_ref[hh], v_g)
    kv_buf[0:blk, :] = kv_buf[tq:tq + blk, :]
    out_ref[...] = (x_ref[...] + jnp.dot(o_buf[...].astype(BF16), wo_ref[...], preferred_element_type=F32)
                    + bo_ref[...])


def _attn_prompt(x, sinks, nw, wqkv, bqkv, wo, bo, *, bsz, seq, tq):
    d = x.shape[1]
    blk = WINDOW
    nb = seq // tq
    nk = N_KV_HEADS * HEAD_DIM
    nq = N_Q_HEADS * HEAD_DIM
    return pl.pallas_call(
        _attn_prompt_body,
        grid=(bsz, nb),
        in_specs=[
            pl.BlockSpec(memory_space=pltpu.SMEM),
            pl.BlockSpec((tq, d), lambda b, s: (b * nb + s, 0)),
            _full(nw.shape), _full(wqkv.shape), _full(bqkv.shape), _full(wo.shape), _full(bo.shape),
        ],
        out_specs=[
            pl.BlockSpec((tq, d), lambda b, s: (b * nb + s, 0)),
            pl.BlockSpec((1, blk, nk), lambda b, s: (b, 0, 0)),
            pl.BlockSpec((1, blk, nk), lambda b, s: (b, 0, 0)),
        ],
        out_shape=[
            jax.ShapeDtypeStruct((bsz * seq, d), F32),
            jax.ShapeDtypeStruct((bsz, blk, nk), F32),
            jax.ShapeDtypeStruct((bsz, blk, nk), F32),
        ],
        scratch_shapes=[
            pltpu.VMEM((blk + tq, 2 * nk), F32),
            pltpu.VMEM((tq, nq), F32),
            pltpu.VMEM((tq, nq), F32),
        ],
        compiler_params=_cparams(("arbitrary", "arbitrary")),
        name="attn_prompt",
    )(sinks, x, nw, wqkv, bqkv, wo, bo)


def _x_specs(xm, tm):
    ntm, d = xm.shape[0] // tm, xm.shape[1]
    return [pl.BlockSpec((tm, d), lambda i: (jnp.minimum(i, ntm - 1), 0)), pl.BlockSpec((tm, d), lambda i: (0, 0))]


def _x_tile(xm_ref, xt_ref, ntm):
    return jnp.where(pl.program_id(0) < ntm, xm_ref[...], xt_ref[...])


def _route_body(xm_ref, xt_ref, nw_ref, wr_ref, br_ref, info_ref, cnt_ref, *, ntm):
    tm = xm_ref.shape[0]
    h = _rms(_x_tile(xm_ref, xt_ref, ntm), nw_ref[...])
    h_hi = h.astype(BF16)
    h_lo = (h - h_hi.astype(F32)).astype(BF16)
    part = jnp.dot(h_hi, wr_ref[...], preferred_element_type=F32)
    logits = (part[:, 0:LANES] + part[:, LANES:] + jnp.dot(h_lo, wr_ref[:, 0:LANES], preferred_element_type=F32)
              + br_ref[...])
    lane_i = lax.broadcasted_iota(I32, (tm, LANES), 1)
    lane = lane_i.astype(F32)
    lane_grp = (lane_i // EXPERTS_PER_GROUP).astype(F32)
    big = float(LANES)
    ninf = -jnp.inf

    def first_argmax(v):
        m = jnp.max(v, axis=-1, keepdims=True)
        return m, jnp.min(jnp.where(v == m, lane, big), axis=-1, keepdims=True)

    gmask = (lane_i >= N_EXPERTS) & (lane_i < N_EXPERTS + N_EXPERT_GROUPS)
    gl = jnp.where(gmask, logits, ninf)
    gmax, gi = first_argmax(gl)
    gi = gi - float(N_EXPERTS)
    pg = 1.0 / jnp.sum(jnp.exp(gl - gmax), axis=-1, keepdims=True)
    emask = (lane_i < N_EXPERTS) & (lane_grp == gi)
    el = jnp.where(emask, logits, ninf)
    m1, i1 = first_argmax(el)
    el2 = jnp.where(lane == i1, ninf, el)
    m2, i2 = first_argmax(el2)
    den = jnp.sum(jnp.exp(el - m1), axis=-1, keepdims=True)
    tp1 = 1.0 / den
    tp2 = jnp.exp(m2 - m1) / den
    g1 = pg * tp1 / (tp1 + tp2)
    g2 = pg * tp2 / (tp1 + tp2)
    hot1 = lane == i1
    hot2 = lane == i2
    onehot = jnp.where(hot1 | hot2, 1.0, 0.0)
    rr = lax.broadcasted_iota(I32, (tm, tm), 0)
    cc = lax.broadcasted_iota(I32, (tm, tm), 1)
    before = jnp.where(rr > cc, 1.0, 0.0)
    cum = _bdot(before, onehot)
    r1 = jnp.sum(jnp.where(hot1, cum, 0.0), axis=-1, keepdims=True)
    r2 = jnp.sum(jnp.where(hot2, cum, 0.0), axis=-1, keepdims=True)
    cnt_row = jnp.sum(onehot, axis=0, keepdims=True)
    nwin_row = jnp.floor((cnt_row + (SEG_W - 1.0)) * (1.0 / SEG_W))
    er = lax.broadcasted_iota(I32, (LANES, LANES), 0)
    ec = lax.broadcasted_iota(I32, (LANES, LANES), 1)
    earlier = jnp.where(er < ec, 1.0, 0.0)
    both = jnp.concatenate([jnp.broadcast_to(cnt_row, (8, LANES)), jnp.broadcast_to(nwin_row, (8, LANES))], axis=0)
    pre = _fdot(both, earlier)
    start = pre[0:1, :]
    start_al = pre[8:9, :] * float(SEG_W)

    def at(hot, row):
        return jnp.sum(jnp.where(hot, row, 0.0), axis=-1, keepdims=True)

    info = jnp.zeros((tm, LANES), F32)
    for k, v in enumerate((g1, g2, i1, i2, at(hot1, start) + r1, at(hot2, start) + r2,
                           at(hot1, start_al) + r1, at(hot2, start_al) + r2)):
        info = jnp.where(lane_i == k, v, info)
    info_ref[...] = info.T[0:8, :]
    cnt_ref[0] = jnp.broadcast_to(cnt_row, (8, LANES))


ROUTE_ROWS = 8


def _route(xm, xt, nw, wr, br, *, tm):
    ntm = xm.shape[0] // tm
    nt = ntm + 1
    return pl.pallas_call(
        functools.partial(_route_body, ntm=ntm),
        grid=(nt,),
        in_specs=_x_specs(xm, tm) + [_full(nw.shape), _full(wr.shape), _full(br.shape)],
        out_specs=[pl.BlockSpec((ROUTE_ROWS, tm), lambda i: (0, i)), pl.BlockSpec((1, 8, LANES), lambda i: (i, 0, 0))],
        out_shape=[jax.ShapeDtypeStruct((ROUTE_ROWS, nt * tm), F32), jax.ShapeDtypeStruct((nt, 8, LANES), F32)],
        compiler_params=_cparams(("arbitrary",)),
        name="moe_route",
    )(xm, xt, nw, wr, br)


def _to_tiles(ref, base, val):
    m, rt = val.shape[0], val.shape[1] // LANES
    for j in range(rt):
        ref[pl.ds(base * rt + j, m, stride=rt), :] = val[:, j * LANES:(j + 1) * LANES]


def _from_tiles(ref, base, m, rt):
    return jnp.concatenate([ref[pl.ds(base * rt + j, m, stride=rt), :] for j in range(rt)], axis=1)


SEG_W = 16
WIN_HDR = 2


def _max_windows(tm):
    return N_EXPERTS + TOP_K * tm // SEG_W


def _seg_copy(src, i, dst, j, sem, rt):
    n = SEG_W * rt
    return pltpu.make_async_copy(src.at[pl.ds(pl.multiple_of(i * rt, rt), n), :],
                                 dst.at[pl.ds(pl.multiple_of(j * rt, rt), n), :], sem)


def _tok(ref, p, rt):
    return ref.at[pl.ds(pl.multiple_of(p * rt, rt), rt), :]


def _dispatch_body(lpos_ref, win_ref, zwin_ref, xm_ref, xt_ref, nw_ref, xb_ref, h_buf, s_buf, sem, *, ntm):
    tm, rt = xm_ref.shape[0], xm_ref.shape[1] // LANES
    i = pl.program_id(0)
    half = TOP_K * tm + SEG_W
    sbase = (i % 2) * half
    mw = _max_windows(tm)

    @pl.when(i == 0)
    def _():
        for hb in range(2):
            s_buf[(hb * half + TOP_K * tm) * rt:(hb + 1) * half * rt, :] = jnp.zeros((SEG_W * rt, LANES), F32)

        def zissue(w, carry):
            _seg_copy(s_buf, TOP_K * tm, xb_ref, zwin_ref[1 + w], sem, rt).start()
            return carry

        def zdrain(w, carry):
            _seg_copy(s_buf, 0, xb_ref, 0, sem, rt).wait()
            return carry

        lax.fori_loop(0, zwin_ref[0], zissue, 0)
        lax.fori_loop(0, zwin_ref[0], zdrain, 0)

    _to_tiles(h_buf, 0, _rms(_x_tile(xm_ref, xt_ref, ntm), nw_ref[...]))

    def move(t, carry):
        v = _tok(h_buf, t, rt)[...]
        for k in range(TOP_K):
            _tok(s_buf, sbase + lpos_ref[0, 0, k * tm + t], rt)[...] = v
        return carry

    lax.fori_loop(0, tm, move, 0, unroll=8)

    def drain(w, carry):
        _seg_copy(s_buf, 0, xb_ref, 0, sem, rt).wait()
        return carry

    @pl.when(i > 0)
    def _():
        lax.fori_loop(0, win_ref[0, 0, 1], drain, 0)

    def issue(w, carry):
        _seg_copy(s_buf, sbase + win_ref[0, 0, WIN_HDR + w], xb_ref, win_ref[0, 0, WIN_HDR + mw + w], sem, rt).start()
        return carry

    lax.fori_loop(0, win_ref[0, 0, 0], issue, 0)

    @pl.when(i == pl.num_programs(0) - 1)
    def _():
        lax.fori_loop(0, win_ref[0, 0, 0], drain, 0)


def _dispatch(xm, xt, nw, lpos, win, zwin, *, tm, n_slots):
    d = xm.shape[1]
    ntm = xm.shape[0] // tm
    nt = ntm + 1
    rt = d // LANES
    return pl.pallas_call(
        functools.partial(_dispatch_body, ntm=ntm),
        grid=(nt,),
        in_specs=[
            pl.BlockSpec((1, 1, lpos.shape[2]), lambda i: (i, 0, 0), memory_space=pltpu.SMEM),
            pl.BlockSpec((1, 1, win.shape[2]), lambda i: (i, 0, 0), memory_space=pltpu.SMEM),
            pl.BlockSpec(memory_space=pltpu.SMEM),
        ] + _x_specs(xm, tm) + [
            _full(nw.shape),
        ],
        out_specs=pl.BlockSpec(memory_space=pl.ANY),
        out_shape=jax.ShapeDtypeStruct((n_slots * rt, LANES), F32),
        scratch_shapes=[pltpu.VMEM((tm * rt, LANES), F32),
                        pltpu.VMEM((2 * (TOP_K * tm + SEG_W) * rt, LANES), F32),
                        pltpu.SemaphoreType.DMA(())],
        compiler_params=_cparams(("arbitrary",)),
        name="moe_dispatch",
    )(lpos, win, zwin, xm, xt, nw)


def _expert_body(be_ref, nu_ref, xb_ref, wg_ref, wu_ref, wd_ref, yb_ref, wg_buf, wu_buf, wd_buf):
    b = pl.program_id(0)
    prev = be_ref[jnp.maximum(b - 1, 0)]
    fresh = (b == 0) | (be_ref[b] != prev)

    @pl.when((b < nu_ref[0]) & fresh)
    def _():
        wg_buf[...] = wg_ref[0, 0].astype(BF16)
        wu_buf[...] = wu_ref[0, 0].astype(BF16)
        wd_buf[...] = wd_ref[0, 0].astype(BF16)

    @pl.when(b < nu_ref[0])
    def _():
        xb = _from_tiles(xb_ref, 0, MOE_BLOCK, wg_buf.shape[0] // LANES).astype(BF16)
        gate = jnp.dot(xb, wg_buf[...], preferred_element_type=F32)
        up = jnp.dot(xb, wu_buf[...], preferred_element_type=F32)
        hid = (_silu(gate) * up).astype(BF16)
        _to_tiles(yb_ref, 0, jnp.dot(hid, wd_buf[...], preferred_element_type=F32))

    @pl.when(b >= nu_ref[0])
    def _():
        yb_ref[...] = jnp.zeros_like(yb_ref)


def _experts(blk_exp, n_used, xb, wg, wu, wd, *, layer):
    d, f = wg.shape[2], wg.shape[3]
    rt = d // LANES
    nb = xb.shape[0] // rt // MOE_BLOCK
    blk_rows = MOE_BLOCK * rt

    def xmap(b, be, nu):
        return (jnp.minimum(b, jnp.maximum(nu[0] - 1, 0)), 0)

    def wmap(b, be, nu):
        return (layer, be[b], 0, 0)

    return pl.pallas_call(
        _expert_body,
        grid_spec=pltpu.PrefetchScalarGridSpec(
            num_scalar_prefetch=2,
            grid=(nb,),
            in_specs=[
                pl.BlockSpec((blk_rows, LANES), xmap),
                pl.BlockSpec((1, 1, d, f), wmap), pl.BlockSpec((1, 1, d, f), wmap),
                pl.BlockSpec((1, 1, f, d), wmap),
            ],
            out_specs=pl.BlockSpec((blk_rows, LANES), lambda b, be, nu: (b, 0)),
            scratch_shapes=[pltpu.VMEM((d, f), BF16), pltpu.VMEM((d, f), BF16), pltpu.VMEM((f, d), BF16)],
        ),
        out_shape=jax.ShapeDtypeStruct(xb.shape, F32),
        compiler_params=_cparams(("arbitrary",)),
        name="moe_experts",
    )(blk_exp, n_used, xb, wg, wu, wd)


def _ybuf_tokens(tm):
    return TOP_K * tm + N_EXPERTS * (SEG_W - 1) + SEG_W


def _combine_body(lpos_ref, gate_ref, win_ref, winn_ref, xm_ref, xt_ref, fw_ref, yb_ref, om_ref, ot_ref,
                  y_buf, x_buf, sem, *, ntm, final_norm):
    tm, rt = xm_ref.shape[0], xm_ref.shape[1] // LANES
    i = pl.program_id(0)
    slot = i % 2
    half = _ybuf_tokens(tm)
    mw = _max_windows(tm)

    def fetch(tab_ref, sl):
        def issue(w, carry):
            _seg_copy(yb_ref, tab_ref[0, 0, WIN_HDR + w], y_buf, sl * half + tab_ref[0, 0, WIN_HDR + mw + w],
                      sem.at[sl], rt).start()
            return carry

        lax.fori_loop(0, tab_ref[0, 0, 0], issue, 0)

    @pl.when(i == 0)
    def _():
        fetch(win_ref, 0)

    @pl.when(i + 1 < pl.num_programs(0))
    def _():
        fetch(winn_ref, 1 - slot)

    _to_tiles(x_buf, 0, _x_tile(xm_ref, xt_ref, ntm))

    def drain(w, carry):
        _seg_copy(yb_ref, 0, y_buf, 0, sem.at[slot], rt).wait()
        return carry

    lax.fori_loop(0, win_ref[0, 0, 0], drain, 0)
    ybase = slot * half

    def comb(t, carry):
        acc = _tok(x_buf, t, rt)[...]
        for k in range(TOP_K):
            a = k * tm + t
            acc = acc + gate_ref[0, 0, a] * _tok(y_buf, ybase + lpos_ref[0, 0, a], rt)[...]
        _tok(x_buf, t, rt)[...] = acc
        return carry

    lax.fori_loop(0, tm, comb, 0, unroll=8)
    out = _from_tiles(x_buf, 0, tm, rt)
    if final_norm:
        out = _rms(out, fw_ref[...])

    @pl.when(i < ntm)
    def _():
        om_ref[...] = out

    @pl.when(i == ntm)
    def _():
        ot_ref[...] = out


def _combine(xm, xt, lpos, gates, win, yb, fw, *, tm, final_norm):
    d = xm.shape[1]
    ntm = xm.shape[0] // tm
    nt = ntm + 1
    rt = d // LANES

    def smem(arr, imap):
        return pl.BlockSpec((1, 1, arr.shape[2]), imap, memory_space=pltpu.SMEM)

    return pl.pallas_call(
        functools.partial(_combine_body, ntm=ntm, final_norm=final_norm),
        grid=(nt,),
        in_specs=[
            smem(lpos, lambda i: (i, 0, 0)),
            smem(gates, lambda i: (i, 0, 0)),
            smem(win, lambda i: (i, 0, 0)),
            smem(win, lambda i: (jnp.minimum(i + 1, nt - 1), 0, 0)),
        ] + _x_specs(xm, tm) + [
            _full(fw.shape),
            pl.BlockSpec(memory_space=pl.ANY),
        ],
        out_specs=_x_specs(xm, tm),
        out_shape=[jax.ShapeDtypeStruct(xm.shape, F32), jax.ShapeDtypeStruct(xt.shape, F32)],
        scratch_shapes=[pltpu.VMEM((2 * _ybuf_tokens(tm) * rt, LANES), F32),
                        pltpu.VMEM((tm * rt, LANES), F32),
                        pltpu.SemaphoreType.DMA((2,))],
        compiler_params=_cparams(("arbitrary",)),
        name="moe_combine",
    )(lpos, gates, win, win, xm, xt, fw, yb)


def _moe(xm, xt, nw, w_group, b_group, w_expert, b_expert, wg, wu, wd, fw, *, layer, final_norm):
    tm, d = xt.shape
    nt = xm.shape[0] // tm + 1
    t = nt * tm
    pad = LANES - N_EXPERTS - N_EXPERT_GROUPS
    wr = jnp.pad(jnp.concatenate([w_expert, w_group], axis=1), ((0, 0), (0, pad)))
    wr_hi = wr.astype(BF16)
    wr = jnp.concatenate([wr_hi, (wr - wr_hi.astype(F32)).astype(BF16)], axis=1)
    br = jnp.pad(jnp.concatenate([b_expert, b_group]), (0, pad)).reshape(1, LANES)
    nw2 = nw.reshape(1, d)
    info, cnt = _route(xm, xt, nw2, wr, br, tm=tm)

    def per_tile(rows):
        return rows.reshape(TOP_K, nt, tm).transpose(1, 0, 2).reshape(nt, 1, TOP_K * tm)

    gates = per_tile(info[0:TOP_K])
    lpos = per_tile(info[4:4 + TOP_K].astype(I32))
    lpos_al = per_tile(info[6:6 + TOP_K].astype(I32))
    cnt = cnt[:, 0, :N_EXPERTS].astype(I32)
    total = jnp.sum(cnt, axis=0)
    padded = jnp.where(total > 0, (total + SEG_W + MOE_BLOCK - 2) // MOE_BLOCK * MOE_BLOCK, 0)
    pend = jnp.cumsum(padded)
    pstart = pend - padded
    gstart = pstart[None, :] + jnp.cumsum(cnt, axis=0) - cnt
    lstart = jnp.cumsum(cnt, axis=1) - cnt
    nwin_e = (cnt + SEG_W - 1) // SEG_W
    lstart_al = (jnp.cumsum(nwin_e, axis=1) - nwin_e) * SEG_W

    winc = jnp.cumsum(nwin_e, axis=1)
    nwin = winc[:, -1:]
    mw = _max_windows(tm)
    j = jnp.arange(mw, dtype=I32)[None, :]
    owner = (jnp.sum(winc[:, None, :] <= j[:, :, None], axis=-1)[:, :, None]
             == jnp.arange(N_EXPERTS, dtype=I32)[None, None, :])
    pick = lambda tab: jnp.sum(jnp.where(owner, tab[:, None, :], 0), axis=-1)
    w_off = (j - pick(winc - nwin_e)) * SEG_W
    live = j < nwin
    src_loc = jnp.where(live, pick(lstart) + w_off, 0)
    slot_g = jnp.where(live, pick(gstart) + w_off, 0)
    dst_loc = jnp.where(live, pick(lstart_al) + w_off, 0)
    nprev = jnp.concatenate([jnp.zeros((1, 1), I32), nwin[:-1]], axis=0)
    win_d = jnp.concatenate([nwin, nprev, src_loc, slot_g], axis=1).reshape(nt, 1, WIN_HDR + 2 * mw)
    win_c = jnp.concatenate([nwin, nprev, slot_g, dst_loc], axis=1).reshape(nt, 1, WIN_HDR + 2 * mw)
    n_blocks = -(-(t * TOP_K + N_EXPERTS * (MOE_BLOCK + SEG_W - 2)) // MOE_BLOCK)
    n_slots = n_blocks * MOE_BLOCK
    zfirst = jnp.concatenate([pstart + total // SEG_W * SEG_W, pend[-1:]])
    zend = jnp.concatenate([pend, jnp.full((1,), n_slots, I32)])
    nz_e = (zend - zfirst) // SEG_W
    zinc = jnp.cumsum(nz_e)
    mz = N_EXPERTS * ((MOE_BLOCK + 2 * SEG_W) // SEG_W) + n_slots // SEG_W - TOP_K * t // SEG_W
    jz = jnp.arange(mz, dtype=I32)
    zowner = (jnp.sum(zinc[None, :] <= jz[:, None], axis=-1)[:, None]
              == jnp.arange(N_EXPERTS + 1, dtype=I32)[None, :])
    zpick = lambda tab: jnp.sum(jnp.where(zowner, tab[None, :], 0), axis=-1)
    zslot = jnp.where(jz < zinc[-1], zpick(zfirst) + (jz - zpick(zinc - nz_e)) * SEG_W, 0)
    zwin = jnp.concatenate([zinc[-1:], zslot]).astype(I32)
    blk_exp = jnp.minimum(jnp.sum(pend[None, :] <= (jnp.arange(n_blocks, dtype=I32) * MOE_BLOCK)[:, None], axis=1),
                          N_EXPERTS - 1).astype(I32)
    n_used = (pend[-1] // MOE_BLOCK).astype(I32).reshape(1)

    xb = _dispatch(xm, xt, nw2, lpos, win_d, zwin, tm=tm, n_slots=n_slots)
    yb = _experts(blk_exp, n_used, xb, wg, wu, wd, layer=layer)
    return _combine(xm, xt, lpos_al, gates, win_c, yb, fw.reshape(1, d), tm=tm, final_norm=final_norm)


def _linear_body(x_ref, nw_ref, w_ref, b_ref, r_ref, out_ref, *, norm):
    x = x_ref[...]
    if norm:
        x = _rms(x, nw_ref[...])
    out_ref[...] = _bdot(x, w_ref[...]) + b_ref[...] + r_ref[...]


def _linear(x, w, *, nw=None, bias=None, res=None, tn):
    m, kd = x.shape
    n = w.shape[1]
    norm = nw is not None
    nw = jnp.ones((1, kd), F32) if nw is None else nw
    bias = jnp.zeros((1, n), F32) if bias is None else bias
    res = jnp.zeros((m, n), F32) if res is None else res
    tn = min(tn, n)
    return pl.pallas_call(
        functools.partial(_linear_body, norm=norm),
        grid=(n // tn,),
        in_specs=[
            _full(x.shape), _full(nw.shape),
            pl.BlockSpec((kd, tn), lambda j: (0, j)),
            pl.BlockSpec((1, tn), lambda j: (0, j)),
            pl.BlockSpec((m, tn), lambda j: (0, j)),
        ],
        out_specs=pl.BlockSpec((m, tn), lambda j: (0, j)),
        out_shape=jax.ShapeDtypeStruct((m, n), F32),
        compiler_params=_cparams(("arbitrary",)),
        name="sample_linear",
    )(x, nw, w, bias, res)


def _sample_conv_body(xbc_ref, st_ref, cw_ref, cb_ref, dtr_ref, dtb_ref, xc_ref, stn_ref, dt_ref, cbg_ref,
                      *, d_inner):
    nst = D_STATE
    xbc = xbc_ref[...]
    acc = cb_ref[...] + cw_ref[3:4, :] * xbc
    for k in range(CONV_W - 1):
        acc = acc + cw_ref[k:k + 1, :] * st_ref[k]
    xc = _silu(acc)
    xc_ref[...] = xc
    stn_ref[0] = st_ref[1]
    stn_ref[1] = st_ref[2]
    stn_ref[2] = xbc
    dt_ref[...] = _softplus(dtr_ref[...] + dtb_ref[...])
    lane = lax.broadcasted_iota(I32, (xbc.shape[0], LANES), 1)
    cbg = jnp.zeros((xbc.shape[0], LANES), F32)
    for g in range(N_BC_GROUPS):
        b_g = xc[:, d_inner + g * nst:d_inner + (g + 1) * nst]
        c_g = xc[:, d_inner + (N_BC_GROUPS + g) * nst:d_inner + (N_BC_GROUPS + g + 1) * nst]
        cbg = jnp.where(lane == g, jnp.sum(b_g * c_g, axis=-1, keepdims=True), cbg)
    cbg_ref[...] = cbg


def _sample_conv(xbc, st_t, cw, cb, dtr, dtb, *, d_inner):
    m, cd = xbc.shape
    return pl.pallas_call(
        functools.partial(_sample_conv_body, d_inner=d_inner),
        grid=(1,),
        in_specs=[_full(xbc.shape), _full(st_t.shape), _full(cw.shape), _full(cb.shape), _full(dtr.shape),
                  _full(dtb.shape)],
        out_specs=[_full((m, cd)), _full(st_t.shape), _full((m, LANES)), _full((m, LANES))],
        out_shape=[jax.ShapeDtypeStruct((m, cd), F32), jax.ShapeDtypeStruct(st_t.shape, F32),
                   jax.ShapeDtypeStruct((m, LANES), F32), jax.ShapeDtypeStruct((m, LANES), F32)],
        compiler_params=_cparams(("arbitrary",)),
        name="sample_conv",
    )(xbc, st_t, cw, cb, dtr, dtb)


SSD_REQS_PER_STEP = 4


def _sample_ssd_body(s0_ref, xt_ref, bc_ref, hs_ref, par_ref, sn_ref, yt_ref, *, n_heads):
    hp = SSM_HEAD_DIM
    hpg = n_heads // N_BC_GROUPS
    a = -jnp.exp(par_ref[0:1, :])
    dsk = par_ref[1:2, :]
    head_row = lax.broadcasted_iota(I32, (n_heads, s0_ref.shape[3]), 0)
    for r in range(s0_ref.shape[0]):
        xt = xt_ref[r]
        dt = hs_ref[r, 0:1, :]
        cbh = hs_ref[r, 1:2, :]
        dec = jnp.exp(dt * a)
        xdt = xt * dt
        xdt_b = xdt.astype(BF16)
        yoff = jnp.zeros((hp, n_heads), F32)
        for hh in range(n_heads):
            g = hh // hpg
            b_row = bc_ref[r, g:g + 1, :]
            c_row = bc_ref[r, N_BC_GROUPS + g:N_BC_GROUPS + g + 1, :]
            s0 = s0_ref[r, hh]
            yoff = yoff + _bdot_nt(s0, jnp.where(head_row == hh, c_row, 0.0))
            b_sel = jnp.where(head_row == hh, b_row, 0.0).astype(BF16)
            sn_ref[r, hh] = s0 * dec[:, hh:hh + 1] + jnp.dot(xdt_b, b_sel, preferred_element_type=F32)
        yt_ref[r] = cbh * xdt + yoff * dec + xt * dsk


def _sample_ssd(s0, xt, bc, hs, par):
    bsz, n_heads, hp, nst = s0.shape
    rb = SSD_REQS_PER_STEP if bsz % SSD_REQS_PER_STEP == 0 else 1
    return pl.pallas_call(
        functools.partial(_sample_ssd_body, n_heads=n_heads),
        grid=(bsz // rb,),
        in_specs=[
            pl.BlockSpec((rb, n_heads, hp, nst), lambda b: (b, 0, 0, 0)),
            pl.BlockSpec((rb, hp, n_heads), lambda b: (b, 0, 0)),
            pl.BlockSpec((rb,) + bc.shape[1:], lambda b: (b, 0, 0)),
            pl.BlockSpec((rb,) + hs.shape[1:], lambda b: (b, 0, 0)),
            _full(par.shape),
        ],
        out_specs=[
            pl.BlockSpec((rb, n_heads, hp, nst), lambda b: (b, 0, 0, 0)),
            pl.BlockSpec((rb, hp, n_heads), lambda b: (b, 0, 0)),
        ],
        out_shape=[jax.ShapeDtypeStruct(s0.shape, F32), jax.ShapeDtypeStruct((bsz, hp, n_heads), F32)],
        compiler_params=_cparams(("arbitrary",)),
        name="sample_ssd",
    )(s0, xt, bc, hs, par)


def _sample_gnorm_out_body(y_ref, z_ref, gnw_ref, wout_ref, x_ref, out_ref, *, d_inner):
    gw = d_inner // N_BC_GROUPS
    acc = x_ref[...]
    for g in range(N_BC_GROUPS):
        glanes = slice(g * gw, (g + 1) * gw)
        gg = y_ref[:, glanes] * _silu(z_ref[:, glanes])
        gg = gg * lax.rsqrt(jnp.mean(gg * gg, axis=-1, keepdims=True) + EPS) * gnw_ref[:, glanes]
        acc = acc + jnp.dot(gg.astype(BF16), wout_ref[glanes, :], preferred_element_type=F32)
    out_ref[...] = acc


def _sample_gnorm_out(y, z, gnw, wout, x):
    d_inner = y.shape[1]
    return pl.pallas_call(
        functools.partial(_sample_gnorm_out_body, d_inner=d_inner),
        grid=(1,),
        in_specs=[_full(y.shape), _full(z.shape), _full(gnw.shape), _full(wout.shape), _full(x.shape)],
        out_specs=_full(x.shape),
        out_shape=jax.ShapeDtypeStruct(x.shape, F32),
        compiler_params=_cparams(("arbitrary",)),
        name="sample_gnorm_out",
    )(y, z, gnw, wout, x)


def _sample_attn_body(q_ref, kn_ref, vn_ref, kc_ref, vc_ref, sink_ref, o_ref, ko_ref, vo_ref, s_buf, sn_buf):
    bt = q_ref.shape[0]
    wb = kc_ref.shape[1]
    hd = HEAD_DIM
    nh = N_Q_HEADS
    qpk = N_Q_HEADS // N_KV_HEADS
    scale = hd ** -0.5
    for b in range(bt):
        kn = kn_ref[b]
        kn_h = jnp.concatenate([jnp.broadcast_to(kn[:, g * hd:(g + 1) * hd], (qpk, hd))
                                for g in range(N_KV_HEADS)], axis=0)
        sn_buf[b * nh:(b + 1) * nh, :] = jnp.sum(q_ref[b] * kn_h, axis=-1, keepdims=True) * scale
        for g in range(N_KV_HEADS):
            rows = slice(b * nh + g * qpk, b * nh + (g + 1) * qpk)
            s_buf[rows, :] = _bdot_nt(q_ref[b, g * qpk:(g + 1) * qpk, :], kc_ref[b, :, g * hd:(g + 1) * hd]) * scale
    s = s_buf[...]
    s_new = sn_buf[...]
    sink = sink_ref[...]
    m = jnp.maximum(jnp.maximum(jnp.max(s, axis=-1, keepdims=True), s_new), sink)
    p = jnp.exp(s - m)
    p_new = jnp.exp(s_new - m)
    inv = 1.0 / (jnp.sum(p, axis=-1, keepdims=True) + p_new + jnp.exp(sink - m))
    s_buf[...] = p * inv
    sn_buf[...] = p_new * inv
    for b in range(bt):
        vn = vn_ref[b]
        for g in range(N_KV_HEADS):
            rows = slice(b * nh + g * qpk, b * nh + (g + 1) * qpk)
            cols = slice(g * hd, (g + 1) * hd)
            o_ref[b, g * qpk:(g + 1) * qpk, :] = (_bdot(s_buf[rows, :], vc_ref[b, :, cols])
                                                   + sn_buf[rows, :] * vn[:, cols])
        ko_ref[b, 0:wb - 1, :] = kc_ref[b, 1:wb, :]
        ko_ref[b, wb - 1:wb, :] = kn_ref[b]
        vo_ref[b, 0:wb - 1, :] = vc_ref[b, 1:wb, :]
        vo_ref[b, wb - 1:wb, :] = vn


def _sample_attn(q3, kn, vn, kc, vc, sinks, *, bt):
    bsz, nqh, hd = q3.shape
    wb, nk = kc.shape[1], kc.shape[2]
    return pl.pallas_call(
        _sample_attn_body,
        grid=(bsz // bt,),
        in_specs=[
            pl.BlockSpec((bt, nqh, hd), lambda i: (i, 0, 0)),
            pl.BlockSpec((bt, 1, nk), lambda i: (i, 0, 0)),
            pl.BlockSpec((bt, 1, nk), lambda i: (i, 0, 0)),
            pl.BlockSpec((bt, wb, nk), lambda i: (i, 0, 0)),
            pl.BlockSpec((bt, wb, nk), lambda i: (i, 0, 0)),
            _full(sinks.shape),
        ],
        out_specs=[
            pl.BlockSpec((bt, nqh, hd), lambda i: (i, 0, 0)),
            pl.BlockSpec((bt, wb, nk), lambda i: (i, 0, 0)),
            pl.BlockSpec((bt, wb, nk), lambda i: (i, 0, 0)),
        ],
        out_shape=[jax.ShapeDtypeStruct(q3.shape, F32), jax.ShapeDtypeStruct(kc.shape, F32),
                   jax.ShapeDtypeStruct(vc.shape, F32)],
        scratch_shapes=[pltpu.VMEM((bt * nqh, wb), F32), pltpu.VMEM((bt * nqh, 1), F32)],
        compiler_params=_cparams(("arbitrary",)),
        name="sample_attn",
    )(q3, kn, vn, kc, vc, sinks)


def _mamba_sample(x, nw, mw, state_conv, state_ssm):
    win, cw, cb, dtb, alog, dsk, gnw, wout = mw
    bsz, d = x.shape
    d_inner = wout.shape[0]
    conv_dim = cw.shape[1]
    n_heads = d_inner // SSM_HEAD_DIM
    hp = SSM_HEAD_DIM
    proj = _linear(x, win, nw=nw, tn=896)
    z = proj[:, :d_inner]
    xbc = proj[:, d_inner:d_inner + conv_dim]
    dtr = proj[:, d_inner + conv_dim:]
    st_t = jnp.transpose(state_conv, (1, 0, 2))
    xc, stn_t, dt, cbg = _sample_conv(xbc, st_t, cw, cb, dtr, dtb, d_inner=d_inner)
    conv_new = jnp.transpose(stn_t, (1, 0, 2))
    xt = jnp.transpose(xc[:, :d_inner].reshape(bsz, n_heads, hp), (0, 2, 1))
    bc = xc[:, d_inner:].reshape(bsz, 2 * N_BC_GROUPS, D_STATE)
    cbh = jnp.repeat(cbg[:, :N_BC_GROUPS], n_heads // N_BC_GROUPS, axis=1)
    hs = jnp.stack([dt[:, :n_heads], cbh], axis=1)
    par = jnp.stack([alog[0, :n_heads], dsk.reshape(n_heads, hp)[:, 0]], axis=0)
    ssm_new, yt = _sample_ssd(state_ssm, xt, bc, hs, par)
    y = jnp.transpose(yt, (0, 2, 1)).reshape(bsz, d_inner)
    out = _sample_gnorm_out(y, z, gnw, wout, x)
    return out, conv_new, ssm_new


def _attn_sample(x, nw, wqkv, bqkv, sinks, wo, bo, cache_k, cache_v):
    bsz, d = x.shape
    wb = cache_k.shape[1]
    nq = N_Q_HEADS * HEAD_DIM
    nk = N_KV_HEADS * HEAD_DIM
    qkv = _linear(x, wqkv, nw=nw, bias=bqkv, tn=512)
    q3 = qkv[:, :nq].reshape(bsz, N_Q_HEADS, HEAD_DIM)
    kn = qkv[:, nq:nq + nk].reshape(bsz, 1, nk)
    vn = qkv[:, nq + nk:].reshape(bsz, 1, nk)
    o3, ko, vo = _sample_attn(q3, kn, vn, cache_k.reshape(bsz, wb, nk), cache_v.reshape(bsz, wb, nk),
                              jnp.tile(sinks.reshape(N_Q_HEADS, 1), (8, 1)), bt=8)
    out = _linear(o3.reshape(bsz, nq), wo, bias=bo, res=x, tn=512)
    return out, ko.reshape(cache_k.shape), vo.reshape(cache_v.shape)


def kernel(x_prompt, x_sample, state_ssm, state_conv, cache_k_win, cache_v_win,
           mamba_w_in, mamba_conv_w, mamba_conv_b, mamba_dt_bias, mamba_a_log, mamba_d,
           mamba_norm_w, mamba_w_out, attn_w_qkv, attn_b_qkv, attn_sinks, attn_w_o, attn_b_o,
           norm_mix, norm_ffn, router_w_group, router_b_group, router_w_expert, router_b_expert,
           expert_w_gate, expert_w_up, expert_w_down, norm_final):
    bsz, seq, d = x_prompt.shape
    dbsz, dseq, _ = x_sample.shape
    assert dseq == 1 and cache_k_win.shape[2] <= WINDOW and seq % WINDOW == 0
    assert dbsz <= MOE_TILE and (bsz * seq) % MOE_TILE == 0
    depth = norm_mix.shape[0]
    xp = x_prompt.reshape(bsz * seq, d)
    xs = x_sample.reshape(dbsz, d)
    ssm_p, conv_p, kp_l, vp_l = [], [], [], []
    ssm_s, conv_s, ks_l, vs_l = [], [], [], []
    for i in range(depth):
        j = i // 2
        nw = norm_mix[i].reshape(1, d)
        if i % 2 == 0:
            mw = _mamba_weights(mamba_w_in[j], mamba_conv_w[j], mamba_conv_b[j], mamba_dt_bias[j],
                                mamba_a_log[j], mamba_d[j], mamba_norm_w[j], mamba_w_out[j])
            xp, cp, sp = _mamba_prompt(xp, nw, *mw, bsz=bsz, seq=seq, ts=2 * SSD_CHUNK)
            xs, cs_, ss_ = _mamba_sample(xs, nw, mw, state_conv[j], state_ssm[j])
            ssm_p.append(sp)
            conv_p.append(cp)
            ssm_s.append(ss_)
            conv_s.append(cs_)
        else:
            wqkv = attn_w_qkv[j].astype(BF16)
            bqkv = attn_b_qkv[j].reshape(1, -1)
            wo = attn_w_o[j].astype(BF16)
            bo = attn_b_o[j].reshape(1, d)
            xp, kp, vp = _attn_prompt(xp, attn_sinks[j], nw, wqkv, bqkv, wo, bo, bsz=bsz, seq=seq,
                                      tq=WINDOW)
            xs, ks_, vs_ = _attn_sample(xs, nw, wqkv, bqkv, attn_sinks[j], wo, bo, cache_k_win[j], cache_v_win[j])
            kp_l.append(kp.reshape(bsz, WINDOW, N_KV_HEADS, HEAD_DIM))
            vp_l.append(vp.reshape(bsz, WINDOW, N_KV_HEADS, HEAD_DIM))
            ks_l.append(ks_)
            vs_l.append(vs_)
        last = i == depth - 1
        moe_w = (norm_ffn[i], router_w_group[i], router_b_group[i], router_w_expert[i], router_b_expert[i],
                 expert_w_gate, expert_w_up, expert_w_down, norm_final)
        xs_tile = jnp.pad(xs, ((0, MOE_TILE - dbsz), (0, 0)))
        xp, xs_tile = _moe(xp, xs_tile, *moe_w, layer=i, final_norm=last)
        xs = xs_tile[:dbsz]
    return (xp.reshape(bsz, seq, d), xs.reshape(dbsz, dseq, d),
            jnp.stack(ssm_p), jnp.stack(conv_p), jnp.stack(kp_l), jnp.stack(vp_l),
            jnp.stack(ssm_s), jnp.stack(conv_s), jnp.stack(ks_l), jnp.stack(vs_l))
```

```python
import functools
import math

import jax
import jax.numpy as jnp
from jax import lax
from jax.experimental import pallas as pl
from jax.experimental.pallas import tpu as pltpu

F32 = jnp.float32
BF16 = jnp.bfloat16
I32 = jnp.int32

EPS = 1e-5
LANES = 128
VMEM_LIMIT = 56 * 1024 * 1024

SSM_HEAD_DIM = 64
D_STATE = 128
N_BC_GROUPS = 8
CONV_W = 4
SSD_CHUNK = 128
N_Q_HEADS = 16
N_KV_HEADS = 4
HEAD_DIM = 64
WINDOW = 128
N_EXPERT_GROUPS = 4
EXPERTS_PER_GROUP = 8
N_EXPERTS = N_EXPERT_GROUPS * EXPERTS_PER_GROUP
TOP_K = 2
MOE_BLOCK = 256
MOE_TILE = 512


def _cparams(sem):
    return pltpu.CompilerParams(dimension_semantics=sem, vmem_limit_bytes=VMEM_LIMIT)


def _full(shape):
    n = len(shape)
    return pl.BlockSpec(shape, lambda *_: (0,) * n)


def _resident(shape):
    n = len(shape)
    return pl.BlockSpec(shape, lambda *_: (0,) * n, pipeline_mode=pl.Buffered(1))


def _rms(x, w):
    return x * lax.rsqrt(jnp.mean(x * x, axis=-1, keepdims=True) + EPS) * w


def _silu(x):
    return x / (1.0 + jnp.exp(-x))


def _softplus(x):
    return jnp.maximum(x, 0.0) + jnp.log(1.0 + jnp.exp(-jnp.abs(x)))


def _bdot(a, b):
    return jnp.dot(a.astype(BF16), b.astype(BF16), preferred_element_type=F32)


def _bdot_nt(a, b):
    return lax.dot_general(a.astype(BF16), b.astype(BF16), (((1,), (1,)), ((), ())),
                           preferred_element_type=F32)


def _bdot_tn(a, b):
    return lax.dot_general(a.astype(BF16), b.astype(BF16), (((0,), (0,)), ((), ())),
                           preferred_element_type=F32)


def _split3(v):
    hi = v.astype(BF16)
    r1 = v - hi.astype(F32)
    mid = r1.astype(BF16)
    lo = (r1 - mid.astype(F32)).astype(BF16)
    return hi, mid, lo


def _dot01_left(m01, v):
    m = m01.astype(BF16)
    return jnp.dot(jnp.concatenate([m, m, m], axis=1), jnp.concatenate(_split3(v), axis=0),
                   preferred_element_type=F32)


def _dot01_right(v, m01):
    m = m01.astype(BF16)
    return jnp.dot(jnp.concatenate(_split3(v), axis=1), jnp.concatenate([m, m, m], axis=0),
                   preferred_element_type=F32)


def _spread(v, onehot_ref):
    hi = v.astype(BF16)
    lo = (v - hi.astype(F32)).astype(BF16)
    return jnp.dot(jnp.concatenate([hi, lo], axis=1), onehot_ref[...], preferred_element_type=F32)


def _mamba_prompt_body(x_ref, nw_ref, win_ref, cw_ref, cb_ref, dtb_ref, alog_ref, dsk_ref, gnw_ref,
                       wout_ref, hexp_ref, out_ref, conv_ref, ssm_ref, h_buf, xbc_buf, xc_buf, st_buf, y_buf,
                       wexp_buf, eexp_buf, *, d_inner, n_heads):
    ts = x_ref.shape[0]
    cs = SSD_CHUNK
    hp = SSM_HEAD_DIM
    nst = D_STATE
    gw = d_inner // N_BC_GROUPS
    hpg = n_heads // N_BC_GROUPS
    conv_dim = d_inner + 2 * N_BC_GROUPS * nst
    s = pl.program_id(1)

    @pl.when(s == 0)
    def _():
        xbc_buf[:, 0:8, :] = jnp.zeros((conv_dim // LANES, 8, LANES), F32)
        st_buf[...] = jnp.zeros_like(st_buf)

    h_buf[...] = _rms(x_ref[...], nw_ref[...]).astype(BF16)
    dtr = jnp.dot(h_buf[...], win_ref[:, d_inner + conv_dim:], preferred_element_type=F32)
    ct = 512
    spp = ct // LANES
    for j in range(conv_dim // ct):
        cols = slice(j * ct, (j + 1) * ct)
        piece = jnp.dot(h_buf[...], win_ref[:, d_inner + j * ct:d_inner + (j + 1) * ct],
                        preferred_element_type=F32)
        for q in range(spp):
            xbc_buf[j * spp + q, 8:8 + ts, :] = piece[:, q * LANES:(q + 1) * LANES]

        def back(k):
            return jnp.concatenate([xbc_buf[j * spp + q, pl.ds(8 - k, ts), :] for q in range(spp)], axis=1)

        acc = cb_ref[:, cols] + cw_ref[3:4, cols] * piece
        acc = acc + cw_ref[2:3, cols] * back(1)
        acc = acc + cw_ref[1:2, cols] * back(2)
        acc = acc + cw_ref[0:1, cols] * back(3)
        xc_buf[:, cols] = _silu(acc)
    for c in range(conv_dim // LANES):
        last3 = xbc_buf[c, 5 + ts:8 + ts, :]
        xbc_buf[c, 5:8, :] = last3
        conv_ref[0, :, c * LANES:(c + 1) * LANES] = last3

    dt = _softplus(dtr + dtb_ref[...])
    da = dt * (-jnp.exp(alog_ref[...]))
    row = lax.broadcasted_iota(I32, (cs, cs), 0)
    col = lax.broadcasted_iota(I32, (cs, cs), 1)
    causal = row >= col
    tril = causal.astype(F32)
    lane = lax.broadcasted_iota(I32, (cs, LANES), 1)
    lo_half = lane < hp

    for c in range(ts // cs):
        rows = slice(c * cs, (c + 1) * cs)
        da_c = da[rows]
        dt_c = dt[rows]
        acum = _dot01_left(tril, da_c)
        acum_t = acum.T
        dt_t = dt_c.T
        a_last = acum[cs - 1:cs, :]
        to_end = jnp.exp(a_last - acum)
        w_all = dt_c * to_end
        ea = jnp.exp(acum)
        cd = jnp.exp(a_last)
        both = _spread(jnp.concatenate([w_all, ea], axis=0), hexp_ref)
        wexp_buf[...] = both[0:cs]
        eexp_buf[...] = both[cs:2 * cs]
        for g in range(N_BC_GROUPS):
            glanes = slice(g * gw, (g + 1) * gw)
            b_g = xc_buf[rows, d_inner + g * nst:d_inner + (g + 1) * nst]
            c_g = xc_buf[rows, d_inner + (N_BC_GROUPS + g) * nst:d_inner + (N_BC_GROUPS + g + 1) * nst]
            cb = _bdot_nt(c_g, b_g)
            y_off = _bdot(c_g, st_buf[:, glanes])
            xw_parts = []
            for pr in range(hpg // 2):
                h0 = g * hpg + 2 * pr
                lanes0 = slice(h0 * hp, h0 * hp + 2 * hp)
                x_pair = xc_buf[rows, lanes0]
                ms = []
                for k in range(2):
                    hh = h0 + k
                    seg = acum[:, hh:hh + 1] - acum_t[hh:hh + 1, :]
                    dec = jnp.exp(jnp.where(causal, seg, -jnp.inf))
                    ms.append(cb * dec * dt_t[hh:hh + 1, :])
                x_ab = jnp.concatenate([jnp.where(lo_half, x_pair, 0.0), jnp.where(lo_half, 0.0, x_pair)], axis=0)
                y_pair = (eexp_buf[:, lanes0] * y_off[:, 2 * pr * hp:2 * (pr + 1) * hp]
                          + _bdot(jnp.concatenate(ms, axis=1), x_ab))
                y_buf[rows, lanes0] = y_pair + x_pair * dsk_ref[:, lanes0]
                xw_parts.append(x_pair * wexp_buf[:, lanes0])
            xw = jnp.concatenate(xw_parts, axis=1)
            cd_parts = [jnp.broadcast_to(cd[:, g * hpg + k:g * hpg + k + 1], (1, hp)) for k in range(hpg)]
            cd_g = jnp.concatenate(cd_parts, axis=1)
            st_buf[:, glanes] = st_buf[:, glanes] * cd_g + _bdot_tn(b_g, xw)

    @pl.when(s == pl.num_programs(1) - 1)
    def _():
        for pr in range(n_heads // 2):
            t = st_buf[:, 2 * pr * hp:2 * (pr + 1) * hp].T
            ssm_ref[0, 2 * pr] = t[0:hp]
            ssm_ref[0, 2 * pr + 1] = t[hp:2 * hp]

    acc = x_ref[...]
    for g in range(N_BC_GROUPS):
        glanes = slice(g * gw, (g + 1) * gw)
        z = jnp.dot(h_buf[...], win_ref[:, glanes], preferred_element_type=F32)
        gg = y_buf[:, glanes] * _silu(z)
        gg = gg * lax.rsqrt(jnp.mean(gg * gg, axis=-1, keepdims=True) + EPS) * gnw_ref[:, glanes]
        acc = acc + jnp.dot(gg.astype(BF16), wout_ref[glanes, :], preferred_element_type=F32)
    out_ref[...] = acc


def _mamba_prompt(x, nw, win, cw, cb, dtb, alog, dsk, gnw, wout, *, bsz, seq, ts):
    d = x.shape[1]
    d_inner = wout.shape[0]
    n_heads = d_inner // SSM_HEAD_DIM
    conv_dim = cw.shape[1]
    ns = seq // ts
    body = functools.partial(_mamba_prompt_body, d_inner=d_inner, n_heads=n_heads)
    hexp = (jnp.arange(2 * LANES, dtype=I32)[:, None] % LANES
            == jnp.arange(d_inner, dtype=I32)[None, :] // SSM_HEAD_DIM).astype(BF16)
    return pl.pallas_call(
        body,
        grid=(bsz, ns),
        in_specs=[
            pl.BlockSpec((ts, d), lambda b, s: (b * ns + s, 0)),
            _full(nw.shape), _resident(win.shape), _full(cw.shape), _full(cb.shape), _full(dtb.shape),
            _full(alog.shape), _full(dsk.shape), _full(gnw.shape), _resident(wout.shape), _resident(hexp.shape),
        ],
        out_specs=[
            pl.BlockSpec((ts, d), lambda b, s: (b * ns + s, 0)),
            pl.BlockSpec((1, CONV_W - 1, conv_dim), lambda b, s: (b, 0, 0)),
            pl.BlockSpec((1, n_heads, SSM_HEAD_DIM, D_STATE), lambda b, s: (b, 0, 0, 0)),
        ],
        out_shape=[
            jax.ShapeDtypeStruct((bsz * seq, d), F32),
            jax.ShapeDtypeStruct((bsz, CONV_W - 1, conv_dim), F32),
            jax.ShapeDtypeStruct((bsz, n_heads, SSM_HEAD_DIM, D_STATE), F32),
        ],
        scratch_shapes=[
            pltpu.VMEM((ts, d), BF16),
            pltpu.VMEM((conv_dim // LANES, 8 + ts, LANES), F32),
            pltpu.VMEM((ts, conv_dim), F32),
            pltpu.VMEM((D_STATE, d_inner), F32),
            pltpu.VMEM((ts, d_inner), F32),
            pltpu.VMEM((SSD_CHUNK, d_inner), F32),
            pltpu.VMEM((SSD_CHUNK, d_inner), F32),
        ],
        compiler_params=_cparams(("arbitrary", "arbitrary")),
        name="mamba_prompt",
    )(x, nw, win, cw, cb, dtb, alog, dsk, gnw, wout, hexp)


def _mamba_weights(w_in, conv_w, conv_b, dt_bias, a_log, d_skip, norm_w, w_out):
    d_inner = w_out.shape[0]
    n_heads = dt_bias.shape[0]
    pad = LANES - n_heads
    win = jnp.pad(w_in, ((0, 0), (0, pad))).astype(BF16)
    dtb = jnp.pad(dt_bias, (0, pad)).reshape(1, LANES)
    alog = jnp.pad(a_log, (0, pad)).reshape(1, LANES)
    dsk = jnp.repeat(d_skip, SSM_HEAD_DIM).reshape(1, d_inner)
    return (win, conv_w, conv_b.reshape(1, -1), dtb, alog, dsk, norm_w.reshape(1, d_inner),
            w_out.astype(BF16))


def _sink_softmax_pv(s, sink, v):
    m = jnp.maximum(jnp.max(s, axis=-1, keepdims=True), sink)
    p = jnp.exp(s - m)
    denom = jnp.sum(p, axis=-1, keepdims=True) + jnp.exp(sink - m)
    return _bdot(p, v) / denom


def _attn_prompt_body(sink_ref, x_ref, nw_ref, wqkv_ref, bqkv_ref, wo_ref, bo_ref,
                      out_ref, kwin_ref, vwin_ref, kv_buf, q_buf, o_buf):
    blk = WINDOW
    hd = HEAD_DIM
    nq = N_Q_HEADS * hd
    nk = N_KV_HEADS * hd
    qpk = N_Q_HEADS // N_KV_HEADS
    tq = x_ref.shape[0]
    s_id = pl.program_id(1)

    @pl.when(s_id == 0)
    def _():
        kv_buf[0:blk, :] = jnp.zeros((blk, 2 * nk), F32)

    h = _rms(x_ref[...], nw_ref[...]).astype(BF16)
    q_buf[...] = jnp.dot(h, wqkv_ref[:, 0:nq], preferred_element_type=F32) + bqkv_ref[:, 0:nq]
    kv_buf[blk:blk + tq, :] = jnp.dot(h, wqkv_ref[:, nq:], preferred_element_type=F32) + bqkv_ref[:, nq:]
    kwin_ref[0] = kv_buf[tq:tq + blk, 0:nk]
    vwin_ref[0] = kv_buf[tq:tq + blk, nk:]

    row = lax.broadcasted_iota(I32, (blk, 2 * blk), 0)
    col = lax.broadcasted_iota(I32, (blk, 2 * blk), 1)
    diff = row + blk - col
    band = (diff >= 0) & (diff <= WINDOW)
    scale = hd ** -0.5
    for qb in range(tq // blk):
        qrows = slice(qb * blk, (qb + 1) * blk)
        krows = slice(qb * blk, (qb + 2) * blk)
        ok = band & ((col >= blk) | (s_id > 0)) if qb == 0 else band
        for g in range(N_KV_HEADS):
            k_g = kv_buf[krows, g * hd:(g + 1) * hd]
            v_g = kv_buf[krows, nk + g * hd:nk + (g + 1) * hd]
            for j in range(qpk):
                hh = g * qpk + j
                s = _bdot_nt(q_buf[qrows, hh * hd:(hh + 1) * hd], k_g) * scale
                s = jnp.where(ok, s, -jnp.inf)
                o_buf[qrows, hh * hd:(hh + 1) * hd] = _sink_softmax_pv(s, sink_ref[hh], v_g)
    kv_buf[0:blk, :] = kv_buf[tq:tq + blk, :]
    out_ref[...] = (x_ref[...] + jnp.dot(o_buf[...].astype(BF16), wo_ref[...], preferred_element_type=F32)
                    + bo_ref[...])


def _attn_prompt(x, sinks, nw, wqkv, bqkv, wo, bo, *, bsz, seq, tq):
    d = x.shape[1]
    blk = WINDOW
    nb = seq // tq
    nk = N_KV_HEADS * HEAD_DIM
    nq = N_Q_HEADS * HEAD_DIM
    return pl.pallas_call(
        _attn_prompt_body,
        grid=(bsz, nb),
        in_specs=[
            pl.BlockSpec(memory_space=pltpu.SMEM),
            pl.BlockSpec((tq, d), lambda b, s: (b * nb + s, 0)),
            _full(nw.shape), _full(wqkv.shape), _full(bqkv.shape), _full(wo.shape), _full(bo.shape),
        ],
        out_specs=[
            pl.BlockSpec((tq, d), lambda b, s: (b * nb + s, 0)),
            pl.BlockSpec((1, blk, nk), lambda b, s: (b, 0, 0)),
            pl.BlockSpec((1, blk, nk), lambda b, s: (b, 0, 0)),
        ],
        out_shape=[
            jax.ShapeDtypeStruct((bsz * seq, d), F32),
            jax.ShapeDtypeStruct((bsz, blk, nk), F32),
            jax.ShapeDtypeStruct((bsz, blk, nk), F32),
        ],
        scratch_shapes=[
            pltpu.VMEM((blk + tq, 2 * nk), F32),
            pltpu.VMEM((tq, nq), F32),
            pltpu.VMEM((tq, nq), F32),
        ],
        compiler_params=_cparams(("arbitrary", "arbitrary")),
        name="attn_prompt",
    )(sinks, x, nw, wqkv, bqkv, wo, bo)


def _x_specs(xm, tm):
    ntm, d = xm.shape[0] // tm, xm.shape[1]
    return [pl.BlockSpec((tm, d), lambda i: (jnp.minimum(i, ntm - 1), 0)), pl.BlockSpec((tm, d), lambda i: (0, 0))]


def _x_tile(xm_ref, xt_ref, ntm):
    return jnp.where(pl.program_id(0) < ntm, xm_ref[...], xt_ref[...])


def _route_body(xm_ref, xt_ref, nw_ref, wr_ref, br_ref, info_ref, cnt_ref, *, ntm):
    tm = xm_ref.shape[0]
    h = _rms(_x_tile(xm_ref, xt_ref, ntm), nw_ref[...])
    h_hi = h.astype(BF16)
    h_lo = (h - h_hi.astype(F32)).astype(BF16)
    part = jnp.dot(h_hi, wr_ref[...], preferred_element_type=F32)
    logits = (part[:, 0:LANES] + part[:, LANES:] + jnp.dot(h_lo, wr_ref[:, 0:LANES], preferred_element_type=F32)
              + br_ref[...])
    lane_i = lax.broadcasted_iota(I32, (tm, LANES), 1)
    lane = lane_i.astype(F32)
    lane_grp = (lane_i // EXPERTS_PER_GROUP).astype(F32)
    big = float(LANES)
    ninf = -jnp.inf

    def first_argmax(v):
        m = jnp.max(v, axis=-1, keepdims=True)
        return m, jnp.min(jnp.where(v == m, lane, big), axis=-1, keepdims=True)

    gmask = (lane_i >= N_EXPERTS) & (lane_i < N_EXPERTS + N_EXPERT_GROUPS)
    gl = jnp.where(gmask, logits, ninf)
    gmax, gi = first_argmax(gl)
    gi = gi - float(N_EXPERTS)
    pg = 1.0 / jnp.sum(jnp.exp(gl - gmax), axis=-1, keepdims=True)
    emask = (lane_i < N_EXPERTS) & (lane_grp == gi)
    el = jnp.where(emask, logits, ninf)
    m1, i1 = first_argmax(el)
    el2 = jnp.where(lane == i1, ninf, el)
    m2, i2 = first_argmax(el2)
    den = jnp.sum(jnp.exp(el - m1), axis=-1, keepdims=True)
    tp1 = 1.0 / den
    tp2 = jnp.exp(m2 - m1) / den
    g1 = pg * tp1 / (tp1 + tp2)
    g2 = pg * tp2 / (tp1 + tp2)
    hot1 = lane == i1
    hot2 = lane == i2
    onehot = jnp.where(hot1 | hot2, 1.0, 0.0)
    rr = lax.broadcasted_iota(I32, (tm, tm), 0)
    cc = lax.broadcasted_iota(I32, (tm, tm), 1)
    before = jnp.where(rr > cc, 1.0, 0.0)
    cum = _bdot(before, onehot)
    r1 = jnp.sum(jnp.where(hot1, cum, 0.0), axis=-1, keepdims=True)
    r2 = jnp.sum(jnp.where(hot2, cum, 0.0), axis=-1, keepdims=True)
    cnt_row = jnp.sum(onehot, axis=0, keepdims=True)
    nwin_row = jnp.floor((cnt_row + (SEG_W - 1.0)) * (1.0 / SEG_W))
    er = lax.broadcasted_iota(I32, (LANES, LANES), 0)
    ec = lax.broadcasted_iota(I32, (LANES, LANES), 1)
    earlier = jnp.where(er < ec, 1.0, 0.0)
    both = jnp.concatenate([jnp.broadcast_to(cnt_row, (8, LANES)), jnp.broadcast_to(nwin_row, (8, LANES))], axis=0)
    pre = _dot01_right(both, earlier)
    start = pre[0:1, :]
    start_al = pre[8:9, :] * float(SEG_W)

    def at(hot, row):
        return jnp.sum(jnp.where(hot, row, 0.0), axis=-1, keepdims=True)

    info = jnp.zeros((tm, LANES), F32)
    for k, v in enumerate((g1, g2, i1, i2, at(hot1, start) + r1, at(hot2, start) + r2,
                           at(hot1, start_al) + r1, at(hot2, start_al) + r2)):
        info = jnp.where(lane_i == k, v, info)
    info_ref[...] = info.T[0:8, :]
    cnt_ref[0] = jnp.broadcast_to(cnt_row, (8, LANES))


ROUTE_ROWS = 8


def _route(xm, xt, nw, wr, br, *, tm):
    ntm = xm.shape[0] // tm
    nt = ntm + 1
    return pl.pallas_call(
        functools.partial(_route_body, ntm=ntm),
        grid=(nt,),
        in_specs=_x_specs(xm, tm) + [_full(nw.shape), _full(wr.shape), _full(br.shape)],
        out_specs=[pl.BlockSpec((ROUTE_ROWS, tm), lambda i: (0, i)), pl.BlockSpec((1, 8, LANES), lambda i: (i, 0, 0))],
        out_shape=[jax.ShapeDtypeStruct((ROUTE_ROWS, nt * tm), F32), jax.ShapeDtypeStruct((nt, 8, LANES), F32)],
        compiler_params=_cparams(("arbitrary",)),
        name="moe_route",
    )(xm, xt, nw, wr, br)


def _to_tiles(ref, base, val):
    m, rt = val.shape[0], val.shape[1] // LANES
    for j in range(rt):
        ref[pl.ds(base * rt + j, m, stride=rt), :] = val[:, j * LANES:(j + 1) * LANES]


def _from_tiles(ref, base, m, rt):
    return jnp.concatenate([ref[pl.ds(base * rt + j, m, stride=rt), :] for j in range(rt)], axis=1)


SEG_W = 16
WIN_HDR = 2


def _max_windows(tm):
    return N_EXPERTS + TOP_K * tm // SEG_W


def _seg_copy(src, i, dst, j, sem, rt):
    n = SEG_W * rt
    return pltpu.make_async_copy(src.at[pl.ds(pl.multiple_of(i * rt, rt), n), :],
                                 dst.at[pl.ds(pl.multiple_of(j * rt, rt), n), :], sem)


def _tok(ref, p, rt):
    return ref.at[pl.ds(pl.multiple_of(p * rt, rt), rt), :]


def _dispatch_body(lpos_ref, win_ref, zwin_ref, xm_ref, xt_ref, nw_ref, xb_ref, h_buf, s_buf, sem, *, ntm):
    tm, rt = xm_ref.shape[0], xm_ref.shape[1] // LANES
    i = pl.program_id(0)
    half = TOP_K * tm + SEG_W
    sbase = (i % 2) * half
    mw = _max_windows(tm)

    @pl.when(i == 0)
    def _():
        for hb in range(2):
            s_buf[(hb * half + TOP_K * tm) * rt:(hb + 1) * half * rt, :] = jnp.zeros((SEG_W * rt, LANES), F32)

        def zissue(w, carry):
            _seg_copy(s_buf, TOP_K * tm, xb_ref, zwin_ref[1 + w], sem, rt).start()
            return carry

        def zdrain(w, carry):
            _seg_copy(s_buf, 0, xb_ref, 0, sem, rt).wait()
            return carry

        lax.fori_loop(0, zwin_ref[0], zissue, 0)
        lax.fori_loop(0, zwin_ref[0], zdrain, 0)

    _to_tiles(h_buf, 0, _rms(_x_tile(xm_ref, xt_ref, ntm), nw_ref[...]))

    def move(t, carry):
        v = _tok(h_buf, t, rt)[...]
        for k in range(TOP_K):
            _tok(s_buf, sbase + lpos_ref[0, 0, k * tm + t], rt)[...] = v
        return carry

    lax.fori_loop(0, tm, move, 0, unroll=8)

    def drain(w, carry):
        _seg_copy(s_buf, 0, xb_ref, 0, sem, rt).wait()
        return carry

    @pl.when(i > 0)
    def _():
        lax.fori_loop(0, win_ref[0, 0, 1], drain, 0)

    def issue(w, carry):
        _seg_copy(s_buf, sbase + win_ref[0, 0, WIN_HDR + w], xb_ref, win_ref[0, 0, WIN_HDR + mw + w], sem, rt).start()
        return carry

    lax.fori_loop(0, win_ref[0, 0, 0], issue, 0)

    @pl.when(i == pl.num_programs(0) - 1)
    def _():
        lax.fori_loop(0, win_ref[0, 0, 0], drain, 0)


def _dispatch(xm, xt, nw, lpos, win, zwin, *, tm, n_slots):
    d = xm.shape[1]
    ntm = xm.shape[0] // tm
    nt = ntm + 1
    rt = d // LANES
    return pl.pallas_call(
        functools.partial(_dispatch_body, ntm=ntm),
        grid=(nt,),
        in_specs=[
            pl.BlockSpec((1, 1, lpos.shape[2]), lambda i: (i, 0, 0), memory_space=pltpu.SMEM),
            pl.BlockSpec((1, 1, win.shape[2]), lambda i: (i, 0, 0), memory_space=pltpu.SMEM),
            pl.BlockSpec(memory_space=pltpu.SMEM),
        ] + _x_specs(xm, tm) + [
            _full(nw.shape),
        ],
        out_specs=pl.BlockSpec(memory_space=pl.ANY),
        out_shape=jax.ShapeDtypeStruct((n_slots * rt, LANES), F32),
        scratch_shapes=[pltpu.VMEM((tm * rt, LANES), F32),
                        pltpu.VMEM((2 * (TOP_K * tm + SEG_W) * rt, LANES), F32),
                        pltpu.SemaphoreType.DMA(())],
        compiler_params=_cparams(("arbitrary",)),
        name="moe_dispatch",
    )(lpos, win, zwin, xm, xt, nw)


def _expert_body(be_ref, nu_ref, xb_ref, wg_ref, wu_ref, wd_ref, yb_ref, wg_buf, wu_buf, wd_buf):
    b = pl.program_id(0)
    prev = be_ref[jnp.maximum(b - 1, 0)]
    fresh = (b == 0) | (be_ref[b] != prev)

    @pl.when((b < nu_ref[0]) & fresh)
    def _():
        wg_buf[...] = wg_ref[0, 0].astype(BF16)
        wu_buf[...] = wu_ref[0, 0].astype(BF16)
        wd_buf[...] = wd_ref[0, 0].astype(BF16)

    @pl.when(b < nu_ref[0])
    def _():
        xb = _from_tiles(xb_ref, 0, MOE_BLOCK, wg_buf.shape[0] // LANES).astype(BF16)
        gate = jnp.dot(xb, wg_buf[...], preferred_element_type=F32)
        up = jnp.dot(xb, wu_buf[...], preferred_element_type=F32)
        hid = (_silu(gate) * up).astype(BF16)
        _to_tiles(yb_ref, 0, jnp.dot(hid, wd_buf[...], preferred_element_type=F32))

    @pl.when(b >= nu_ref[0])
    def _():
        yb_ref[...] = jnp.zeros_like(yb_ref)


def _experts(blk_exp, n_used, xb, wg, wu, wd, *, layer):
    d, f = wg.shape[2], wg.shape[3]
    rt = d // LANES
    nb = xb.shape[0] // rt // MOE_BLOCK
    blk_rows = MOE_BLOCK * rt

    def xmap(b, be, nu):
        return (jnp.minimum(b, jnp.maximum(nu[0] - 1, 0)), 0)

    def wmap(b, be, nu):
        return (layer, be[b], 0, 0)

    return pl.pallas_call(
        _expert_body,
        grid_spec=pltpu.PrefetchScalarGridSpec(
            num_scalar_prefetch=2,
            grid=(nb,),
            in_specs=[
                pl.BlockSpec((blk_rows, LANES), xmap),
                pl.BlockSpec((1, 1, d, f), wmap), pl.BlockSpec((1, 1, d, f), wmap),
                pl.BlockSpec((1, 1, f, d), wmap),
            ],
            out_specs=pl.BlockSpec((blk_rows, LANES), lambda b, be, nu: (b, 0)),
            scratch_shapes=[pltpu.VMEM((d, f), BF16), pltpu.VMEM((d, f), BF16), pltpu.VMEM((f, d), BF16)],
        ),
        out_shape=jax.ShapeDtypeStruct(xb.shape, F32),
        compiler_params=_cparams(("arbitrary",)),
        name="moe_experts",
    )(blk_exp, n_used, xb, wg, wu, wd)


def _ybuf_tokens(tm):
    return TOP_K * tm + N_EXPERTS * (SEG_W - 1) + SEG_W


def _combine_body(lpos_ref, gate_ref, win_ref, winn_ref, xm_ref, xt_ref, fw_ref, yb_ref, om_ref, ot_ref,
                  y_buf, x_buf, sem, *, ntm, final_norm):
    tm, rt = xm_ref.shape[0], xm_ref.shape[1] // LANES
    i = pl.program_id(0)
    slot = i % 2
    half = _ybuf_tokens(tm)
    mw = _max_windows(tm)

    def fetch(tab_ref, sl):
        def issue(w, carry):
            _seg_copy(yb_ref, tab_ref[0, 0, WIN_HDR + w], y_buf, sl * half + tab_ref[0, 0, WIN_HDR + mw + w],
                      sem.at[sl], rt).start()
            return carry

        lax.fori_loop(0, tab_ref[0, 0, 0], issue, 0)

    @pl.when(i == 0)
    def _():
        fetch(win_ref, 0)

    @pl.when(i + 1 < pl.num_programs(0))
    def _():
        fetch(winn_ref, 1 - slot)

    _to_tiles(x_buf, 0, _x_tile(xm_ref, xt_ref, ntm))

    def drain(w, carry):
        _seg_copy(yb_ref, 0, y_buf, 0, sem.at[slot], rt).wait()
        return carry

    lax.fori_loop(0, win_ref[0, 0, 0], drain, 0)
    ybase = slot * half

    def comb(t, carry):
        acc = _tok(x_buf, t, rt)[...]
        for k in range(TOP_K):
            a = k * tm + t
            acc = acc + gate_ref[0, 0, a] * _tok(y_buf, ybase + lpos_ref[0, 0, a], rt)[...]
        _tok(x_buf, t, rt)[...] = acc
        return carry

    lax.fori_loop(0, tm, comb, 0, unroll=8)
    out = _from_tiles(x_buf, 0, tm, rt)
    if final_norm:
        out = _rms(out, fw_ref[...])

    @pl.when(i < ntm)
    def _():
        om_ref[...] = out

    @pl.when(i == ntm)
    def _():
        ot_ref[...] = out


def _combine(xm, xt, lpos, gates, win, yb, fw, *, tm, final_norm):
    d = xm.shape[1]
    ntm = xm.shape[0] // tm
    nt = ntm + 1
    rt = d // LANES

    def smem(arr, imap):
        return pl.BlockSpec((1, 1, arr.shape[2]), imap, memory_space=pltpu.SMEM)

    return pl.pallas_call(
        functools.partial(_combine_body, ntm=ntm, final_norm=final_norm),
        grid=(nt,),
        in_specs=[
            smem(lpos, lambda i: (i, 0, 0)),
            smem(gates, lambda i: (i, 0, 0)),
            smem(win, lambda i: (i, 0, 0)),
            smem(win, lambda i: (jnp.minimum(i + 1, nt - 1), 0, 0)),
        ] + _x_specs(xm, tm) + [
            _full(fw.shape),
            pl.BlockSpec(memory_space=pl.ANY),
        ],
        out_specs=_x_specs(xm, tm),
        out_shape=[jax.ShapeDtypeStruct(xm.shape, F32), jax.ShapeDtypeStruct(xt.shape, F32)],
        scratch_shapes=[pltpu.VMEM((2 * _ybuf_tokens(tm) * rt, LANES), F32),
                        pltpu.VMEM((tm * rt, LANES), F32),
                        pltpu.SemaphoreType.DMA((2,))],
        compiler_params=_cparams(("arbitrary",)),
        name="moe_combine",
    )(lpos, gates, win, win, xm, xt, fw, yb)


def _moe(xm, xt, nw, w_group, b_group, w_expert, b_expert, wg, wu, wd, fw, *, layer, final_norm):
    tm, d = xt.shape
    nt = xm.shape[0] // tm + 1
    t = nt * tm
    pad = LANES - N_EXPERTS - N_EXPERT_GROUPS
    wr = jnp.pad(jnp.concatenate([w_expert, w_group], axis=1), ((0, 0), (0, pad)))
    wr_hi = wr.astype(BF16)
    wr = jnp.concatenate([wr_hi, (wr - wr_hi.astype(F32)).astype(BF16)], axis=1)
    br = jnp.pad(jnp.concatenate([b_expert, b_group]), (0, pad)).reshape(1, LANES)
    nw2 = nw.reshape(1, d)
    info, cnt = _route(xm, xt, nw2, wr, br, tm=tm)

    def per_tile(rows):
        return rows.reshape(TOP_K, nt, tm).transpose(1, 0, 2).reshape(nt, 1, TOP_K * tm)

    gates = per_tile(info[0:TOP_K])
    lpos = per_tile(info[4:4 + TOP_K].astype(I32))
    lpos_al = per_tile(info[6:6 + TOP_K].astype(I32))
    cnt = cnt[:, 0, :N_EXPERTS].astype(I32)
    total = jnp.sum(cnt, axis=0)
    padded = jnp.where(total > 0, (total + SEG_W + MOE_BLOCK - 2) // MOE_BLOCK * MOE_BLOCK, 0)
    pend = jnp.cumsum(padded)
    pstart = pend - padded
    gstart = pstart[None, :] + jnp.cumsum(cnt, axis=0) - cnt
    lstart = jnp.cumsum(cnt, axis=1) - cnt
    nwin_e = (cnt + SEG_W - 1) // SEG_W
    lstart_al = (jnp.cumsum(nwin_e, axis=1) - nwin_e) * SEG_W

    winc = jnp.cumsum(nwin_e, axis=1)
    nwin = winc[:, -1:]
    mw = _max_windows(tm)
    j = jnp.arange(mw, dtype=I32)[None, :]
    owner = (jnp.sum(winc[:, None, :] <= j[:, :, None], axis=-1)[:, :, None]
             == jnp.arange(N_EXPERTS, dtype=I32)[None, None, :])
    pick = lambda tab: jnp.sum(jnp.where(owner, tab[:, None, :], 0), axis=-1)
    w_off = (j - pick(winc - nwin_e)) * SEG_W
    live = j < nwin
    src_loc = jnp.where(live, pick(lstart) + w_off, 0)
    slot_g = jnp.where(live, pick(gstart) + w_off, 0)
    dst_loc = jnp.where(live, pick(lstart_al) + w_off, 0)
    nprev = jnp.concatenate([jnp.zeros((1, 1), I32), nwin[:-1]], axis=0)
    win_d = jnp.concatenate([nwin, nprev, src_loc, slot_g], axis=1).reshape(nt, 1, WIN_HDR + 2 * mw)
    win_c = jnp.concatenate([nwin, nprev, slot_g, dst_loc], axis=1).reshape(nt, 1, WIN_HDR + 2 * mw)
    n_blocks = -(-(t * TOP_K + N_EXPERTS * (MOE_BLOCK + SEG_W - 2)) // MOE_BLOCK)
    n_slots = n_blocks * MOE_BLOCK
    zfirst = jnp.concatenate([pstart + total // SEG_W * SEG_W, pend[-1:]])
    zend = jnp.concatenate([pend, jnp.full((1,), n_slots, I32)])
    nz_e = (zend - zfirst) // SEG_W
    zinc = jnp.cumsum(nz_e)
    mz = N_EXPERTS * ((MOE_BLOCK + 2 * SEG_W) // SEG_W) + n_slots // SEG_W - TOP_K * t // SEG_W
    jz = jnp.arange(mz, dtype=I32)
    zowner = (jnp.sum(zinc[None, :] <= jz[:, None], axis=-1)[:, None]
              == jnp.arange(N_EXPERTS + 1, dtype=I32)[None, :])
    zpick = lambda tab: jnp.sum(jnp.where(zowner, tab[None, :], 0), axis=-1)
    zslot = jnp.where(jz < zinc[-1], zpick(zfirst) + (jz - zpick(zinc - nz_e)) * SEG_W, 0)
    zwin = jnp.concatenate([zinc[-1:], zslot]).astype(I32)
    blk_exp = jnp.minimum(jnp.sum(pend[None, :] <= (jnp.arange(n_blocks, dtype=I32) * MOE_BLOCK)[:, None], axis=1),
                          N_EXPERTS - 1).astype(I32)
    n_used = (pend[-1] // MOE_BLOCK).astype(I32).reshape(1)

    xb = _dispatch(xm, xt, nw2, lpos, win_d, zwin, tm=tm, n_slots=n_slots)
    yb = _experts(blk_exp, n_used, xb, wg, wu, wd, layer=layer)
    return _combine(xm, xt, lpos_al, gates, win_c, yb, fw.reshape(1, d), tm=tm, final_norm=final_norm)


def _linear_body(x_ref, nw_ref, w_ref, b_ref, r_ref, out_ref, *, norm):
    x = x_ref[...]
    if norm:
        x = _rms(x, nw_ref[...])
    out_ref[...] = _bdot(x, w_ref[...]) + b_ref[...] + r_ref[...]


def _linear(x, w, *, nw=None, bias=None, res=None, tn):
    m, kd = x.shape
    n = w.shape[1]
    norm = nw is not None
    nw = jnp.ones((1, kd), F32) if nw is None else nw
    bias = jnp.zeros((1, n), F32) if bias is None else bias
    res = jnp.zeros((m, n), F32) if res is None else res
    tn = min(tn, n)
    return pl.pallas_call(
        functools.partial(_linear_body, norm=norm),
        grid=(n // tn,),
        in_specs=[
            _full(x.shape), _full(nw.shape),
            pl.BlockSpec((kd, tn), lambda j: (0, j)),
            pl.BlockSpec((1, tn), lambda j: (0, j)),
            pl.BlockSpec((m, tn), lambda j: (0, j)),
        ],
        out_specs=pl.BlockSpec((m, tn), lambda j: (0, j)),
        out_shape=jax.ShapeDtypeStruct((m, n), F32),
        compiler_params=_cparams(("arbitrary",)),
        name="sample_linear",
    )(x, nw, w, bias, res)


def _sample_conv_body(xbc_ref, st_ref, cw_ref, cb_ref, dtr_ref, dtb_ref, xc_ref, stn_ref, dt_ref, cbg_ref,
                      *, d_inner):
    nst = D_STATE
    xbc = xbc_ref[...]
    acc = cb_ref[...] + cw_ref[3:4, :] * xbc
    for k in range(CONV_W - 1):
        acc = acc + cw_ref[k:k + 1, :] * st_ref[k]
    xc = _silu(acc)
    xc_ref[...] = xc
    stn_ref[0] = st_ref[1]
    stn_ref[1] = st_ref[2]
    stn_ref[2] = xbc
    dt_ref[...] = _softplus(dtr_ref[...] + dtb_ref[...])
    lane = lax.broadcasted_iota(I32, (xbc.shape[0], LANES), 1)
    cbg = jnp.zeros((xbc.shape[0], LANES), F32)
    for g in range(N_BC_GROUPS):
        b_g = xc[:, d_inner + g * nst:d_inner + (g + 1) * nst]
        c_g = xc[:, d_inner + (N_BC_GROUPS + g) * nst:d_inner + (N_BC_GROUPS + g + 1) * nst]
        cbg = jnp.where(lane == g, jnp.sum(b_g * c_g, axis=-1, keepdims=True), cbg)
    cbg_ref[...] = cbg


def _sample_conv(xbc, st_t, cw, cb, dtr, dtb, *, d_inner):
    m, cd = xbc.shape
    return pl.pallas_call(
        functools.partial(_sample_conv_body, d_inner=d_inner),
        grid=(1,),
        in_specs=[_full(xbc.shape), _full(st_t.shape), _full(cw.shape), _full(cb.shape), _full(dtr.shape),
                  _full(dtb.shape)],
        out_specs=[_full((m, cd)), _full(st_t.shape), _full((m, LANES)), _full((m, LANES))],
        out_shape=[jax.ShapeDtypeStruct((m, cd), F32), jax.ShapeDtypeStruct(st_t.shape, F32),
                   jax.ShapeDtypeStruct((m, LANES), F32), jax.ShapeDtypeStruct((m, LANES), F32)],
        compiler_params=_cparams(("arbitrary",)),
        name="sample_conv",
    )(xbc, st_t, cw, cb, dtr, dtb)


SSD_REQS_PER_STEP = 4


def _sample_ssd_body(s0_ref, xt_ref, bc_ref, hs_ref, par_ref, sn_ref, yt_ref, *, n_heads):
    hp = SSM_HEAD_DIM
    hpg = n_heads // N_BC_GROUPS
    a = -jnp.exp(par_ref[0:1, :])
    dsk = par_ref[1:2, :]
    head_row = lax.broadcasted_iota(I32, (n_heads, s0_ref.shape[3]), 0)
    for r in range(s0_ref.shape[0]):
        xt = xt_ref[r]
        dt = hs_ref[r, 0:1, :]
        cbh = hs_ref[r, 1:2, :]
        dec = jnp.exp(dt * a)
        xdt = xt * dt
        xdt_b = xdt.astype(BF16)
        yoff = jnp.zeros((hp, n_heads), F32)
        for hh in range(n_heads):
            g = hh // hpg
            b_row = bc_ref[r, g:g + 1, :]
            c_row = bc_ref[r, N_BC_GROUPS + g:N_BC_GROUPS + g + 1, :]
            s0 = s0_ref[r, hh]
            yoff = yoff + _bdot_nt(s0, jnp.where(head_row == hh, c_row, 0.0))
            b_sel = jnp.where(head_row == hh, b_row, 0.0).astype(BF16)
            sn_ref[r, hh] = s0 * dec[:, hh:hh + 1] + jnp.dot(xdt_b, b_sel, preferred_element_type=F32)
        yt_ref[r] = cbh * xdt + yoff * dec + xt * dsk


def _sample_ssd(s0, xt, bc, hs, par):
    bsz, n_heads, hp, nst = s0.shape
    rb = SSD_REQS_PER_STEP if bsz % SSD_REQS_PER_STEP == 0 else 1
    return pl.pallas_call(
        functools.partial(_sample_ssd_body, n_heads=n_heads),
        grid=(bsz // rb,),
        in_specs=[
            pl.BlockSpec((rb, n_heads, hp, nst), lambda b: (b, 0, 0, 0)),
            pl.BlockSpec((rb, hp, n_heads), lambda b: (b, 0, 0)),
            pl.BlockSpec((rb,) + bc.shape[1:], lambda b: (b, 0, 0)),
            pl.BlockSpec((rb,) + hs.shape[1:], lambda b: (b, 0, 0)),
            _full(par.shape),
        ],
        out_specs=[
            pl.BlockSpec((rb, n_heads, hp, nst), lambda b: (b, 0, 0, 0)),
            pl.BlockSpec((rb, hp, n_heads), lambda b: (b, 0, 0)),
        ],
        out_shape=[jax.ShapeDtypeStruct(s0.shape, F32), jax.ShapeDtypeStruct((bsz, hp, n_heads), F32)],
        compiler_params=_cparams(("arbitrary",)),
        name="sample_ssd",
    )(s0, xt, bc, hs, par)


def _sample_gnorm_out_body(y_ref, z_ref, gnw_ref, wout_ref, x_ref, out_ref, *, d_inner):
    gw = d_inner // N_BC_GROUPS
    acc = x_ref[...]
    for g in range(N_BC_GROUPS):
        glanes = slice(g * gw, (g + 1) * gw)
        gg = y_ref[:, glanes] * _silu(z_ref[:, glanes])
        gg = gg * lax.rsqrt(jnp.mean(gg * gg, axis=-1, keepdims=True) + EPS) * gnw_ref[:, glanes]
        acc = acc + jnp.dot(gg.astype(BF16), wout_ref[glanes, :], preferred_element_type=F32)
    out_ref[...] = acc


def _sample_gnorm_out(y, z, gnw, wout, x):
    d_inner = y.shape[1]
    return pl.pallas_call(
        functools.partial(_sample_gnorm_out_body, d_inner=d_inner),
        grid=(1,),
        in_specs=[_full(y.shape), _full(z.shape), _full(gnw.shape), _full(wout.shape), _full(x.shape)],
        out_specs=_full(x.shape),
        out_shape=jax.ShapeDtypeStruct(x.shape, F32),
        compiler_params=_cparams(("arbitrary",)),
        name="sample_gnorm_out",
    )(y, z, gnw, wout, x)


def _sample_attn_body(q_ref, kn_ref, vn_ref, kc_ref, vc_ref, sink_ref, o_ref, ko_ref, vo_ref, s_buf, sn_buf):
    bt = q_ref.shape[0]
    wb = kc_ref.shape[1]
    hd = HEAD_DIM
    nh = N_Q_HEADS
    qpk = N_Q_HEADS // N_KV_HEADS
    scale = hd ** -0.5
    for b in range(bt):
        kn = kn_ref[b]
        kn_h = jnp.concatenate([jnp.broadcast_to(kn[:, g * hd:(g + 1) * hd], (qpk, hd))
                                for g in range(N_KV_HEADS)], axis=0)
        sn_buf[b * nh:(b + 1) * nh, :] = jnp.sum(q_ref[b] * kn_h, axis=-1, keepdims=True) * scale
        for g in range(N_KV_HEADS):
            rows = slice(b * nh + g * qpk, b * nh + (g + 1) * qpk)
            s_buf[rows, :] = _bdot_nt(q_ref[b, g * qpk:(g + 1) * qpk, :], kc_ref[b, :, g * hd:(g + 1) * hd]) * scale
    s = s_buf[...]
    s_new = sn_buf[...]
    sink = sink_ref[...]
    m = jnp.maximum(jnp.maximum(jnp.max(s, axis=-1, keepdims=True), s_new), sink)
    p = jnp.exp(s - m)
    p_new = jnp.exp(s_new - m)
    inv = 1.0 / (jnp.sum(p, axis=-1, keepdims=True) + p_new + jnp.exp(sink - m))
    s_buf[...] = p * inv
    sn_buf[...] = p_new * inv
    for b in range(bt):
        vn = vn_ref[b]
        for g in range(N_KV_HEADS):
            rows = slice(b * nh + g * qpk, b * nh + (g + 1) * qpk)
            cols = slice(g * hd, (g + 1) * hd)
            o_ref[b, g * qpk:(g + 1) * qpk, :] = (_bdot(s_buf[rows, :], vc_ref[b, :, cols])
                                                   + sn_buf[rows, :] * vn[:, cols])
        ko_ref[b, 0:wb - 1, :] = kc_ref[b, 1:wb, :]
        ko_ref[b, wb - 1:wb, :] = kn_ref[b]
        vo_ref[b, 0:wb - 1, :] = vc_ref[b, 1:wb, :]
        vo_ref[b, wb - 1:wb, :] = vn


def _sample_attn(q3, kn, vn, kc, vc, sinks, *, bt):
    bsz, nqh, hd = q3.shape
    wb, nk = kc.shape[1], kc.shape[2]
    return pl.pallas_call(
        _sample_attn_body,
        grid=(bsz // bt,),
        in_specs=[
            pl.BlockSpec((bt, nqh, hd), lambda i: (i, 0, 0)),
            pl.BlockSpec((bt, 1, nk), lambda i: (i, 0, 0)),
            pl.BlockSpec((bt, 1, nk), lambda i: (i, 0, 0)),
            pl.BlockSpec((bt, wb, nk), lambda i: (i, 0, 0)),
            pl.BlockSpec((bt, wb, nk), lambda i: (i, 0, 0)),
            _full(sinks.shape),
        ],
        out_specs=[
            pl.BlockSpec((bt, nqh, hd), lambda i: (i, 0, 0)),
            pl.BlockSpec((bt, wb, nk), lambda i: (i, 0, 0)),
            pl.BlockSpec((bt, wb, nk), lambda i: (i, 0, 0)),
        ],
        out_shape=[jax.ShapeDtypeStruct(q3.shape, F32), jax.ShapeDtypeStruct(kc.shape, F32),
                   jax.ShapeDtypeStruct(vc.shape, F32)],
        scratch_shapes=[pltpu.VMEM((bt * nqh, wb), F32), pltpu.VMEM((bt * nqh, 1), F32)],
        compiler_params=_cparams(("arbitrary",)),
        name="sample_attn",
    )(q3, kn, vn, kc, vc, sinks)


def _mamba_sample(x, nw, mw, state_conv, state_ssm):
    win, cw, cb, dtb, alog, dsk, gnw, wout = mw
    bsz, d = x.shape
    d_inner = wout.shape[0]
    conv_dim = cw.shape[1]
    n_heads = d_inner // SSM_HEAD_DIM
    hp = SSM_HEAD_DIM
    proj = _linear(x, win, nw=nw, tn=896)
    z = proj[:, :d_inner]
    xbc = proj[:, d_inner:d_inner + conv_dim]
    dtr = proj[:, d_inner + conv_dim:]
    st_t = jnp.transpose(state_conv, (1, 0, 2))
    xc, stn_t, dt, cbg = _sample_conv(xbc, st_t, cw, cb, dtr, dtb, d_inner=d_inner)
    conv_new = jnp.transpose(stn_t, (1, 0, 2))
    xt = jnp.transpose(xc[:, :d_inner].reshape(bsz, n_heads, hp), (0, 2, 1))
    bc = xc[:, d_inner:].reshape(bsz, 2 * N_BC_GROUPS, D_STATE)
    cbh = jnp.repeat(cbg[:, :N_BC_GROUPS], n_heads // N_BC_GROUPS, axis=1)
    hs = jnp.stack([dt[:, :n_heads], cbh], axis=1)
    par = jnp.stack([alog[0, :n_heads], dsk.reshape(n_heads, hp)[:, 0]], axis=0)
    ssm_new, yt = _sample_ssd(state_ssm, xt, bc, hs, par)
    y = jnp.transpose(yt, (0, 2, 1)).reshape(bsz, d_inner)
    out = _sample_gnorm_out(y, z, gnw, wout, x)
    return out, conv_new, ssm_new


def _attn_sample(x, nw, wqkv, bqkv, sinks, wo, bo, cache_k, cache_v):
    bsz, d = x.shape
    wb = cache_k.shape[1]
    nq = N_Q_HEADS * HEAD_DIM
    nk = N_KV_HEADS * HEAD_DIM
    qkv = _linear(x, wqkv, nw=nw, bias=bqkv, tn=512)
    q3 = qkv[:, :nq].reshape(bsz, N_Q_HEADS, HEAD_DIM)
    kn = qkv[:, nq:nq + nk].reshape(bsz, 1, nk)
    vn = qkv[:, nq + nk:].reshape(bsz, 1, nk)
    o3, ko, vo = _sample_attn(q3, kn, vn, cache_k.reshape(bsz, wb, nk), cache_v.reshape(bsz, wb, nk),
                              jnp.tile(sinks.reshape(N_Q_HEADS, 1), (8, 1)), bt=8)
    out = _linear(o3.reshape(bsz, nq), wo, bias=bo, res=x, tn=512)
    return out, ko.reshape(cache_k.shape), vo.reshape(cache_v.shape)


def kernel(x_prompt, x_sample, state_ssm, state_conv, cache_k_win, cache_v_win,
           mamba_w_in, mamba_conv_w, mamba_conv_b, mamba_dt_bias, mamba_a_log, mamba_d,
           mamba_norm_w, mamba_w_out, attn_w_qkv, attn_b_qkv, attn_sinks, attn_w_o, attn_b_o,
           norm_mix, norm_ffn, router_w_group, router_b_group, router_w_expert, router_b_expert,
           expert_w_gate, expert_w_up, expert_w_down, norm_final):
    bsz, seq, d = x_prompt.shape
    dbsz, dseq, _ = x_sample.shape
    assert dseq == 1 and cache_k_win.shape[2] <= WINDOW and seq % WINDOW == 0
    assert dbsz <= MOE_TILE and (bsz * seq) % MOE_TILE == 0
    depth = norm_mix.shape[0]
    xp = x_prompt.reshape(bsz * seq, d)
    xs = x_sample.reshape(dbsz, d)
    ssm_p, conv_p, kp_l, vp_l = [], [], [], []
    ssm_s, conv_s, ks_l, vs_l = [], [], [], []
    for i in range(depth):
        j = i // 2
        nw = norm_mix[i].reshape(1, d)
        if i % 2 == 0:
            mw = _mamba_weights(mamba_w_in[j], mamba_conv_w[j], mamba_conv_b[j], mamba_dt_bias[j],
                                mamba_a_log[j], mamba_d[j], mamba_norm_w[j], mamba_w_out[j])
            xp, cp, sp = _mamba_prompt(xp, nw, *mw, bsz=bsz, seq=seq, ts=2 * SSD_CHUNK)
            xs, cs_, ss_ = _mamba_sample(xs, nw, mw, state_conv[j], state_ssm[j])
            ssm_p.append(sp)
            conv_p.append(cp)
            ssm_s.append(ss_)
            conv_s.append(cs_)
        else:
            wqkv = attn_w_qkv[j].astype(BF16)
            bqkv = attn_b_qkv[j].reshape(1, -1)
            wo = attn_w_o[j].astype(BF16)
            bo = attn_b_o[j].reshape(1, d)
            xp, kp, vp = _attn_prompt(xp, attn_sinks[j], nw, wqkv, bqkv, wo, bo, bsz=bsz, seq=seq,
                                      tq=WINDOW)
            xs, ks_, vs_ = _attn_sample(xs, nw, wqkv, bqkv, attn_sinks[j], wo, bo, cache_k_win[j], cache_v_win[j])
            kp_l.append(kp.reshape(bsz, WINDOW, N_KV_HEADS, HEAD_DIM))
            vp_l.append(vp.reshape(bsz, WINDOW, N_KV_HEADS, HEAD_DIM))
            ks_l.append(ks_)
            vs_l.append(vs_)
        last = i == depth - 1
        moe_w = (norm_ffn[i], router_w_group[i], router_b_group[i], router_w_expert[i], router_b_expert[i],
                 expert_w_gate, expert_w_up, expert_w_down, norm_final)
        xs_tile = jnp.pad(xs, ((0, MOE_TILE - dbsz), (0, 0)))
        xp, xs_tile = _moe(xp, xs_tile, *moe_w, layer=i, final_norm=last)
        xs = xs_tile[:dbsz]
    return (xp.reshape(bsz, seq, d), xs.reshape(dbsz, dseq, d),
            jnp.stack(ssm_p), jnp.stack(conv_p), jnp.stack(kp_l), jnp.stack(vp_l),
            jnp.stack(ssm_s), jnp.stack(conv_s), jnp.stack(ks_l), jnp.stack(vs_l))
```

```python
import functools
import math

import jax
import jax.numpy as jnp
from jax import lax
from jax.experimental import pallas as pl
from jax.experimental.pallas import tpu as pltpu

F32 = jnp.float32
BF16 = jnp.bfloat16
I32 = jnp.int32

EPS = 1e-5
LANES = 128
VMEM_LIMIT = 56 * 1024 * 1024

SSM_HEAD_DIM = 64
D_STATE = 128
N_BC_GROUPS = 8
CONV_W = 4
SSD_CHUNK = 128
N_Q_HEADS = 16
N_KV_HEADS = 4
HEAD_DIM = 64
WINDOW = 128
N_EXPERT_GROUPS = 4
EXPERTS_PER_GROUP = 8
N_EXPERTS = N_EXPERT_GROUPS * EXPERTS_PER_GROUP
TOP_K = 2
MOE_BLOCK = 384
MOE_TILE = 512


def _cparams(sem):
    return pltpu.CompilerParams(dimension_semantics=sem, vmem_limit_bytes=VMEM_LIMIT)


def _full(shape):
    n = len(shape)
    return pl.BlockSpec(shape, lambda *_: (0,) * n)


def _resident(shape):
    n = len(shape)
    return pl.BlockSpec(shape, lambda *_: (0,) * n, pipeline_mode=pl.Buffered(1))


def _rms(x, w):
    return x * lax.rsqrt(jnp.mean(x * x, axis=-1, keepdims=True) + EPS) * w


def _silu(x):
    return x / (1.0 + jnp.exp(-x))


def _softplus(x):
    return jnp.maximum(x, 0.0) + jnp.log(1.0 + jnp.exp(-jnp.abs(x)))


def _bdot(a, b):
    return jnp.dot(a.astype(BF16), b.astype(BF16), preferred_element_type=F32)


def _bdot_nt(a, b):
    return lax.dot_general(a.astype(BF16), b.astype(BF16), (((1,), (1,)), ((), ())),
                           preferred_element_type=F32)


def _bdot_tn(a, b):
    return lax.dot_general(a.astype(BF16), b.astype(BF16), (((0,), (0,)), ((), ())),
                           preferred_element_type=F32)


def _split3(v):
    hi = v.astype(BF16)
    r1 = v - hi.astype(F32)
    mid = r1.astype(BF16)
    lo = (r1 - mid.astype(F32)).astype(BF16)
    return hi, mid, lo


def _dot01_left(m01, v):
    m = m01.astype(BF16)
    return jnp.dot(jnp.concatenate([m, m, m], axis=1), jnp.concatenate(_split3(v), axis=0),
                   preferred_element_type=F32)


def _dot01_right(v, m01):
    m = m01.astype(BF16)
    return jnp.dot(jnp.concatenate(_split3(v), axis=1), jnp.concatenate([m, m, m], axis=0),
                   preferred_element_type=F32)


def _spread(v, onehot_ref):
    hi = v.astype(BF16)
    lo = (v - hi.astype(F32)).astype(BF16)
    return jnp.dot(jnp.concatenate([hi, lo], axis=1), onehot_ref[...], preferred_element_type=F32)


def _mamba_prompt_body(x_ref, nw_ref, win_ref, cw_ref, cb_ref, dtb_ref, alog_ref, dsk_ref, gnw_ref,
                       wout_ref, hexp_ref, out_ref, conv_ref, ssm_ref, h_buf, xbc_buf, xc_buf, st_buf, y_buf,
                       wexp_buf, eexp_buf, *, d_inner, n_heads):
    ts = x_ref.shape[0]
    cs = SSD_CHUNK
    hp = SSM_HEAD_DIM
    nst = D_STATE
    gw = d_inner // N_BC_GROUPS
    hpg = n_heads // N_BC_GROUPS
    conv_dim = d_inner + 2 * N_BC_GROUPS * nst
    s = pl.program_id(1)

    @pl.when(s == 0)
    def _():
        xbc_buf[:, 0:8, :] = jnp.zeros((conv_dim // LANES, 8, LANES), F32)
        st_buf[...] = jnp.zeros_like(st_buf)

    h_buf[...] = _rms(x_ref[...], nw_ref[...]).astype(BF16)
    dtr = jnp.dot(h_buf[...], win_ref[:, d_inner + conv_dim:], preferred_element_type=F32)
    ct = 512
    spp = ct // LANES
    for j in range(conv_dim // ct):
        cols = slice(j * ct, (j + 1) * ct)
        piece = jnp.dot(h_buf[...], win_ref[:, d_inner + j * ct:d_inner + (j + 1) * ct],
                        preferred_element_type=F32)
        for q in range(spp):
            xbc_buf[j * spp + q, 8:8 + ts, :] = piece[:, q * LANES:(q + 1) * LANES]

        def back(k):
            return jnp.concatenate([xbc_buf[j * spp + q, pl.ds(8 - k, ts), :] for q in range(spp)], axis=1)

        acc = cb_ref[:, cols] + cw_ref[3:4, cols] * piece
        acc = acc + cw_ref[2:3, cols] * back(1)
        acc = acc + cw_ref[1:2, cols] * back(2)
        acc = acc + cw_ref[0:1, cols] * back(3)
        xc_buf[:, cols] = _silu(acc)
    for c in range(conv_dim // LANES):
        last3 = xbc_buf[c, 5 + ts:8 + ts, :]
        xbc_buf[c, 5:8, :] = last3
        conv_ref[0, :, c * LANES:(c + 1) * LANES] = last3

    dt = _softplus(dtr + dtb_ref[...])
    da = dt * (-jnp.exp(alog_ref[...]))
    row = lax.broadcasted_iota(I32, (cs, cs), 0)
    col = lax.broadcasted_iota(I32, (cs, cs), 1)
    causal = row >= col
    tril = causal.astype(F32)
    lane = lax.broadcasted_iota(I32, (cs, LANES), 1)
    lo_half = lane < hp

    for c in range(ts // cs):
        rows = slice(c * cs, (c + 1) * cs)
        da_c = da[rows]
        dt_c = dt[rows]
        acum = _dot01_left(tril, da_c)
        acum_t = acum.T
        dt_t = dt_c.T
        a_last = acum[cs - 1:cs, :]
        to_end = jnp.exp(a_last - acum)
        w_all = dt_c * to_end
        ea = jnp.exp(acum)
        cd = jnp.exp(a_last)
        both = _spread(jnp.concatenate([w_all, ea], axis=0), hexp_ref)
        wexp_buf[...] = both[0:cs]
        eexp_buf[...] = both[cs:2 * cs]
        for g in range(N_BC_GROUPS):
            glanes = slice(g * gw, (g + 1) * gw)
            b_g = xc_buf[rows, d_inner + g * nst:d_inner + (g + 1) * nst]
            c_g = xc_buf[rows, d_inner + (N_BC_GROUPS + g) * nst:d_inner + (N_BC_GROUPS + g + 1) * nst]
            cb = _bdot_nt(c_g, b_g)
            y_off = _bdot(c_g, st_buf[:, glanes])
            xw_parts = []
            for pr in range(hpg // 2):
                h0 = g * hpg + 2 * pr
                lanes0 = slice(h0 * hp, h0 * hp + 2 * hp)
                x_pair = xc_buf[rows, lanes0]
                ms = []
                for k in range(2):
                    hh = h0 + k
                    seg = acum[:, hh:hh + 1] - acum_t[hh:hh + 1, :]
                    dec = jnp.exp(jnp.where(causal, seg, -jnp.inf))
                    ms.append(cb * dec * dt_t[hh:hh + 1, :])
                x_ab = jnp.concatenate([jnp.where(lo_half, x_pair, 0.0), jnp.where(lo_half, 0.0, x_pair)], axis=0)
                y_pair = (eexp_buf[:, lanes0] * y_off[:, 2 * pr * hp:2 * (pr + 1) * hp]
                          + _bdot(jnp.concatenate(ms, axis=1), x_ab))
                y_buf[rows, lanes0] = y_pair + x_pair * dsk_ref[:, lanes0]
                xw_parts.append(x_pair * wexp_buf[:, lanes0])
            xw = jnp.concatenate(xw_parts, axis=1)
            cd_parts = [jnp.broadcast_to(cd[:, g * hpg + k:g * hpg + k + 1], (1, hp)) for k in range(hpg)]
            cd_g = jnp.concatenate(cd_parts, axis=1)
            st_buf[:, glanes] = st_buf[:, glanes] * cd_g + _bdot_tn(b_g, xw)

    @pl.when(s == pl.num_programs(1) - 1)
    def _():
        for pr in range(n_heads // 2):
            t = st_buf[:, 2 * pr * hp:2 * (pr + 1) * hp].T
            ssm_ref[0, 2 * pr] = t[0:hp]
            ssm_ref[0, 2 * pr + 1] = t[hp:2 * hp]

    acc = x_ref[...]
    for g in range(N_BC_GROUPS):
        glanes = slice(g * gw, (g + 1) * gw)
        z = jnp.dot(h_buf[...], win_ref[:, glanes], preferred_element_type=F32)
        gg = y_buf[:, glanes] * _silu(z)
        gg = gg * lax.rsqrt(jnp.mean(gg * gg, axis=-1, keepdims=True) + EPS) * gnw_ref[:, glanes]
        acc = acc + jnp.dot(gg.astype(BF16), wout_ref[glanes, :], preferred_element_type=F32)
    out_ref[...] = acc


def _mamba_prompt(x, nw, win, cw, cb, dtb, alog, dsk, gnw, wout, *, bsz, seq, ts):
    d = x.shape[1]
    d_inner = wout.shape[0]
    n_heads = d_inner // SSM_HEAD_DIM
    conv_dim = cw.shape[1]
    ns = seq // ts
    body = functools.partial(_mamba_prompt_body, d_inner=d_inner, n_heads=n_heads)
    hexp = (jnp.arange(2 * LANES, dtype=I32)[:, None] % LANES
            == jnp.arange(d_inner, dtype=I32)[None, :] // SSM_HEAD_DIM).astype(BF16)
    return pl.pallas_call(
        body,
        grid=(bsz, ns),
        in_specs=[
            pl.BlockSpec((ts, d), lambda b, s: (b * ns + s, 0)),
            _full(nw.shape), _resident(win.shape), _full(cw.shape), _full(cb.shape), _full(dtb.shape),
            _full(alog.shape), _full(dsk.shape), _full(gnw.shape), _resident(wout.shape), _resident(hexp.shape),
        ],
        out_specs=[
            pl.BlockSpec((ts, d), lambda b, s: (b * ns + s, 0)),
            pl.BlockSpec((1, CONV_W - 1, conv_dim), lambda b, s: (b, 0, 0)),
            pl.BlockSpec((1, n_heads, SSM_HEAD_DIM, D_STATE), lambda b, s: (b, 0, 0, 0)),
        ],
        out_shape=[
            jax.ShapeDtypeStruct((bsz * seq, d), F32),
            jax.ShapeDtypeStruct((bsz, CONV_W - 1, conv_dim), F32),
            jax.ShapeDtypeStruct((bsz, n_heads, SSM_HEAD_DIM, D_STATE), F32),
        ],
        scratch_shapes=[
            pltpu.VMEM((ts, d), BF16),
            pltpu.VMEM((conv_dim // LANES, 8 + ts, LANES), F32),
            pltpu.VMEM((ts, conv_dim), F32),
            pltpu.VMEM((D_STATE, d_inner), F32),
            pltpu.VMEM((ts, d_inner), F32),
            pltpu.VMEM((SSD_CHUNK, d_inner), F32),
            pltpu.VMEM((SSD_CHUNK, d_inner), F32),
        ],
        compiler_params=_cparams(("arbitrary", "arbitrary")),
        name="mamba_prompt",
    )(x, nw, win, cw, cb, dtb, alog, dsk, gnw, wout, hexp)


def _mamba_weights(w_in, conv_w, conv_b, dt_bias, a_log, d_skip, norm_w, w_out):
    d_inner = w_out.shape[0]
    n_heads = dt_bias.shape[0]
    pad = LANES - n_heads
    win = jnp.pad(w_in, ((0, 0), (0, pad))).astype(BF16)
    dtb = jnp.pad(dt_bias, (0, pad)).reshape(1, LANES)
    alog = jnp.pad(a_log, (0, pad)).reshape(1, LANES)
    dsk = jnp.repeat(d_skip, SSM_HEAD_DIM).reshape(1, d_inner)
    return (win, conv_w, conv_b.reshape(1, -1), dtb, alog, dsk, norm_w.reshape(1, d_inner),
            w_out.astype(BF16))


def _sink_softmax_pv(s, sink, v):
    m = jnp.maximum(jnp.max(s, axis=-1, keepdims=True), sink)
    p = jnp.exp(s - m)
    denom = jnp.sum(p, axis=-1, keepdims=True) + jnp.exp(sink - m)
    return _bdot(p, v) / denom


def _attn_prompt_body(sink_ref, x_ref, nw_ref, wqkv_ref, bqkv_ref, wo_ref, bo_ref,
                      out_ref, kwin_ref, vwin_ref, kv_buf, q_buf, o_buf):
    blk = WINDOW
    hd = HEAD_DIM
    nq = N_Q_HEADS * hd
    nk = N_KV_HEADS * hd
    qpk = N_Q_HEADS // N_KV_HEADS
    tq = x_ref.shape[0]
    s_id = pl.program_id(1)

    @pl.when(s_id == 0)
    def _():
        kv_buf[0:blk, :] = jnp.zeros((blk, 2 * nk), F32)

    h = _rms(x_ref[...], nw_ref[...]).astype(BF16)
    q_buf[...] = jnp.dot(h, wqkv_ref[:, 0:nq], preferred_element_type=F32) + bqkv_ref[:, 0:nq]
    kv_buf[blk:blk + tq, :] = jnp.dot(h, wqkv_ref[:, nq:], preferred_element_type=F32) + bqkv_ref[:, nq:]
    kwin_ref[0] = kv_buf[tq:tq + blk, 0:nk]
    vwin_ref[0] = kv_buf[tq:tq + blk, nk:]

    row = lax.broadcasted_iota(I32, (blk, 2 * blk), 0)
    col = lax.broadcasted_iota(I32, (blk, 2 * blk), 1)
    diff = row + blk - col
    band = (diff >= 0) & (diff <= WINDOW)
    scale = hd ** -0.5
    for qb in range(tq // blk):
        qrows = slice(qb * blk, (qb + 1) * blk)
        krows = slice(qb * blk, (qb + 2) * blk)
        ok = band & ((col >= blk) | (s_id > 0)) if qb == 0 else band
        for g in range(N_KV_HEADS):
            k_g = kv_buf[krows, g * hd:(g + 1) * hd]
            v_g = kv_buf[krows, nk + g * hd:nk + (g + 1) * hd]
            for j in range(qpk):
                hh = g * qpk + j
                s = _bdot_nt(q_buf[qrows, hh * hd:(hh + 1) * hd], k_g) * scale
                s = jnp.where(ok, s, -jnp.inf)
                o_buf[qrows, hh * hd:(hh + 1) * hd] = _sink_softmax_pv(s, sink_ref[hh], v_g)
    kv_buf[0:blk, :] = kv_buf[tq:tq + blk, :]
    out_ref[...] = (x_ref[...] + jnp.dot(o_buf[...].astype(BF16), wo_ref[...], preferred_element_type=F32)
                    + bo_ref[...])


def _attn_prompt(x, sinks, nw, wqkv, bqkv, wo, bo, *, bsz, seq, tq):
    d = x.shape[1]
    blk = WINDOW
    nb = seq // tq
    nk = N_KV_HEADS * HEAD_DIM
    nq = N_Q_HEADS * HEAD_DIM
    return pl.pallas_call(
        _attn_prompt_body,
        grid=(bsz, nb),
        in_specs=[
            pl.BlockSpec(memory_space=pltpu.SMEM),
            pl.BlockSpec((tq, d), lambda b, s: (b * nb + s, 0)),
            _full(nw.shape), _full(wqkv.shape), _full(bqkv.shape), _full(wo.shape), _full(bo.shape),
        ],
        out_specs=[
            pl.BlockSpec((tq, d), lambda b, s: (b * nb + s, 0)),
            pl.BlockSpec((1, blk, nk), lambda b, s: (b, 0, 0)),
            pl.BlockSpec((1, blk, nk), lambda b, s: (b, 0, 0)),
        ],
        out_shape=[
            jax.ShapeDtypeStruct((bsz * seq, d), F32),
            jax.ShapeDtypeStruct((bsz, blk, nk), F32),
            jax.ShapeDtypeStruct((bsz, blk, nk), F32),
        ],
        scratch_shapes=[
            pltpu.VMEM((blk + tq, 2 * nk), F32),
            pltpu.VMEM((tq, nq), F32),
            pltpu.VMEM((tq, nq), F32),
        ],
        compiler_params=_cparams(("arbitrary", "arbitrary")),
        name="attn_prompt",
    )(sinks, x, nw, wqkv, bqkv, wo, bo)


def _x_specs(xm, tm):
    ntm, d = xm.shape[0] // tm, xm.shape[1]
    return [pl.BlockSpec((tm, d), lambda i: (jnp.minimum(i, ntm - 1), 0)), pl.BlockSpec((tm, d), lambda i: (0, 0))]


def _x_tile(xm_ref, xt_ref, ntm):
    return jnp.where(pl.program_id(0) < ntm, xm_ref[...], xt_ref[...])


def _route_body(xm_ref, xt_ref, nw_ref, wr_ref, br_ref, info_ref, cnt_ref, *, ntm):
    tm = xm_ref.shape[0]
    h = _rms(_x_tile(xm_ref, xt_ref, ntm), nw_ref[...])
    h_hi = h.astype(BF16)
    h_lo = (h - h_hi.astype(F32)).astype(BF16)
    part = jnp.dot(h_hi, wr_ref[...], preferred_element_type=F32)
    logits = (part[:, 0:LANES] + part[:, LANES:] + jnp.dot(h_lo, wr_ref[:, 0:LANES], preferred_element_type=F32)
              + br_ref[...])
    lane_i = lax.broadcasted_iota(I32, (tm, LANES), 1)
    lane = lane_i.astype(F32)
    lane_grp = (lane_i // EXPERTS_PER_GROUP).astype(F32)
    big = float(LANES)
    ninf = -jnp.inf

    def first_argmax(v):
        m = jnp.max(v, axis=-1, keepdims=True)
        return m, jnp.min(jnp.where(v == m, lane, big), axis=-1, keepdims=True)

    gmask = (lane_i >= N_EXPERTS) & (lane_i < N_EXPERTS + N_EXPERT_GROUPS)
    gl = jnp.where(gmask, logits, ninf)
    gmax, gi = first_argmax(gl)
    gi = gi - float(N_EXPERTS)
    pg = 1.0 / jnp.sum(jnp.exp(gl - gmax), axis=-1, keepdims=True)
    emask = (lane_i < N_EXPERTS) & (lane_grp == gi)
    el = jnp.where(emask, logits, ninf)
    m1, i1 = first_argmax(el)
    el2 = jnp.where(lane == i1, ninf, el)
    m2, i2 = first_argmax(el2)
    den = jnp.sum(jnp.exp(el - m1), axis=-1, keepdims=True)
    tp1 = 1.0 / den
    tp2 = jnp.exp(m2 - m1) / den
    g1 = pg * tp1 / (tp1 + tp2)
    g2 = pg * tp2 / (tp1 + tp2)
    hot1 = lane == i1
    hot2 = lane == i2
    onehot = jnp.where(hot1 | hot2, 1.0, 0.0)
    rr = lax.broadcasted_iota(I32, (tm, tm), 0)
    cc = lax.broadcasted_iota(I32, (tm, tm), 1)
    before = jnp.where(rr > cc, 1.0, 0.0)
    cum = _bdot(before, onehot)
    r1 = jnp.sum(jnp.where(hot1, cum, 0.0), axis=-1, keepdims=True)
    r2 = jnp.sum(jnp.where(hot2, cum, 0.0), axis=-1, keepdims=True)
    cnt_row = jnp.sum(onehot, axis=0, keepdims=True)
    nwin_row = jnp.floor((cnt_row + (SEG_W - 1.0)) * (1.0 / SEG_W))
    er = lax.broadcasted_iota(I32, (LANES, LANES), 0)
    ec = lax.broadcasted_iota(I32, (LANES, LANES), 1)
    earlier = jnp.where(er < ec, 1.0, 0.0)
    both = jnp.concatenate([jnp.broadcast_to(cnt_row, (8, LANES)), jnp.broadcast_to(nwin_row, (8, LANES))], axis=0)
    pre = _dot01_right(both, earlier)
    start = pre[0:1, :]
    start_al = pre[8:9, :] * float(SEG_W)

    def at(hot, row):
        return jnp.sum(jnp.where(hot, row, 0.0), axis=-1, keepdims=True)

    info = jnp.zeros((tm, LANES), F32)
    for k, v in enumerate((g1, g2, i1, i2, at(hot1, start) + r1, at(hot2, start) + r2,
                           at(hot1, start_al) + r1, at(hot2, start_al) + r2)):
        info = jnp.where(lane_i == k, v, info)
    info_ref[...] = info.T[0:8, :]
    cnt_ref[0] = jnp.broadcast_to(cnt_row, (8, LANES))


ROUTE_ROWS = 8


def _route(xm, xt, nw, wr, br, *, tm):
    ntm = xm.shape[0] // tm
    nt = ntm + 1
    return pl.pallas_call(
        functools.partial(_route_body, ntm=ntm),
        grid=(nt,),
        in_specs=_x_specs(xm, tm) + [_full(nw.shape), _full(wr.shape), _full(br.shape)],
        out_specs=[pl.BlockSpec((ROUTE_ROWS, tm), lambda i: (0, i)), pl.BlockSpec((1, 8, LANES), lambda i: (i, 0, 0))],
        out_shape=[jax.ShapeDtypeStruct((ROUTE_ROWS, nt * tm), F32), jax.ShapeDtypeStruct((nt, 8, LANES), F32)],
        compiler_params=_cparams(("arbitrary",)),
        name="moe_route",
    )(xm, xt, nw, wr, br)


def _to_tiles(ref, base, val):
    m, rt = val.shape[0], val.shape[1] // LANES
    for j in range(rt):
        ref[pl.ds(base * rt + j, m, stride=rt), :] = val[:, j * LANES:(j + 1) * LANES]


def _from_tiles(ref, base, m, rt):
    return jnp.concatenate([ref[pl.ds(base * rt + j, m, stride=rt), :] for j in range(rt)], axis=1)


SEG_W = 16
WIN_HDR = 2


def _max_windows(tm):
    return N_EXPERTS + TOP_K * tm // SEG_W


def _seg_copy(src, i, dst, j, sem, rt):
    n = SEG_W * rt
    return pltpu.make_async_copy(src.at[pl.ds(pl.multiple_of(i * rt, rt), n), :],
                                 dst.at[pl.ds(pl.multiple_of(j * rt, rt), n), :], sem)


def _tok(ref, p, rt):
    return ref.at[pl.ds(pl.multiple_of(p * rt, rt), rt), :]


def _dispatch_body(lpos_ref, win_ref, zwin_ref, xm_ref, xt_ref, nw_ref, xb_ref, h_buf, s_buf, sem, *, ntm):
    tm, rt = xm_ref.shape[0], xm_ref.shape[1] // LANES
    i = pl.program_id(0)
    half = TOP_K * tm + SEG_W
    sbase = (i % 2) * half
    mw = _max_windows(tm)

    @pl.when(i == 0)
    def _():
        for hb in range(2):
            s_buf[(hb * half + TOP_K * tm) * rt:(hb + 1) * half * rt, :] = jnp.zeros((SEG_W * rt, LANES), F32)

        def zissue(w, carry):
            _seg_copy(s_buf, TOP_K * tm, xb_ref, zwin_ref[1 + w], sem, rt).start()
            return carry

        def zdrain(w, carry):
            _seg_copy(s_buf, 0, xb_ref, 0, sem, rt).wait()
            return carry

        lax.fori_loop(0, zwin_ref[0], zissue, 0)
        lax.fori_loop(0, zwin_ref[0], zdrain, 0)

    _to_tiles(h_buf, 0, _rms(_x_tile(xm_ref, xt_ref, ntm), nw_ref[...]))

    def move(t, carry):
        v = _tok(h_buf, t, rt)[...]
        for k in range(TOP_K):
            _tok(s_buf, sbase + lpos_ref[0, 0, k * tm + t], rt)[...] = v
        return carry

    lax.fori_loop(0, tm, move, 0, unroll=8)

    def drain(w, carry):
        _seg_copy(s_buf, 0, xb_ref, 0, sem, rt).wait()
        return carry

    @pl.when(i > 0)
    def _():
        lax.fori_loop(0, win_ref[0, 0, 1], drain, 0)

    def issue(w, carry):
        _seg_copy(s_buf, sbase + win_ref[0, 0, WIN_HDR + w], xb_ref, win_ref[0, 0, WIN_HDR + mw + w], sem, rt).start()
        return carry

    lax.fori_loop(0, win_ref[0, 0, 0], issue, 0)

    @pl.when(i == pl.num_programs(0) - 1)
    def _():
        lax.fori_loop(0, win_ref[0, 0, 0], drain, 0)


def _dispatch(xm, xt, nw, lpos, win, zwin, *, tm, n_slots):
    d = xm.shape[1]
    ntm = xm.shape[0] // tm
    nt = ntm + 1
    rt = d // LANES
    return pl.pallas_call(
        functools.partial(_dispatch_body, ntm=ntm),
        grid=(nt,),
        in_specs=[
            pl.BlockSpec((1, 1, lpos.shape[2]), lambda i: (i, 0, 0), memory_space=pltpu.SMEM),
            pl.BlockSpec((1, 1, win.shape[2]), lambda i: (i, 0, 0), memory_space=pltpu.SMEM),
            pl.BlockSpec(memory_space=pltpu.SMEM),
        ] + _x_specs(xm, tm) + [
            _full(nw.shape),
        ],
        out_specs=pl.BlockSpec(memory_space=pl.ANY),
        out_shape=jax.ShapeDtypeStruct((n_slots * rt, LANES), F32),
        scratch_shapes=[pltpu.VMEM((tm * rt, LANES), F32),
                        pltpu.VMEM((2 * (TOP_K * tm + SEG_W) * rt, LANES), F32),
                        pltpu.SemaphoreType.DMA(())],
        compiler_params=_cparams(("arbitrary",)),
        name="moe_dispatch",
    )(lpos, win, zwin, xm, xt, nw)


def _expert_body(be_ref, nu_ref, xb_ref, wg_ref, wu_ref, wd_ref, yb_ref, wg_buf, wu_buf, wd_buf):
    b = pl.program_id(0)
    prev = be_ref[jnp.maximum(b - 1, 0)]
    fresh = (b == 0) | (be_ref[b] != prev)

    @pl.when((b < nu_ref[0]) & fresh)
    def _():
        wg_buf[...] = wg_ref[0, 0].astype(BF16)
        wu_buf[...] = wu_ref[0, 0].astype(BF16)
        wd_buf[...] = wd_ref[0, 0].astype(BF16)

    @pl.when(b < nu_ref[0])
    def _():
        xb = _from_tiles(xb_ref, 0, MOE_BLOCK, wg_buf.shape[0] // LANES).astype(BF16)
        gate = jnp.dot(xb, wg_buf[...], preferred_element_type=F32)
        up = jnp.dot(xb, wu_buf[...], preferred_element_type=F32)
        hid = (_silu(gate) * up).astype(BF16)
        _to_tiles(yb_ref, 0, jnp.dot(hid, wd_buf[...], preferred_element_type=F32))

    @pl.when(b >= nu_ref[0])
    def _():
        yb_ref[...] = jnp.zeros_like(yb_ref)


def _experts(blk_exp, n_used, xb, wg, wu, wd, *, layer):
    d, f = wg.shape[2], wg.shape[3]
    rt = d // LANES
    nb = xb.shape[0] // rt // MOE_BLOCK
    blk_rows = MOE_BLOCK * rt

    def xmap(b, be, nu):
        return (jnp.minimum(b, jnp.maximum(nu[0] - 1, 0)), 0)

    def wmap(b, be, nu):
        return (layer, be[b], 0, 0)

    return pl.pallas_call(
        _expert_body,
        grid_spec=pltpu.PrefetchScalarGridSpec(
            num_scalar_prefetch=2,
            grid=(nb,),
            in_specs=[
                pl.BlockSpec((blk_rows, LANES), xmap),
                pl.BlockSpec((1, 1, d, f), wmap), pl.BlockSpec((1, 1, d, f), wmap),
                pl.BlockSpec((1, 1, f, d), wmap),
            ],
            out_specs=pl.BlockSpec((blk_rows, LANES), lambda b, be, nu: (b, 0)),
            scratch_shapes=[pltpu.VMEM((d, f), BF16), pltpu.VMEM((d, f), BF16), pltpu.VMEM((f, d), BF16)],
        ),
        out_shape=jax.ShapeDtypeStruct(xb.shape, F32),
        compiler_params=_cparams(("arbitrary",)),
        name="moe_experts",
    )(blk_exp, n_used, xb, wg, wu, wd)


def _ybuf_tokens(tm):
    return TOP_K * tm + N_EXPERTS * (SEG_W - 1) + SEG_W


def _combine_body(lpos_ref, gate_ref, win_ref, winn_ref, xm_ref, xt_ref, fw_ref, yb_ref, om_ref, ot_ref,
                  y_buf, x_buf, sem, *, ntm, final_norm):
    tm, rt = xm_ref.shape[0], xm_ref.shape[1] // LANES
    i = pl.program_id(0)
    slot = i % 2
    half = _ybuf_tokens(tm)
    mw = _max_windows(tm)

    def fetch(tab_ref, sl):
        def issue(w, carry):
            _seg_copy(yb_ref, tab_ref[0, 0, WIN_HDR + w], y_buf, sl * half + tab_ref[0, 0, WIN_HDR + mw + w],
                      sem.at[sl], rt).start()
            return carry

        lax.fori_loop(0, tab_ref[0, 0, 0], issue, 0)

    @pl.when(i == 0)
    def _():
        fetch(win_ref, 0)

    @pl.when(i + 1 < pl.num_programs(0))
    def _():
        fetch(winn_ref, 1 - slot)

    _to_tiles(x_buf, 0, _x_tile(xm_ref, xt_ref, ntm))

    def drain(w, carry):
        _seg_copy(yb_ref, 0, y_buf, 0, sem.at[slot], rt).wait()
        return carry

    lax.fori_loop(0, win_ref[0, 0, 0], drain, 0)
    ybase = slot * half

    def comb(t, carry):
        acc = _tok(x_buf, t, rt)[...]
        for k in range(TOP_K):
            a = k * tm + t
            acc = acc + gate_ref[0, 0, a] * _tok(y_buf, ybase + lpos_ref[0, 0, a], rt)[...]
        _tok(x_buf, t, rt)[...] = acc
        return carry

    lax.fori_loop(0, tm, comb, 0, unroll=8)
    out = _from_tiles(x_buf, 0, tm, rt)
    if final_norm:
        out = _rms(out, fw_ref[...])

    @pl.when(i < ntm)
    def _():
        om_ref[...] = out

    @pl.when(i == ntm)
    def _():
        ot_ref[...] = out


def _combine(xm, xt, lpos, gates, win, yb, fw, *, tm, final_norm):
    d = xm.shape[1]
    ntm = xm.shape[0] // tm
    nt = ntm + 1
    rt = d // LANES

    def smem(arr, imap):
        return pl.BlockSpec((1, 1, arr.shape[2]), imap, memory_space=pltpu.SMEM)

    return pl.pallas_call(
        functools.partial(_combine_body, ntm=ntm, final_norm=final_norm),
        grid=(nt,),
        in_specs=[
            smem(lpos, lambda i: (i, 0, 0)),
            smem(gates, lambda i: (i, 0, 0)),
            smem(win, lambda i: (i, 0, 0)),
            smem(win, lambda i: (jnp.minimum(i + 1, nt - 1), 0, 0)),
        ] + _x_specs(xm, tm) + [
            _full(fw.shape),
            pl.BlockSpec(memory_space=pl.ANY),
        ],
        out_specs=_x_specs(xm, tm),
        out_shape=[jax.ShapeDtypeStruct(xm.shape, F32), jax.ShapeDtypeStruct(xt.shape, F32)],
        scratch_shapes=[pltpu.VMEM((2 * _ybuf_tokens(tm) * rt, LANES), F32),
                        pltpu.VMEM((tm * rt, LANES), F32),
                        pltpu.SemaphoreType.DMA((2,))],
        compiler_params=_cparams(("arbitrary",)),
        name="moe_combine",
    )(lpos, gates, win, win, xm, xt, fw, yb)


def _moe(xm, xt, nw, w_group, b_group, w_expert, b_expert, wg, wu, wd, fw, *, layer, final_norm):
    tm, d = xt.shape
    nt = xm.shape[0] // tm + 1
    t = nt * tm
    pad = LANES - N_EXPERTS - N_EXPERT_GROUPS
    wr = jnp.pad(jnp.concatenate([w_expert, w_group], axis=1), ((0, 0), (0, pad)))
    wr_hi = wr.astype(BF16)
    wr = jnp.concatenate([wr_hi, (wr - wr_hi.astype(F32)).astype(BF16)], axis=1)
    br = jnp.pad(jnp.concatenate([b_expert, b_group]), (0, pad)).reshape(1, LANES)
    nw2 = nw.reshape(1, d)
    info, cnt = _route(xm, xt, nw2, wr, br, tm=tm)

    def per_tile(rows):
        return rows.reshape(TOP_K, nt, tm).transpose(1, 0, 2).reshape(nt, 1, TOP_K * tm)

    gates = per_tile(info[0:TOP_K])
    lpos = per_tile(info[4:4 + TOP_K].astype(I32))
    lpos_al = per_tile(info[6:6 + TOP_K].astype(I32))
    cnt = cnt[:, 0, :N_EXPERTS].astype(I32)
    total = jnp.sum(cnt, axis=0)
    padded = jnp.where(total > 0, (total + SEG_W + MOE_BLOCK - 2) // MOE_BLOCK * MOE_BLOCK, 0)
    pend = jnp.cumsum(padded)
    pstart = pend - padded
    gstart = pstart[None, :] + jnp.cumsum(cnt, axis=0) - cnt
    lstart = jnp.cumsum(cnt, axis=1) - cnt
    nwin_e = (cnt + SEG_W - 1) // SEG_W
    lstart_al = (jnp.cumsum(nwin_e, axis=1) - nwin_e) * SEG_W

    winc = jnp.cumsum(nwin_e, axis=1)
    nwin = winc[:, -1:]
    mw = _max_windows(tm)
    j = jnp.arange(mw, dtype=I32)[None, :]
    owner = (jnp.sum(winc[:, None, :] <= j[:, :, None], axis=-1)[:, :, None]
             == jnp.arange(N_EXPERTS, dtype=I32)[None, None, :])
    pick = lambda tab: jnp.sum(jnp.where(owner, tab[:, None, :], 0), axis=-1)
    w_off = (j - pick(winc - nwin_e)) * SEG_W
    live = j < nwin
    src_loc = jnp.where(live, pick(lstart) + w_off, 0)
    slot_g = jnp.where(live, pick(gstart) + w_off, 0)
    dst_loc = jnp.where(live, pick(lstart_al) + w_off, 0)
    nprev = jnp.concatenate([jnp.zeros((1, 1), I32), nwin[:-1]], axis=0)
    win_d = jnp.concatenate([nwin, nprev, src_loc, slot_g], axis=1).reshape(nt, 1, WIN_HDR + 2 * mw)
    win_c = jnp.concatenate([nwin, nprev, slot_g, dst_loc], axis=1).reshape(nt, 1, WIN_HDR + 2 * mw)
    n_blocks = -(-(t * TOP_K + N_EXPERTS * (MOE_BLOCK + SEG_W - 2)) // MOE_BLOCK)
    n_slots = n_blocks * MOE_BLOCK
    zfirst = jnp.concatenate([pstart + total // SEG_W * SEG_W, pend[-1:]])
    zend = jnp.concatenate([pend, jnp.full((1,), n_slots, I32)])
    nz_e = (zend - zfirst) // SEG_W
    zinc = jnp.cumsum(nz_e)
    mz = N_EXPERTS * ((MOE_BLOCK + 2 * SEG_W) // SEG_W) + n_slots // SEG_W - TOP_K * t // SEG_W
    jz = jnp.arange(mz, dtype=I32)
    zowner = (jnp.sum(zinc[None, :] <= jz[:, None], axis=-1)[:, None]
              == jnp.arange(N_EXPERTS + 1, dtype=I32)[None, :])
    zpick = lambda tab: jnp.sum(jnp.where(zowner, tab[None, :], 0), axis=-1)
    zslot = jnp.where(jz < zinc[-1], zpick(zfirst) + (jz - zpick(zinc - nz_e)) * SEG_W, 0)
    zwin = jnp.concatenate([zinc[-1:], zslot]).astype(I32)
    blk_exp = jnp.minimum(jnp.sum(pend[None, :] <= (jnp.arange(n_blocks, dtype=I32) * MOE_BLOCK)[:, None], axis=1),
                          N_EXPERTS - 1).astype(I32)
    n_used = (pend[-1] // MOE_BLOCK).astype(I32).reshape(1)

    xb = _dispatch(xm, xt, nw2, lpos, win_d, zwin, tm=tm, n_slots=n_slots)
    yb = _experts(blk_exp, n_used, xb, wg, wu, wd, layer=layer)
    return _combine(xm, xt, lpos_al, gates, win_c, yb, fw.reshape(1, d), tm=tm, final_norm=final_norm)


def _linear_body(x_ref, nw_ref, w_ref, b_ref, r_ref, out_ref, *, norm):
    x = x_ref[...]
    if norm:
        x = _rms(x, nw_ref[...])
    out_ref[...] = _bdot(x, w_ref[...]) + b_ref[...] + r_ref[...]


def _linear(x, w, *, nw=None, bias=None, res=None, tn):
    m, kd = x.shape
    n = w.shape[1]
    norm = nw is not None
    nw = jnp.ones((1, kd), F32) if nw is None else nw
    bias = jnp.zeros((1, n), F32) if bias is None else bias
    res = jnp.zeros((m, n), F32) if res is None else res
    tn = min(tn, n)
    return pl.pallas_call(
        functools.partial(_linear_body, norm=norm),
        grid=(n // tn,),
        in_specs=[
            _full(x.shape), _full(nw.shape),
            pl.BlockSpec((kd, tn), lambda j: (0, j)),
            pl.BlockSpec((1, tn), lambda j: (0, j)),
            pl.BlockSpec((m, tn), lambda j: (0, j)),
        ],
        out_specs=pl.BlockSpec((m, tn), lambda j: (0, j)),
        out_shape=jax.ShapeDtypeStruct((m, n), F32),
        compiler_params=_cparams(("arbitrary",)),
        name="sample_linear",
    )(x, nw, w, bias, res)


def _sample_conv_body(xbc_ref, st_ref, cw_ref, cb_ref, dtr_ref, dtb_ref, xc_ref, stn_ref, dt_ref, cbg_ref,
                      *, d_inner):
    nst = D_STATE
    xbc = xbc_ref[...]
    acc = cb_ref[...] + cw_ref[3:4, :] * xbc
    for k in range(CONV_W - 1):
        acc = acc + cw_ref[k:k + 1, :] * st_ref[k]
    xc = _silu(acc)
    xc_ref[...] = xc
    stn_ref[0] = st_ref[1]
    stn_ref[1] = st_ref[2]
    stn_ref[2] = xbc
    dt_ref[...] = _softplus(dtr_ref[...] + dtb_ref[...])
    lane = lax.broadcasted_iota(I32, (xbc.shape[0], LANES), 1)
    cbg = jnp.zeros((xbc.shape[0], LANES), F32)
    for g in range(N_BC_GROUPS):
        b_g = xc[:, d_inner + g * nst:d_inner + (g + 1) * nst]
        c_g = xc[:, d_inner + (N_BC_GROUPS + g) * nst:d_inner + (N_BC_GROUPS + g + 1) * nst]
        cbg = jnp.where(lane == g, jnp.sum(b_g * c_g, axis=-1, keepdims=True), cbg)
    cbg_ref[...] = cbg


def _sample_conv(xbc, st_t, cw, cb, dtr, dtb, *, d_inner):
    m, cd = xbc.shape
    return pl.pallas_call(
        functools.partial(_sample_conv_body, d_inner=d_inner),
        grid=(1,),
        in_specs=[_full(xbc.shape), _full(st_t.shape), _full(cw.shape), _full(cb.shape), _full(dtr.shape),
                  _full(dtb.shape)],
        out_specs=[_full((m, cd)), _full(st_t.shape), _full((m, LANES)), _full((m, LANES))],
        out_shape=[jax.ShapeDtypeStruct((m, cd), F32), jax.ShapeDtypeStruct(st_t.shape, F32),
                   jax.ShapeDtypeStruct((m, LANES), F32), jax.ShapeDtypeStruct((m, LANES), F32)],
        compiler_params=_cparams(("arbitrary",)),
        name="sample_conv",
    )(xbc, st_t, cw, cb, dtr, dtb)


SSD_REQS_PER_STEP = 4


def _sample_ssd_body(s0_ref, xt_ref, bc_ref, hs_ref, par_ref, sn_ref, yt_ref, *, n_heads):
    hp = SSM_HEAD_DIM
    hpg = n_heads // N_BC_GROUPS
    a = -jnp.exp(par_ref[0:1, :])
    dsk = par_ref[1:2, :]
    head_row = lax.broadcasted_iota(I32, (n_heads, s0_ref.shape[3]), 0)
    for r in range(s0_ref.shape[0]):
        xt = xt_ref[r]
        dt = hs_ref[r, 0:1, :]
        cbh = hs_ref[r, 1:2, :]
        dec = jnp.exp(dt * a)
        xdt = xt * dt
        xdt_b = xdt.astype(BF16)
        yoff = jnp.zeros((hp, n_heads), F32)
        for hh in range(n_heads):
            g = hh // hpg
            b_row = bc_ref[r, g:g + 1, :]
            c_row = bc_ref[r, N_BC_GROUPS + g:N_BC_GROUPS + g + 1, :]
            s0 = s0_ref[r, hh]
            yoff = yoff + _bdot_nt(s0, jnp.where(head_row == hh, c_row, 0.0))
            b_sel = jnp.where(head_row == hh, b_row, 0.0).astype(BF16)
            sn_ref[r, hh] = s0 * dec[:, hh:hh + 1] + jnp.dot(xdt_b, b_sel, preferred_element_type=F32)
        yt_ref[r] = cbh * xdt + yoff * dec + xt * dsk


def _sample_ssd(s0, xt, bc, hs, par):
    bsz, n_heads, hp, nst = s0.shape
    rb = SSD_REQS_PER_STEP if bsz % SSD_REQS_PER_STEP == 0 else 1
    return pl.pallas_call(
        functools.partial(_sample_ssd_body, n_heads=n_heads),
        grid=(bsz // rb,),
        in_specs=[
            pl.BlockSpec((rb, n_heads, hp, nst), lambda b: (b, 0, 0, 0)),
            pl.BlockSpec((rb, hp, n_heads), lambda b: (b, 0, 0)),
            pl.BlockSpec((rb,) + bc.shape[1:], lambda b: (b, 0, 0)),
            pl.BlockSpec((rb,) + hs.shape[1:], lambda b: (b, 0, 0)),
            _full(par.shape),
        ],
        out_specs=[
            pl.BlockSpec((rb, n_heads, hp, nst), lambda b: (b, 0, 0, 0)),
            pl.BlockSpec((rb, hp, n_heads), lambda b: (b, 0, 0)),
        ],
        out_shape=[jax.ShapeDtypeStruct(s0.shape, F32), jax.ShapeDtypeStruct((bsz, hp, n_heads), F32)],
        compiler_params=_cparams(("arbitrary",)),
        name="sample_ssd",
    )(s0, xt, bc, hs, par)


def _sample_gnorm_out_body(y_ref, z_ref, gnw_ref, wout_ref, x_ref, out_ref, *, d_inner):
    gw = d_inner // N_BC_GROUPS
    acc = x_ref[...]
    for g in range(N_BC_GROUPS):
        glanes = slice(g * gw, (g + 1) * gw)
        gg = y_ref[:, glanes] * _silu(z_ref[:, glanes])
        gg = gg * lax.rsqrt(jnp.mean(gg * gg, axis=-1, keepdims=True) + EPS) * gnw_ref[:, glanes]
        acc = acc + jnp.dot(gg.astype(BF16), wout_ref[glanes, :], preferred_element_type=F32)
    out_ref[...] = acc


def _sample_gnorm_out(y, z, gnw, wout, x):
    d_inner = y.shape[1]
    return pl.pallas_call(
        functools.partial(_sample_gnorm_out_body, d_inner=d_inner),
        grid=(1,),
        in_specs=[_full(y.shape), _full(z.shape), _full(gnw.shape), _full(wout.shape), _full(x.shape)],
        out_specs=_full(x.shape),
        out_shape=jax.ShapeDtypeStruct(x.shape, F32),
        compiler_params=_cparams(("arbitrary",)),
        name="sample_gnorm_out",
    )(y, z, gnw, wout, x)


def _sample_attn_body(q_ref, kn_ref, vn_ref, kc_ref, vc_ref, sink_ref, o_ref, ko_ref, vo_ref, s_buf, sn_buf):
    bt = q_ref.shape[0]
    wb = kc_ref.shape[1]
    hd = HEAD_DIM
    nh = N_Q_HEADS
    qpk = N_Q_HEADS // N_KV_HEADS
    scale = hd ** -0.5
    for b in range(bt):
        kn = kn_ref[b]
        kn_h = jnp.concatenate([jnp.broadcast_to(kn[:, g * hd:(g + 1) * hd], (qpk, hd))
                                for g in range(N_KV_HEADS)], axis=0)
        sn_buf[b * nh:(b + 1) * nh, :] = jnp.sum(q_ref[b] * kn_h, axis=-1, keepdims=True) * scale
        for g in range(N_KV_HEADS):
            rows = slice(b * nh + g * qpk, b * nh + (g + 1) * qpk)
            s_buf[rows, :] = _bdot_nt(q_ref[b, g * qpk:(g + 1) * qpk, :], kc_ref[b, :, g * hd:(g + 1) * hd]) * scale
    s = s_buf[...]
    s_new = sn_buf[...]
    sink = sink_ref[...]
    m = jnp.maximum(jnp.maximum(jnp.max(s, axis=-1, keepdims=True), s_new), sink)
    p = jnp.exp(s - m)
    p_new = jnp.exp(s_new - m)
    inv = 1.0 / (jnp.sum(p, axis=-1, keepdims=True) + p_new + jnp.exp(sink - m))
    s_buf[...] = p * inv
    sn_buf[...] = p_new * inv
    for b in range(bt):
        vn = vn_ref[b]
        for g in range(N_KV_HEADS):
            rows = slice(b * nh + g * qpk, b * nh + (g + 1) * qpk)
            cols = slice(g * hd, (g + 1) * hd)
            o_ref[b, g * qpk:(g + 1) * qpk, :] = (_bdot(s_buf[rows, :], vc_ref[b, :, cols])
                                                   + sn_buf[rows, :] * vn[:, cols])
        ko_ref[b, 0:wb - 1, :] = kc_ref[b, 1:wb, :]
        ko_ref[b, wb - 1:wb, :] = kn_ref[b]
        vo_ref[b, 0:wb - 1, :] = vc_ref[b, 1:wb, :]
        vo_ref[b, wb - 1:wb, :] = vn


def _sample_attn(q3, kn, vn, kc, vc, sinks, *, bt):
    bsz, nqh, hd = q3.shape
    wb, nk = kc.shape[1], kc.shape[2]
    return pl.pallas_call(
        _sample_attn_body,
        grid=(bsz // bt,),
        in_specs=[
            pl.BlockSpec((bt, nqh, hd), lambda i: (i, 0, 0)),
            pl.BlockSpec((bt, 1, nk), lambda i: (i, 0, 0)),
            pl.BlockSpec((bt, 1, nk), lambda i: (i, 0, 0)),
            pl.BlockSpec((bt, wb, nk), lambda i: (i, 0, 0)),
            pl.BlockSpec((bt, wb, nk), lambda i: (i, 0, 0)),
            _full(sinks.shape),
        ],
        out_specs=[
            pl.BlockSpec((bt, nqh, hd), lambda i: (i, 0, 0)),
            pl.BlockSpec((bt, wb, nk), lambda i: (i, 0, 0)),
            pl.BlockSpec((bt, wb, nk), lambda i: (i, 0, 0)),
        ],
        out_shape=[jax.ShapeDtypeStruct(q3.shape, F32), jax.ShapeDtypeStruct(kc.shape, F32),
                   jax.ShapeDtypeStruct(vc.shape, F32)],
        scratch_shapes=[pltpu.VMEM((bt * nqh, wb), F32), pltpu.VMEM((bt * nqh, 1), F32)],
        compiler_params=_cparams(("arbitrary",)),
        name="sample_attn",
    )(q3, kn, vn, kc, vc, sinks)


def _mamba_sample(x, nw, mw, state_conv, state_ssm):
    win, cw, cb, dtb, alog, dsk, gnw, wout = mw
    bsz, d = x.shape
    d_inner = wout.shape[0]
    conv_dim = cw.shape[1]
    n_heads = d_inner // SSM_HEAD_DIM
    hp = SSM_HEAD_DIM
    proj = _linear(x, win, nw=nw, tn=896)
    z = proj[:, :d_inner]
    xbc = proj[:, d_inner:d_inner + conv_dim]
    dtr = proj[:, d_inner + conv_dim:]
    st_t = jnp.transpose(state_conv, (1, 0, 2))
    xc, stn_t, dt, cbg = _sample_conv(xbc, st_t, cw, cb, dtr, dtb, d_inner=d_inner)
    conv_new = jnp.transpose(stn_t, (1, 0, 2))
    xt = jnp.transpose(xc[:, :d_inner].reshape(bsz, n_heads, hp), (0, 2, 1))
    bc = xc[:, d_inner:].reshape(bsz, 2 * N_BC_GROUPS, D_STATE)
    cbh = jnp.repeat(cbg[:, :N_BC_GROUPS], n_heads // N_BC_GROUPS, axis=1)
    hs = jnp.stack([dt[:, :n_heads], cbh], axis=1)
    par = jnp.stack([alog[0, :n_heads], dsk.reshape(n_heads, hp)[:, 0]], axis=0)
    ssm_new, yt = _sample_ssd(state_ssm, xt, bc, hs, par)
    y = jnp.transpose(yt, (0, 2, 1)).reshape(bsz, d_inner)
    out = _sample_gnorm_out(y, z, gnw, wout, x)
    return out, conv_new, ssm_new


def _attn_sample(x, nw, wqkv, bqkv, sinks, wo, bo, cache_k, cache_v):
    bsz, d = x.shape
    wb = cache_k.shape[1]
    nq = N_Q_HEADS * HEAD_DIM
    nk = N_KV_HEADS * HEAD_DIM
    qkv = _linear(x, wqkv, nw=nw, bias=bqkv, tn=512)
    q3 = qkv[:, :nq].reshape(bsz, N_Q_HEADS, HEAD_DIM)
    kn = qkv[:, nq:nq + nk].reshape(bsz, 1, nk)
    vn = qkv[:, nq + nk:].reshape(bsz, 1, nk)
    o3, ko, vo = _sample_attn(q3, kn, vn, cache_k.reshape(bsz, wb, nk), cache_v.reshape(bsz, wb, nk),
                              jnp.tile(sinks.reshape(N_Q_HEADS, 1), (8, 1)), bt=8)
    out = _linear(o3.reshape(bsz, nq), wo, bias=bo, res=x, tn=512)
    return out, ko.reshape(cache_k.shape), vo.reshape(cache_v.shape)


def kernel(x_prompt, x_sample, state_ssm, state_conv, cache_k_win, cache_v_win,
           mamba_w_in, mamba_conv_w, mamba_conv_b, mamba_dt_bias, mamba_a_log, mamba_d,
           mamba_norm_w, mamba_w_out, attn_w_qkv, attn_b_qkv, attn_sinks, attn_w_o, attn_b_o,
           norm_mix, norm_ffn, router_w_group, router_b_group, router_w_expert, router_b_expert,
           expert_w_gate, expert_w_up, expert_w_down, norm_final):
    bsz, seq, d = x_prompt.shape
    dbsz, dseq, _ = x_sample.shape
    assert dseq == 1 and cache_k_win.shape[2] <= WINDOW and seq % WINDOW == 0
    assert dbsz <= MOE_TILE and (bsz * seq) % MOE_TILE == 0
    depth = norm_mix.shape[0]
    xp = x_prompt.reshape(bsz * seq, d)
    xs = x_sample.reshape(dbsz, d)
    ssm_p, conv_p, kp_l, vp_l = [], [], [], []
    ssm_s, conv_s, ks_l, vs_l = [], [], [], []
    for i in range(depth):
        j = i // 2
        nw = norm_mix[i].reshape(1, d)
        if i % 2 == 0:
            mw = _mamba_weights(mamba_w_in[j], mamba_conv_w[j], mamba_conv_b[j], mamba_dt_bias[j],
                                mamba_a_log[j], mamba_d[j], mamba_norm_w[j], mamba_w_out[j])
            xp, cp, sp = _mamba_prompt(xp, nw, *mw, bsz=bsz, seq=seq, ts=2 * SSD_CHUNK)
            xs, cs_, ss_ = _mamba_sample(xs, nw, mw, state_conv[j], state_ssm[j])
            ssm_p.append(sp)
            conv_p.append(cp)
            ssm_s.append(ss_)
            conv_s.append(cs_)
        else:
            wqkv = attn_w_qkv[j].astype(BF16)
            bqkv = attn_b_qkv[j].reshape(1, -1)
            wo = attn_w_o[j].astype(BF16)
            bo = attn_b_o[j].reshape(1, d)
            xp, kp, vp = _attn_prompt(xp, attn_sinks[j], nw, wqkv, bqkv, wo, bo, bsz=bsz, seq=seq,
                                      tq=WINDOW)
            xs, ks_, vs_ = _attn_sample(xs, nw, wqkv, bqkv, attn_sinks[j], wo, bo, cache_k_win[j], cache_v_win[j])
            kp_l.append(kp.reshape(bsz, WINDOW, N_KV_HEADS, HEAD_DIM))
            vp_l.append(vp.reshape(bsz, WINDOW, N_KV_HEADS, HEAD_DIM))
            ks_l.append(ks_)
            vs_l.append(vs_)
        last = i == depth - 1
        moe_w = (norm_ffn[i], router_w_group[i], router_b_group[i], router_w_expert[i], router_b_expert[i],
                 expert_w_gate, expert_w_up, expert_w_down, norm_final)
        xs_tile = jnp.pad(xs, ((0, MOE_TILE - dbsz), (0, 0)))
        xp, xs_tile = _moe(xp, xs_tile, *moe_w, layer=i, final_norm=last)
        xs = xs_tile[:dbsz]
    return (xp.reshape(bsz, seq, d), xs.reshape(dbsz, dseq, d),
            jnp.stack(ssm_p), jnp.stack(conv_p), jnp.stack(kp_l), jnp.stack(vp_l),
            jnp.stack(ssm_s), jnp.stack(conv_s), jnp.stack(ks_l), jnp.stack(vs_l))
```

```python
import functools
import math

import jax
import jax.numpy as jnp
from jax import lax
from jax.experimental import pallas as pl
from jax.experimental.pallas import tpu as pltpu

F32 = jnp.float32
BF16 = jnp.bfloat16
I32 = jnp.int32

EPS = 1e-5
LANES = 128
VMEM_LIMIT = 56 * 1024 * 1024

SSM_HEAD_DIM = 64
D_STATE = 128
N_BC_GROUPS = 8
CONV_W = 4
SSD_CHUNK = 128
N_Q_HEADS = 16
N_KV_HEADS = 4
HEAD_DIM = 64
WINDOW = 128
N_EXPERT_GROUPS = 4
EXPERTS_PER_GROUP = 8
N_EXPERTS = N_EXPERT_GROUPS * EXPERTS_PER_GROUP
TOP_K = 2
MOE_BLOCK = 384
MOE_TILE = 512


def _cparams(sem):
    return pltpu.CompilerParams(dimension_semantics=sem, vmem_limit_bytes=VMEM_LIMIT)


def _full(shape):
    n = len(shape)
    return pl.BlockSpec(shape, lambda *_: (0,) * n)


def _resident(shape):
    n = len(shape)
    return pl.BlockSpec(shape, lambda *_: (0,) * n, pipeline_mode=pl.Buffered(1))


def _rms(x, w):
    return x * lax.rsqrt(jnp.mean(x * x, axis=-1, keepdims=True) + EPS) * w


def _silu(x):
    return x / (1.0 + jnp.exp(-x))


def _softplus(x):
    return jnp.maximum(x, 0.0) + jnp.log(1.0 + jnp.exp(-jnp.abs(x)))


def _bdot(a, b):
    return jnp.dot(a.astype(BF16), b.astype(BF16), preferred_element_type=F32)


def _bdot_nt(a, b):
    return lax.dot_general(a.astype(BF16), b.astype(BF16), (((1,), (1,)), ((), ())),
                           preferred_element_type=F32)


def _bdot_tn(a, b):
    return lax.dot_general(a.astype(BF16), b.astype(BF16), (((0,), (0,)), ((), ())),
                           preferred_element_type=F32)


def _split3(v):
    hi = v.astype(BF16)
    r1 = v - hi.astype(F32)
    mid = r1.astype(BF16)
    lo = (r1 - mid.astype(F32)).astype(BF16)
    return hi, mid, lo


def _dot01_left(m01, v):
    m = m01.astype(BF16)
    return jnp.dot(jnp.concatenate([m, m, m], axis=1), jnp.concatenate(_split3(v), axis=0),
                   preferred_element_type=F32)


def _dot01_right(v, m01):
    m = m01.astype(BF16)
    return jnp.dot(jnp.concatenate(_split3(v), axis=1), jnp.concatenate([m, m, m], axis=0),
                   preferred_element_type=F32)


def _spread(v, onehot_ref):
    hi = v.astype(BF16)
    lo = (v - hi.astype(F32)).astype(BF16)
    return jnp.dot(jnp.concatenate([hi, lo], axis=1), onehot_ref[...], preferred_element_type=F32)


def _mamba_prompt_body(x_ref, nw_ref, win_ref, cw_ref, cb_ref, dtb_ref, alog_ref, dsk_ref, gnw_ref,
                       wout_ref, hexp_ref, out_ref, conv_ref, ssm_ref, h_buf, xbc_buf, xc_buf, st_buf, y_buf,
                       wexp_buf, eexp_buf, *, d_inner, n_heads):
    ts = x_ref.shape[0]
    cs = SSD_CHUNK
    hp = SSM_HEAD_DIM
    nst = D_STATE
    gw = d_inner // N_BC_GROUPS
    hpg = n_heads // N_BC_GROUPS
    conv_dim = d_inner + 2 * N_BC_GROUPS * nst
    s = pl.program_id(1)

    @pl.when(s == 0)
    def _():
        xbc_buf[:, 0:8, :] = jnp.zeros((conv_dim // LANES, 8, LANES), F32)
        st_buf[...] = jnp.zeros_like(st_buf)

    h_buf[...] = _rms(x_ref[...], nw_ref[...]).astype(BF16)
    dtr = jnp.dot(h_buf[...], win_ref[:, d_inner + conv_dim:], preferred_element_type=F32)
    ct = 512
    spp = ct // LANES
    for j in range(conv_dim // ct):
        cols = slice(j * ct, (j + 1) * ct)
        piece = jnp.dot(h_buf[...], win_ref[:, d_inner + j * ct:d_inner + (j + 1) * ct],
                        preferred_element_type=F32)
        for q in range(spp):
            xbc_buf[j * spp + q, 8:8 + ts, :] = piece[:, q * LANES:(q + 1) * LANES]

        def back(k):
            return jnp.concatenate([xbc_buf[j * spp + q, pl.ds(8 - k, ts), :] for q in range(spp)], axis=1)

        acc = cb_ref[:, cols] + cw_ref[3:4, cols] * piece
        acc = acc + cw_ref[2:3, cols] * back(1)
        acc = acc + cw_ref[1:2, cols] * back(2)
        acc = acc + cw_ref[0:1, cols] * back(3)
        xc_buf[:, cols] = _silu(acc)
    for c in range(conv_dim // LANES):
        last3 = xbc_buf[c, 5 + ts:8 + ts, :]
        xbc_buf[c, 5:8, :] = last3
        conv_ref[0, :, c * LANES:(c + 1) * LANES] = last3

    dt = _softplus(dtr + dtb_ref[...])
    da = dt * (-jnp.exp(alog_ref[...]))
    row = lax.broadcasted_iota(I32, (cs, cs), 0)
    col = lax.broadcasted_iota(I32, (cs, cs), 1)
    causal = row >= col
    tril = causal.astype(F32)
    lane = lax.broadcasted_iota(I32, (cs, LANES), 1)
    lo_half = lane < hp

    for c in range(ts // cs):
        rows = slice(c * cs, (c + 1) * cs)
        da_c = da[rows]
        dt_c = dt[rows]
        acum = _dot01_left(tril, da_c)
        acum_t = acum.T
        dt_t = dt_c.T
        a_last = acum[cs - 1:cs, :]
        to_end = jnp.exp(a_last - acum)
        w_all = dt_c * to_end
        ea = jnp.exp(acum)
        cd = jnp.exp(a_last)
        both = _spread(jnp.concatenate([w_all, ea], axis=0), hexp_ref)
        wexp_buf[...] = both[0:cs]
        eexp_buf[...] = both[cs:2 * cs]
        for g in range(N_BC_GROUPS):
            glanes = slice(g * gw, (g + 1) * gw)
            b_g = xc_buf[rows, d_inner + g * nst:d_inner + (g + 1) * nst]
            c_g = xc_buf[rows, d_inner + (N_BC_GROUPS + g) * nst:d_inner + (N_BC_GROUPS + g + 1) * nst]
            cb = _bdot_nt(c_g, b_g)
            y_off = _bdot(c_g, st_buf[:, glanes])
            xw_parts = []
            for pr in range(hpg // 2):
                h0 = g * hpg + 2 * pr
                lanes0 = slice(h0 * hp, h0 * hp + 2 * hp)
                x_pair = xc_buf[rows, lanes0]
                ms = []
                for k in range(2):
                    hh = h0 + k
                    seg = acum[:, hh:hh + 1] - acum_t[hh:hh + 1, :]
                    dec = jnp.exp(jnp.where(causal, seg, -jnp.inf))
                    ms.append(cb * dec * dt_t[hh:hh + 1, :])
                x_ab = jnp.concatenate([jnp.where(lo_half, x_pair, 0.0), jnp.where(lo_half, 0.0, x_pair)], axis=0)
                y_pair = (eexp_buf[:, lanes0] * y_off[:, 2 * pr * hp:2 * (pr + 1) * hp]
                          + _bdot(jnp.concatenate(ms, axis=1), x_ab))
                y_buf[rows, lanes0] = y_pair + x_pair * dsk_ref[:, lanes0]
                xw_parts.append(x_pair * wexp_buf[:, lanes0])
            xw = jnp.concatenate(xw_parts, axis=1)
            cd_parts = [jnp.broadcast_to(cd[:, g * hpg + k:g * hpg + k + 1], (1, hp)) for k in range(hpg)]
            cd_g = jnp.concatenate(cd_parts, axis=1)
            st_buf[:, glanes] = st_buf[:, glanes] * cd_g + _bdot_tn(b_g, xw)

    @pl.when(s == pl.num_programs(1) - 1)
    def _():
        for pr in range(n_heads // 2):
            t = st_buf[:, 2 * pr * hp:2 * (pr + 1) * hp].T
            ssm_ref[0, 2 * pr] = t[0:hp]
            ssm_ref[0, 2 * pr + 1] = t[hp:2 * hp]

    acc = x_ref[...]
    for g in range(N_BC_GROUPS):
        glanes = slice(g * gw, (g + 1) * gw)
        z = jnp.dot(h_buf[...], win_ref[:, glanes], preferred_element_type=F32)
        gg = y_buf[:, glanes] * _silu(z)
        gg = gg * lax.rsqrt(jnp.mean(gg * gg, axis=-1, keepdims=True) + EPS) * gnw_ref[:, glanes]
        acc = acc + jnp.dot(gg.astype(BF16), wout_ref[glanes, :], preferred_element_type=F32)
    out_ref[...] = acc


def _mamba_prompt(x, nw, win, cw, cb, dtb, alog, dsk, gnw, wout, *, bsz, seq, ts):
    d = x.shape[1]
    d_inner = wout.shape[0]
    n_heads = d_inner // SSM_HEAD_DIM
    conv_dim = cw.shape[1]
    ns = seq // ts
    body = functools.partial(_mamba_prompt_body, d_inner=d_inner, n_heads=n_heads)
    hexp = (jnp.arange(2 * LANES, dtype=I32)[:, None] % LANES
            == jnp.arange(d_inner, dtype=I32)[None, :] // SSM_HEAD_DIM).astype(BF16)
    return pl.pallas_call(
        body,
        grid=(bsz, ns),
        in_specs=[
            pl.BlockSpec((ts, d), lambda b, s: (b * ns + s, 0)),
            _full(nw.shape), _resident(win.shape), _full(cw.shape), _full(cb.shape), _full(dtb.shape),
            _full(alog.shape), _full(dsk.shape), _full(gnw.shape), _resident(wout.shape), _resident(hexp.shape),
        ],
        out_specs=[
            pl.BlockSpec((ts, d), lambda b, s: (b * ns + s, 0)),
            pl.BlockSpec((1, CONV_W - 1, conv_dim), lambda b, s: (b, 0, 0)),
            pl.BlockSpec((1, n_heads, SSM_HEAD_DIM, D_STATE), lambda b, s: (b, 0, 0, 0)),
        ],
        out_shape=[
            jax.ShapeDtypeStruct((bsz * seq, d), F32),
            jax.ShapeDtypeStruct((bsz, CONV_W - 1, conv_dim), F32),
            jax.ShapeDtypeStruct((bsz, n_heads, SSM_HEAD_DIM, D_STATE), F32),
        ],
        scratch_shapes=[
            pltpu.VMEM((ts, d), BF16),
            pltpu.VMEM((conv_dim // LANES, 8 + ts, LANES), F32),
            pltpu.VMEM((ts, conv_dim), F32),
            pltpu.VMEM((D_STATE, d_inner), F32),
            pltpu.VMEM((ts, d_inner), F32),
            pltpu.VMEM((SSD_CHUNK, d_inner), F32),
            pltpu.VMEM((SSD_CHUNK, d_inner), F32),
        ],
        compiler_params=_cparams(("arbitrary", "arbitrary")),
        name="mamba_prompt",
    )(x, nw, win, cw, cb, dtb, alog, dsk, gnw, wout, hexp)


def _mamba_weights(w_in, conv_w, conv_b, dt_bias, a_log, d_skip, norm_w, w_out):
    d_inner = w_out.shape[0]
    n_heads = dt_bias.shape[0]
    pad = LANES - n_heads
    win = jnp.pad(w_in, ((0, 0), (0, pad))).astype(BF16)
    dtb = jnp.pad(dt_bias, (0, pad)).reshape(1, LANES)
    alog = jnp.pad(a_log, (0, pad)).reshape(1, LANES)
    dsk = jnp.repeat(d_skip, SSM_HEAD_DIM).reshape(1, d_inner)
    return (win, conv_w, conv_b.reshape(1, -1), dtb, alog, dsk, norm_w.reshape(1, d_inner),
            w_out.astype(BF16))


def _sink_softmax_pv(s, sink, v):
    m = jnp.maximum(jnp.max(s, axis=-1, keepdims=True), sink)
    p = jnp.exp(s - m)
    denom = jnp.sum(p, axis=-1, keepdims=True) + jnp.exp(sink - m)
    return _bdot(p, v) / denom


def _attn_prompt_body(sink_ref, x_ref, nw_ref, wqkv_ref, bqkv_ref, wo_ref, bo_ref,
                      out_ref, kwin_ref, vwin_ref, kv_buf, q_buf, o_buf):
    blk = WINDOW
    hd = HEAD_DIM
    nq = N_Q_HEADS * hd
    nk = N_KV_HEADS * hd
    qpk = N_Q_HEADS // N_KV_HEADS
    tq = x_ref.shape[0]
    s_id = pl.program_id(1)

    @pl.when(s_id == 0)
    def _():
        kv_buf[0:blk, :] = jnp.zeros((blk, 2 * nk), F32)

    h = _rms(x_ref[...], nw_ref[...]).astype(BF16)
    q_buf[...] = jnp.dot(h, wqkv_ref[:, 0:nq], preferred_element_type=F32) + bqkv_ref[:, 0:nq]
    kv_buf[blk:blk + tq, :] = jnp.dot(h, wqkv_ref[:, nq:], preferred_element_type=F32) + bqkv_ref[:, nq:]
    kwin_ref[0] = kv_buf[tq:tq + blk, 0:nk]
    vwin_ref[0] = kv_buf[tq:tq + blk, nk:]

    row = lax.broadcasted_iota(I32, (blk, 2 * blk), 0)
    col = lax.broadcasted_iota(I32, (blk, 2 * blk), 1)
    diff = row + blk - col
    band = (diff >= 0) & (diff <= WINDOW)
    scale = hd ** -0.5
    for qb in range(tq // blk):
        qrows = slice(qb * blk, (qb + 1) * blk)
        krows = slice(qb * blk, (qb + 2) * blk)
        ok = band & ((col >= blk) | (s_id > 0)) if qb == 0 else band
        for g in range(N_KV_HEADS):
            k_g = kv_buf[krows, g * hd:(g + 1) * hd]
            v_g = kv_buf[krows, nk + g * hd:nk + (g + 1) * hd]
            for j in range(qpk):
                hh = g * qpk + j
                s = _bdot_nt(q_buf[qrows, hh * hd:(hh + 1) * hd], k_g) * scale
                s = jnp.where(ok, s, -jnp.inf)
                o_buf[qrows, hh * hd:(hh + 1) * hd] = _sink_softmax_pv(s, sink_ref[hh], v_g)
    kv_buf[0:blk, :] = kv_buf[tq:tq + blk, :]
    out_ref[...] = (x_ref[...] + jnp.dot(o_buf[...].astype(BF16), wo_ref[...], preferred_element_type=F32)
                    + bo_ref[...])


def _attn_prompt(x, sinks, nw, wqkv, bqkv, wo, bo, *, bsz, seq, tq):
    d = x.shape[1]
    blk = WINDOW
    nb = seq // tq
    nk = N_KV_HEADS * HEAD_DIM
    nq = N_Q_HEADS * HEAD_DIM
    return pl.pallas_call(
        _attn_prompt_body,
        grid=(bsz, nb),
        in_specs=[
            pl.BlockSpec(memory_space=pltpu.SMEM),
            pl.BlockSpec((tq, d), lambda b, s: (b * nb + s, 0)),
            _full(nw.shape), _full(wqkv.shape), _full(bqkv.shape), _full(wo.shape), _full(bo.shape),
        ],
        out_specs=[
            pl.BlockSpec((tq, d), lambda b, s: (b * nb + s, 0)),
            pl.BlockSpec((1, blk, nk), lambda b, s: (b, 0, 0)),
            pl.BlockSpec((1, blk, nk), lambda b, s: (b, 0, 0)),
        ],
        out_shape=[
            jax.ShapeDtypeStruct((bsz * seq, d), F32),
            jax.ShapeDtypeStruct((bsz, blk, nk), F32),
            jax.ShapeDtypeStruct((bsz, blk, nk), F32),
        ],
        scratch_shapes=[
            pltpu.VMEM((blk + tq, 2 * nk), F32),
            pltpu.VMEM((tq, nq), F32),
            pltpu.VMEM((tq, nq), F32),
        ],
        compiler_params=_cparams(("arbitrary", "arbitrary")),
        name="attn_prompt",
    )(sinks, x, nw, wqkv, bqkv, wo, bo)


def _x_specs(xm, tm):
    ntm, d = xm.shape[0] // tm, xm.shape[1]
    return [pl.BlockSpec((tm, d), lambda i: (jnp.minimum(i, ntm - 1), 0)), pl.BlockSpec((tm, d), lambda i: (0, 0))]


def _x_tile(xm_ref, xt_ref, ntm):
    return jnp.where(pl.program_id(0) < ntm, xm_ref[...], xt_ref[...])


def _route_body(xm_ref, xt_ref, nw_ref, wr_ref, br_ref, info_ref, cnt_ref, *, ntm):
    tm = xm_ref.shape[0]
    h = _rms(_x_tile(xm_ref, xt_ref, ntm), nw_ref[...])
    h_hi = h.astype(BF16)
    h_lo = (h - h_hi.astype(F32)).astype(BF16)
    part = jnp.dot(h_hi, wr_ref[...], preferred_element_type=F32)
    logits = (part[:, 0:LANES] + part[:, LANES:] + jnp.dot(h_lo, wr_ref[:, 0:LANES], preferred_element_type=F32)
              + br_ref[...])
    lane_i = lax.broadcasted_iota(I32, (tm, LANES), 1)
    lane = lane_i.astype(F32)
    lane_grp = (lane_i // EXPERTS_PER_GROUP).astype(F32)
    big = float(LANES)
    ninf = -jnp.inf

    def first_argmax(v):
        m = jnp.max(v, axis=-1, keepdims=True)
        return m, jnp.min(jnp.where(v == m, lane, big), axis=-1, keepdims=True)

    gmask = (lane_i >= N_EXPERTS) & (lane_i < N_EXPERTS + N_EXPERT_GROUPS)
    gl = jnp.where(gmask, logits, ninf)
    gmax, gi = first_argmax(gl)
    gi = gi - float(N_EXPERTS)
    pg = 1.0 / jnp.sum(jnp.exp(gl - gmax), axis=-1, keepdims=True)
    emask = (lane_i < N_EXPERTS) & (lane_grp == gi)
    el = jnp.where(emask, logits, ninf)
    m1, i1 = first_argmax(el)
    el2 = jnp.where(lane == i1, ninf, el)
    m2, i2 = first_argmax(el2)
    den = jnp.sum(jnp.exp(el - m1), axis=-1, keepdims=True)
    tp1 = 1.0 / den
    tp2 = jnp.exp(m2 - m1) / den
    g1 = pg * tp1 / (tp1 + tp2)
    g2 = pg * tp2 / (tp1 + tp2)
    hot1 = lane == i1
    hot2 = lane == i2
    onehot = jnp.where(hot1 | hot2, 1.0, 0.0)
    rr = lax.broadcasted_iota(I32, (tm, tm), 0)
    cc = lax.broadcasted_iota(I32, (tm, tm), 1)
    before = jnp.where(rr > cc, 1.0, 0.0)
    cum = _bdot(before, onehot)
    r1 = jnp.sum(jnp.where(hot1, cum, 0.0), axis=-1, keepdims=True)
    r2 = jnp.sum(jnp.where(hot2, cum, 0.0), axis=-1, keepdims=True)
    cnt_row = jnp.sum(onehot, axis=0, keepdims=True)
    nwin_row = jnp.floor((cnt_row + (SEG_W - 1.0)) * (1.0 / SEG_W))
    er = lax.broadcasted_iota(I32, (LANES, LANES), 0)
    ec = lax.broadcasted_iota(I32, (LANES, LANES), 1)
    earlier = jnp.where(er < ec, 1.0, 0.0)
    both = jnp.concatenate([jnp.broadcast_to(cnt_row, (8, LANES)), jnp.broadcast_to(nwin_row, (8, LANES))], axis=0)
    pre = _dot01_right(both, earlier)
    start = pre[0:1, :]
    start_al = pre[8:9, :] * float(SEG_W)

    def at(hot, row):
        return jnp.sum(jnp.where(hot, row, 0.0), axis=-1, keepdims=True)

    info = jnp.zeros((tm, LANES), F32)
    for k, v in enumerate((g1, g2, i1, i2, at(hot1, start) + r1, at(hot2, start) + r2,
                           at(hot1, start_al) + r1, at(hot2, start_al) + r2)):
        info = jnp.where(lane_i == k, v, info)
    info_ref[...] = info.T[0:8, :]
    cnt_ref[0] = jnp.broadcast_to(cnt_row, (8, LANES))


ROUTE_ROWS = 8


def _route(xm, xt, nw, wr, br, *, tm):
    ntm = xm.shape[0] // tm
    nt = ntm + 1
    return pl.pallas_call(
        functools.partial(_route_body, ntm=ntm),
        grid=(nt,),
        in_specs=_x_specs(xm, tm) + [_full(nw.shape), _full(wr.shape), _full(br.shape)],
        out_specs=[pl.BlockSpec((ROUTE_ROWS, tm), lambda i: (0, i)), pl.BlockSpec((1, 8, LANES), lambda i: (i, 0, 0))],
        out_shape=[jax.ShapeDtypeStruct((ROUTE_ROWS, nt * tm), F32), jax.ShapeDtypeStruct((nt, 8, LANES), F32)],
        compiler_params=_cparams(("arbitrary",)),
        name="moe_route",
    )(xm, xt, nw, wr, br)


def _to_tiles(ref, base, val):
    m, rt = val.shape[0], val.shape[1] // LANES
    for j in range(rt):
        ref[pl.ds(base * rt + j, m, stride=rt), :] = val[:, j * LANES:(j + 1) * LANES]


def _from_tiles(ref, base, m, rt):
    return jnp.concatenate([ref[pl.ds(base * rt + j, m, stride=rt), :] for j in range(rt)], axis=1)


SEG_W = 16
WIN_HDR = 2


def _max_windows(tm):
    return N_EXPERTS + TOP_K * tm // SEG_W


def _seg_copy(src, i, dst, j, sem, rt):
    n = SEG_W * rt
    return pltpu.make_async_copy(src.at[pl.ds(pl.multiple_of(i * rt, rt), n), :],
                                 dst.at[pl.ds(pl.multiple_of(j * rt, rt), n), :], sem)


def _tok(ref, p, rt):
    return ref.at[pl.ds(pl.multiple_of(p * rt, rt), rt), :]


def _dispatch_body(lpos_ref, win_ref, zwin_ref, xm_ref, xt_ref, nw_ref, xb_ref, h_buf, s_buf, sem, *, ntm):
    tm, rt = xm_ref.shape[0], xm_ref.shape[1] // LANES
    i = pl.program_id(0)
    half = TOP_K * tm + SEG_W
    sbase = (i % 2) * half
    mw = _max_windows(tm)

    @pl.when(i == 0)
    def _():
        for hb in range(2):
            s_buf[(hb * half + TOP_K * tm) * rt:(hb + 1) * half * rt, :] = jnp.zeros((SEG_W * rt, LANES), F32)

        def zissue(w, carry):
            _seg_copy(s_buf, TOP_K * tm, xb_ref, zwin_ref[1 + w], sem, rt).start()
            return carry

        def zdrain(w, carry):
            _seg_copy(s_buf, 0, xb_ref, 0, sem, rt).wait()
            return carry

        lax.fori_loop(0, zwin_ref[0], zissue, 0)
        lax.fori_loop(0, zwin_ref[0], zdrain, 0)

    _to_tiles(h_buf, 0, _rms(_x_tile(xm_ref, xt_ref, ntm), nw_ref[...]))

    def move(t, carry):
        v = _tok(h_buf, t, rt)[...]
        for k in range(TOP_K):
            _tok(s_buf, sbase + lpos_ref[0, 0, k * tm + t], rt)[...] = v
        return carry

    lax.fori_loop(0, tm, move, 0, unroll=16)

    def drain(w, carry):
        _seg_copy(s_buf, 0, xb_ref, 0, sem, rt).wait()
        return carry

    @pl.when(i > 0)
    def _():
        lax.fori_loop(0, win_ref[0, 0, 1], drain, 0)

    def issue(w, carry):
        _seg_copy(s_buf, sbase + win_ref[0, 0, WIN_HDR + w], xb_ref, win_ref[0, 0, WIN_HDR + mw + w], sem, rt).start()
        return carry

    lax.fori_loop(0, win_ref[0, 0, 0], issue, 0)

    @pl.when(i == pl.num_programs(0) - 1)
    def _():
        lax.fori_loop(0, win_ref[0, 0, 0], drain, 0)


def _dispatch(xm, xt, nw, lpos, win, zwin, *, tm, n_slots):
    d = xm.shape[1]
    ntm = xm.shape[0] // tm
    nt = ntm + 1
    rt = d // LANES
    return pl.pallas_call(
        functools.partial(_dispatch_body, ntm=ntm),
        grid=(nt,),
        in_specs=[
            pl.BlockSpec((1, 1, lpos.shape[2]), lambda i: (i, 0, 0), memory_space=pltpu.SMEM),
            pl.BlockSpec((1, 1, win.shape[2]), lambda i: (i, 0, 0), memory_space=pltpu.SMEM),
            pl.BlockSpec(memory_space=pltpu.SMEM),
        ] + _x_specs(xm, tm) + [
            _full(nw.shape),
        ],
        out_specs=pl.BlockSpec(memory_space=pl.ANY),
        out_shape=jax.ShapeDtypeStruct((n_slots * rt, LANES), F32),
        scratch_shapes=[pltpu.VMEM((tm * rt, LANES), F32),
                        pltpu.VMEM((2 * (TOP_K * tm + SEG_W) * rt, LANES), F32),
                        pltpu.SemaphoreType.DMA(())],
        compiler_params=_cparams(("arbitrary",)),
        name="moe_dispatch",
    )(lpos, win, zwin, xm, xt, nw)


def _expert_body(be_ref, nu_ref, xb_ref, wg_ref, wu_ref, wd_ref, yb_ref, wg_buf, wu_buf, wd_buf):
    b = pl.program_id(0)
    prev = be_ref[jnp.maximum(b - 1, 0)]
    fresh = (b == 0) | (be_ref[b] != prev)

    @pl.when((b < nu_ref[0]) & fresh)
    def _():
        wg_buf[...] = wg_ref[0, 0].astype(BF16)
        wu_buf[...] = wu_ref[0, 0].astype(BF16)
        wd_buf[...] = wd_ref[0, 0].astype(BF16)

    @pl.when(b < nu_ref[0])
    def _():
        xb = _from_tiles(xb_ref, 0, MOE_BLOCK, wg_buf.shape[0] // LANES).astype(BF16)
        gate = jnp.dot(xb, wg_buf[...], preferred_element_type=F32)
        up = jnp.dot(xb, wu_buf[...], preferred_element_type=F32)
        hid = (_silu(gate) * up).astype(BF16)
        _to_tiles(yb_ref, 0, jnp.dot(hid, wd_buf[...], preferred_element_type=F32))

    @pl.when(b >= nu_ref[0])
    def _():
        yb_ref[...] = jnp.zeros_like(yb_ref)


def _experts(blk_exp, n_used, xb, wg, wu, wd, *, layer):
    d, f = wg.shape[2], wg.shape[3]
    rt = d // LANES
    nb = xb.shape[0] // rt // MOE_BLOCK
    blk_rows = MOE_BLOCK * rt

    def xmap(b, be, nu):
        return (jnp.minimum(b, jnp.maximum(nu[0] - 1, 0)), 0)

    def wmap(b, be, nu):
        return (layer, be[b], 0, 0)

    return pl.pallas_call(
        _expert_body,
        grid_spec=pltpu.PrefetchScalarGridSpec(
            num_scalar_prefetch=2,
            grid=(nb,),
            in_specs=[
                pl.BlockSpec((blk_rows, LANES), xmap),
                pl.BlockSpec((1, 1, d, f), wmap), pl.BlockSpec((1, 1, d, f), wmap),
                pl.BlockSpec((1, 1, f, d), wmap),
            ],
            out_specs=pl.BlockSpec((blk_rows, LANES), lambda b, be, nu: (b, 0)),
            scratch_shapes=[pltpu.VMEM((d, f), BF16), pltpu.VMEM((d, f), BF16), pltpu.VMEM((f, d), BF16)],
        ),
        out_shape=jax.ShapeDtypeStruct(xb.shape, F32),
        compiler_params=_cparams(("arbitrary",)),
        name="moe_experts",
    )(blk_exp, n_used, xb, wg, wu, wd)


def _ybuf_tokens(tm):
    return TOP_K * tm + N_EXPERTS * (SEG_W - 1) + SEG_W


def _combine_body(lpos_ref, gate_ref, win_ref, winn_ref, xm_ref, xt_ref, fw_ref, yb_ref, om_ref, ot_ref,
                  y_buf, x_buf, sem, *, ntm, final_norm):
    tm, rt = xm_ref.shape[0], xm_ref.shape[1] // LANES
    i = pl.program_id(0)
    slot = i % 2
    half = _ybuf_tokens(tm)
    mw = _max_windows(tm)

    def fetch(tab_ref, sl):
        def issue(w, carry):
            _seg_copy(yb_ref, tab_ref[0, 0, WIN_HDR + w], y_buf, sl * half + tab_ref[0, 0, WIN_HDR + mw + w],
                      sem.at[sl], rt).start()
            return carry

        lax.fori_loop(0, tab_ref[0, 0, 0], issue, 0)

    @pl.when(i == 0)
    def _():
        fetch(win_ref, 0)

    @pl.when(i + 1 < pl.num_programs(0))
    def _():
        fetch(winn_ref, 1 - slot)

    _to_tiles(x_buf, 0, _x_tile(xm_ref, xt_ref, ntm))

    def drain(w, carry):
        _seg_copy(yb_ref, 0, y_buf, 0, sem.at[slot], rt).wait()
        return carry

    lax.fori_loop(0, win_ref[0, 0, 0], drain, 0)
    ybase = slot * half

    def comb(t, carry):
        acc = _tok(x_buf, t, rt)[...]
        for k in range(TOP_K):
            a = k * tm + t
            acc = acc + gate_ref[0, 0, a] * _tok(y_buf, ybase + lpos_ref[0, 0, a], rt)[...]
        _tok(x_buf, t, rt)[...] = acc
        return carry

    lax.fori_loop(0, tm, comb, 0, unroll=16)
    out = _from_tiles(x_buf, 0, tm, rt)
    if final_norm:
        out = _rms(out, fw_ref[...])

    @pl.when(i < ntm)
    def _():
        om_ref[...] = out

    @pl.when(i == ntm)
    def _():
        ot_ref[...] = out


def _combine(xm, xt, lpos, gates, win, yb, fw, *, tm, final_norm):
    d = xm.shape[1]
    ntm = xm.shape[0] // tm
    nt = ntm + 1
    rt = d // LANES

    def smem(arr, imap):
        return pl.BlockSpec((1, 1, arr.shape[2]), imap, memory_space=pltpu.SMEM)

    return pl.pallas_call(
        functools.partial(_combine_body, ntm=ntm, final_norm=final_norm),
        grid=(nt,),
        in_specs=[
            smem(lpos, lambda i: (i, 0, 0)),
            smem(gates, lambda i: (i, 0, 0)),
            smem(win, lambda i: (i, 0, 0)),
            smem(win, lambda i: (jnp.minimum(i + 1, nt - 1), 0, 0)),
        ] + _x_specs(xm, tm) + [
            _full(fw.shape),
            pl.BlockSpec(memory_space=pl.ANY),
        ],
        out_specs=_x_specs(xm, tm),
        out_shape=[jax.ShapeDtypeStruct(xm.shape, F32), jax.ShapeDtypeStruct(xt.shape, F32)],
        scratch_shapes=[pltpu.VMEM((2 * _ybuf_tokens(tm) * rt, LANES), F32),
                        pltpu.VMEM((tm * rt, LANES), F32),
                        pltpu.SemaphoreType.DMA((2,))],
        compiler_params=_cparams(("arbitrary",)),
        name="moe_combine",
    )(lpos, gates, win, win, xm, xt, fw, yb)


def _moe(xm, xt, nw, w_group, b_group, w_expert, b_expert, wg, wu, wd, fw, *, layer, final_norm):
    tm, d = xt.shape
    nt = xm.shape[0] // tm + 1
    t = nt * tm
    pad = LANES - N_EXPERTS - N_EXPERT_GROUPS
    wr = jnp.pad(jnp.concatenate([w_expert, w_group], axis=1), ((0, 0), (0, pad)))
    wr_hi = wr.astype(BF16)
    wr = jnp.concatenate([wr_hi, (wr - wr_hi.astype(F32)).astype(BF16)], axis=1)
    br = jnp.pad(jnp.concatenate([b_expert, b_group]), (0, pad)).reshape(1, LANES)
    nw2 = nw.reshape(1, d)
    info, cnt = _route(xm, xt, nw2, wr, br, tm=tm)

    def per_tile(rows):
        return rows.reshape(TOP_K, nt, tm).transpose(1, 0, 2).reshape(nt, 1, TOP_K * tm)

    gates = per_tile(info[0:TOP_K])
    lpos = per_tile(info[4:4 + TOP_K].astype(I32))
    lpos_al = per_tile(info[6:6 + TOP_K].astype(I32))
    cnt = cnt[:, 0, :N_EXPERTS].astype(I32)
    total = jnp.sum(cnt, axis=0)
    padded = jnp.where(total > 0, (total + SEG_W + MOE_BLOCK - 2) // MOE_BLOCK * MOE_BLOCK, 0)
    pend = jnp.cumsum(padded)
    pstart = pend - padded
    gstart = pstart[None, :] + jnp.cumsum(cnt, axis=0) - cnt
    lstart = jnp.cumsum(cnt, axis=1) - cnt
    nwin_e = (cnt + SEG_W - 1) // SEG_W
    lstart_al = (jnp.cumsum(nwin_e, axis=1) - nwin_e) * SEG_W

    winc = jnp.cumsum(nwin_e, axis=1)
    nwin = winc[:, -1:]
    mw = _max_windows(tm)
    j = jnp.arange(mw, dtype=I32)[None, :]
    owner = (jnp.sum(winc[:, None, :] <= j[:, :, None], axis=-1)[:, :, None]
             == jnp.arange(N_EXPERTS, dtype=I32)[None, None, :])
    pick = lambda tab: jnp.sum(jnp.where(owner, tab[:, None, :], 0), axis=-1)
    w_off = (j - pick(winc - nwin_e)) * SEG_W
    live = j < nwin
    src_loc = jnp.where(live, pick(lstart) + w_off, 0)
    slot_g = jnp.where(live, pick(gstart) + w_off, 0)
    dst_loc = jnp.where(live, pick(lstart_al) + w_off, 0)
    nprev = jnp.concatenate([jnp.zeros((1, 1), I32), nwin[:-1]], axis=0)
    win_d = jnp.concatenate([nwin, nprev, src_loc, slot_g], axis=1).reshape(nt, 1, WIN_HDR + 2 * mw)
    win_c = jnp.concatenate([nwin, nprev, slot_g, dst_loc], axis=1).reshape(nt, 1, WIN_HDR + 2 * mw)
    n_blocks = -(-(t * TOP_K + N_EXPERTS * (MOE_BLOCK + SEG_W - 2)) // MOE_BLOCK)
    n_slots = n_blocks * MOE_BLOCK
    zfirst = jnp.concatenate([pstart + total // SEG_W * SEG_W, pend[-1:]])
    zend = jnp.concatenate([pend, jnp.full((1,), n_slots, I32)])
    nz_e = (zend - zfirst) // SEG_W
    zinc = jnp.cumsum(nz_e)
    mz = N_EXPERTS * ((MOE_BLOCK + 2 * SEG_W) // SEG_W) + n_slots // SEG_W - TOP_K * t // SEG_W
    jz = jnp.arange(mz, dtype=I32)
    zowner = (jnp.sum(zinc[None, :] <= jz[:, None], axis=-1)[:, None]
              == jnp.arange(N_EXPERTS + 1, dtype=I32)[None, :])
    zpick = lambda tab: jnp.sum(jnp.where(zowner, tab[None, :], 0), axis=-1)
    zslot = jnp.where(jz < zinc[-1], zpick(zfirst) + (jz - zpick(zinc - nz_e)) * SEG_W, 0)
    zwin = jnp.concatenate([zinc[-1:], zslot]).astype(I32)
    blk_exp = jnp.minimum(jnp.sum(pend[None, :] <= (jnp.arange(n_blocks, dtype=I32) * MOE_BLOCK)[:, None], axis=1),
                          N_EXPERTS - 1).astype(I32)
    n_used = (pend[-1] // MOE_BLOCK).astype(I32).reshape(1)

    xb = _dispatch(xm, xt, nw2, lpos, win_d, zwin, tm=tm, n_slots=n_slots)
    yb = _experts(blk_exp, n_used, xb, wg, wu, wd, layer=layer)
    return _combine(xm, xt, lpos_al, gates, win_c, yb, fw.reshape(1, d), tm=tm, final_norm=final_norm)


def _linear_body(x_ref, nw_ref, w_ref, b_ref, r_ref, out_ref, *, norm):
    x = x_ref[...]
    if norm:
        x = _rms(x, nw_ref[...])
    out_ref[...] = _bdot(x, w_ref[...]) + b_ref[...] + r_ref[...]


def _linear(x, w, *, nw=None, bias=None, res=None, tn):
    m, kd = x.shape
    n = w.shape[1]
    norm = nw is not None
    nw = jnp.ones((1, kd), F32) if nw is None else nw
    bias = jnp.zeros((1, n), F32) if bias is None else bias
    res = jnp.zeros((m, n), F32) if res is None else res
    tn = min(tn, n)
    return pl.pallas_call(
        functools.partial(_linear_body, norm=norm),
        grid=(n // tn,),
        in_specs=[
            _full(x.shape), _full(nw.shape),
            pl.BlockSpec((kd, tn), lambda j: (0, j)),
            pl.BlockSpec((1, tn), lambda j: (0, j)),
            pl.BlockSpec((m, tn), lambda j: (0, j)),
        ],
        out_specs=pl.BlockSpec((m, tn), lambda j: (0, j)),
        out_shape=jax.ShapeDtypeStruct((m, n), F32),
        compiler_params=_cparams(("arbitrary",)),
        name="sample_linear",
    )(x, nw, w, bias, res)


def _sample_conv_body(xbc_ref, st_ref, cw_ref, cb_ref, dtr_ref, dtb_ref, xc_ref, stn_ref, dt_ref, cbg_ref,
                      *, d_inner):
    nst = D_STATE
    xbc = xbc_ref[...]
    acc = cb_ref[...] + cw_ref[3:4, :] * xbc
    for k in range(CONV_W - 1):
        acc = acc + cw_ref[k:k + 1, :] * st_ref[k]
    xc = _silu(acc)
    xc_ref[...] = xc
    stn_ref[0] = st_ref[1]
    stn_ref[1] = st_ref[2]
    stn_ref[2] = xbc
    dt_ref[...] = _softplus(dtr_ref[...] + dtb_ref[...])
    lane = lax.broadcasted_iota(I32, (xbc.shape[0], LANES), 1)
    cbg = jnp.zeros((xbc.shape[0], LANES), F32)
    for g in range(N_BC_GROUPS):
        b_g = xc[:, d_inner + g * nst:d_inner + (g + 1) * nst]
        c_g = xc[:, d_inner + (N_BC_GROUPS + g) * nst:d_inner + (N_BC_GROUPS + g + 1) * nst]
        cbg = jnp.where(lane == g, jnp.sum(b_g * c_g, axis=-1, keepdims=True), cbg)
    cbg_ref[...] = cbg


def _sample_conv(xbc, st_t, cw, cb, dtr, dtb, *, d_inner):
    m, cd = xbc.shape
    return pl.pallas_call(
        functools.partial(_sample_conv_body, d_inner=d_inner),
        grid=(1,),
        in_specs=[_full(xbc.shape), _full(st_t.shape), _full(cw.shape), _full(cb.shape), _full(dtr.shape),
                  _full(dtb.shape)],
        out_specs=[_full((m, cd)), _full(st_t.shape), _full((m, LANES)), _full((m, LANES))],
        out_shape=[jax.ShapeDtypeStruct((m, cd), F32), jax.ShapeDtypeStruct(st_t.shape, F32),
                   jax.ShapeDtypeStruct((m, LANES), F32), jax.ShapeDtypeStruct((m, LANES), F32)],
        compiler_params=_cparams(("arbitrary",)),
        name="sample_conv",
    )(xbc, st_t, cw, cb, dtr, dtb)


SSD_REQS_PER_STEP = 4


def _sample_ssd_body(s0_ref, xt_ref, bc_ref, hs_ref, par_ref, sn_ref, yt_ref, *, n_heads):
    hp = SSM_HEAD_DIM
    hpg = n_heads // N_BC_GROUPS
    a = -jnp.exp(par_ref[0:1, :])
    dsk = par_ref[1:2, :]
    head_row = lax.broadcasted_iota(I32, (n_heads, s0_ref.shape[3]), 0)
    for r in range(s0_ref.shape[0]):
        xt = xt_ref[r]
        dt = hs_ref[r, 0:1, :]
        cbh = hs_ref[r, 1:2, :]
        dec = jnp.exp(dt * a)
        xdt = xt * dt
        xdt_b = xdt.astype(BF16)
        yoff = jnp.zeros((hp, n_heads), F32)
        for hh in range(n_heads):
            g = hh // hpg
            b_row = bc_ref[r, g:g + 1, :]
            c_row = bc_ref[r, N_BC_GROUPS + g:N_BC_GROUPS + g + 1, :]
            s0 = s0_ref[r, hh]
            yoff = yoff + _bdot_nt(s0, jnp.where(head_row == hh, c_row, 0.0))
            b_sel = jnp.where(head_row == hh, b_row, 0.0).astype(BF16)
            sn_ref[r, hh] = s0 * dec[:, hh:hh + 1] + jnp.dot(xdt_b, b_sel, preferred_element_type=F32)
        yt_ref[r] = cbh * xdt + yoff * dec + xt * dsk


def _sample_ssd(s0, xt, bc, hs, par):
    bsz, n_heads, hp, nst = s0.shape
    rb = SSD_REQS_PER_STEP if bsz % SSD_REQS_PER_STEP == 0 else 1
    return pl.pallas_call(
        functools.partial(_sample_ssd_body, n_heads=n_heads),
        grid=(bsz // rb,),
        in_specs=[
            pl.BlockSpec((rb, n_heads, hp, nst), lambda b: (b, 0, 0, 0)),
            pl.BlockSpec((rb, hp, n_heads), lambda b: (b, 0, 0)),
            pl.BlockSpec((rb,) + bc.shape[1:], lambda b: (b, 0, 0)),
            pl.BlockSpec((rb,) + hs.shape[1:], lambda b: (b, 0, 0)),
            _full(par.shape),
        ],
        out_specs=[
            pl.BlockSpec((rb, n_heads, hp, nst), lambda b: (b, 0, 0, 0)),
            pl.BlockSpec((rb, hp, n_heads), lambda b: (b, 0, 0)),
        ],
        out_shape=[jax.ShapeDtypeStruct(s0.shape, F32), jax.ShapeDtypeStruct((bsz, hp, n_heads), F32)],
        compiler_params=_cparams(("arbitrary",)),
        name="sample_ssd",
    )(s0, xt, bc, hs, par)


def _sample_gnorm_out_body(y_ref, z_ref, gnw_ref, wout_ref, x_ref, out_ref, *, d_inner):
    gw = d_inner // N_BC_GROUPS
    acc = x_ref[...]
    for g in range(N_BC_GROUPS):
        glanes = slice(g * gw, (g + 1) * gw)
        gg = y_ref[:, glanes] * _silu(z_ref[:, glanes])
        gg = gg * lax.rsqrt(jnp.mean(gg * gg, axis=-1, keepdims=True) + EPS) * gnw_ref[:, glanes]
        acc = acc + jnp.dot(gg.astype(BF16), wout_ref[glanes, :], preferred_element_type=F32)
    out_ref[...] = acc


def _sample_gnorm_out(y, z, gnw, wout, x):
    d_inner = y.shape[1]
    return pl.pallas_call(
        functools.partial(_sample_gnorm_out_body, d_inner=d_inner),
        grid=(1,),
        in_specs=[_full(y.shape), _full(z.shape), _full(gnw.shape), _full(wout.shape), _full(x.shape)],
        out_specs=_full(x.shape),
        out_shape=jax.ShapeDtypeStruct(x.shape, F32),
        compiler_params=_cparams(("arbitrary",)),
        name="sample_gnorm_out",
    )(y, z, gnw, wout, x)


def _sample_attn_body(q_ref, kn_ref, vn_ref, kc_ref, vc_ref, sink_ref, o_ref, ko_ref, vo_ref, s_buf, sn_buf):
    bt = q_ref.shape[0]
    wb = kc_ref.shape[1]
    hd = HEAD_DIM
    nh = N_Q_HEADS
    qpk = N_Q_HEADS // N_KV_HEADS
    scale = hd ** -0.5
    for b in range(bt):
        kn = kn_ref[b]
        kn_h = jnp.concatenate([jnp.broadcast_to(kn[:, g * hd:(g + 1) * hd], (qpk, hd))
                                for g in range(N_KV_HEADS)], axis=0)
        sn_buf[b * nh:(b + 1) * nh, :] = jnp.sum(q_ref[b] * kn_h, axis=-1, keepdims=True) * scale
        for g in range(N_KV_HEADS):
            rows = slice(b * nh + g * qpk, b * nh + (g + 1) * qpk)
            s_buf[rows, :] = _bdot_nt(q_ref[b, g * qpk:(g + 1) * qpk, :], kc_ref[b, :, g * hd:(g + 1) * hd]) * scale
    s = s_buf[...]
    s_new = sn_buf[...]
    sink = sink_ref[...]
    m = jnp.maximum(jnp.maximum(jnp.max(s, axis=-1, keepdims=True), s_new), sink)
    p = jnp.exp(s - m)
    p_new = jnp.exp(s_new - m)
    inv = 1.0 / (jnp.sum(p, axis=-1, keepdims=True) + p_new + jnp.exp(sink - m))
    s_buf[...] = p * inv
    sn_buf[...] = p_new * inv
    for b in range(bt):
        vn = vn_ref[b]
        for g in range(N_KV_HEADS):
            rows = slice(b * nh + g * qpk, b * nh + (g + 1) * qpk)
            cols = slice(g * hd, (g + 1) * hd)
            o_ref[b, g * qpk:(g + 1) * qpk, :] = (_bdot(s_buf[rows, :], vc_ref[b, :, cols])
                                                   + sn_buf[rows, :] * vn[:, cols])
        ko_ref[b, 0:wb - 1, :] = kc_ref[b, 1:wb, :]
        ko_ref[b, wb - 1:wb, :] = kn_ref[b]
        vo_ref[b, 0:wb - 1, :] = vc_ref[b, 1:wb, :]
        vo_ref[b, wb - 1:wb, :] = vn


def _sample_attn(q3, kn, vn, kc, vc, sinks, *, bt):
    bsz, nqh, hd = q3.shape
    wb, nk = kc.shape[1], kc.shape[2]
    return pl.pallas_call(
        _sample_attn_body,
        grid=(bsz // bt,),
        in_specs=[
            pl.BlockSpec((bt, nqh, hd), lambda i: (i, 0, 0)),
            pl.BlockSpec((bt, 1, nk), lambda i: (i, 0, 0)),
            pl.BlockSpec((bt, 1, nk), lambda i: (i, 0, 0)),
            pl.BlockSpec((bt, wb, nk), lambda i: (i, 0, 0)),
            pl.BlockSpec((bt, wb, nk), lambda i: (i, 0, 0)),
            _full(sinks.shape),
        ],
        out_specs=[
            pl.BlockSpec((bt, nqh, hd), lambda i: (i, 0, 0)),
            pl.BlockSpec((bt, wb, nk), lambda i: (i, 0, 0)),
            pl.BlockSpec((bt, wb, nk), lambda i: (i, 0, 0)),
        ],
        out_shape=[jax.ShapeDtypeStruct(q3.shape, F32), jax.ShapeDtypeStruct(kc.shape, F32),
                   jax.ShapeDtypeStruct(vc.shape, F32)],
        scratch_shapes=[pltpu.VMEM((bt * nqh, wb), F32), pltpu.VMEM((bt * nqh, 1), F32)],
        compiler_params=_cparams(("arbitrary",)),
        name="sample_attn",
    )(q3, kn, vn, kc, vc, sinks)


def _mamba_sample(x, nw, mw, state_conv, state_ssm):
    win, cw, cb, dtb, alog, dsk, gnw, wout = mw
    bsz, d = x.shape
    d_inner = wout.shape[0]
    conv_dim = cw.shape[1]
    n_heads = d_inner // SSM_HEAD_DIM
    hp = SSM_HEAD_DIM
    proj = _linear(x, win, nw=nw, tn=896)
    z = proj[:, :d_inner]
    xbc = proj[:, d_inner:d_inner + conv_dim]
    dtr = proj[:, d_inner + conv_dim:]
    st_t = jnp.transpose(state_conv, (1, 0, 2))
    xc, stn_t, dt, cbg = _sample_conv(xbc, st_t, cw, cb, dtr, dtb, d_inner=d_inner)
    conv_new = jnp.transpose(stn_t, (1, 0, 2))
    xt = jnp.transpose(xc[:, :d_inner].reshape(bsz, n_heads, hp), (0, 2, 1))
    bc = xc[:, d_inner:].reshape(bsz, 2 * N_BC_GROUPS, D_STATE)
    cbh = jnp.repeat(cbg[:, :N_BC_GROUPS], n_heads // N_BC_GROUPS, axis=1)
    hs = jnp.stack([dt[:, :n_heads], cbh], axis=1)
    par = jnp.stack([alog[0, :n_heads], dsk.reshape(n_heads, hp)[:, 0]], axis=0)
    ssm_new, yt = _sample_ssd(state_ssm, xt, bc, hs, par)
    y = jnp.transpose(yt, (0, 2, 1)).reshape(bsz, d_inner)
    out = _sample_gnorm_out(y, z, gnw, wout, x)
    return out, conv_new, ssm_new


def _attn_sample(x, nw, wqkv, bqkv, sinks, wo, bo, cache_k, cache_v):
    bsz, d = x.shape
    wb = cache_k.shape[1]
    nq = N_Q_HEADS * HEAD_DIM
    nk = N_KV_HEADS * HEAD_DIM
    qkv = _linear(x, wqkv, nw=nw, bias=bqkv, tn=512)
    q3 = qkv[:, :nq].reshape(bsz, N_Q_HEADS, HEAD_DIM)
    kn = qkv[:, nq:nq + nk].reshape(bsz, 1, nk)
    vn = qkv[:, nq + nk:].reshape(bsz, 1, nk)
    o3, ko, vo = _sample_attn(q3, kn, vn, cache_k.reshape(bsz, wb, nk), cache_v.reshape(bsz, wb, nk),
                              jnp.tile(sinks.reshape(N_Q_HEADS, 1), (8, 1)), bt=8)
    out = _linear(o3.reshape(bsz, nq), wo, bias=bo, res=x, tn=512)
    return out, ko.reshape(cache_k.shape), vo.reshape(cache_v.shape)


def kernel(x_prompt, x_sample, state_ssm, state_conv, cache_k_win, cache_v_win,
           mamba_w_in, mamba_conv_w, mamba_conv_b, mamba_dt_bias, mamba_a_log, mamba_d,
           mamba_norm_w, mamba_w_out, attn_w_qkv, attn_b_qkv, attn_sinks, attn_w_o, attn_b_o,
           norm_mix, norm_ffn, router_w_group, router_b_group, router_w_expert, router_b_expert,
           expert_w_gate, expert_w_up, expert_w_down, norm_final):
    bsz, seq, d = x_prompt.shape
    dbsz, dseq, _ = x_sample.shape
    assert dseq == 1 and cache_k_win.shape[2] <= WINDOW and seq % WINDOW == 0
    assert dbsz <= MOE_TILE and (bsz * seq) % MOE_TILE == 0
    depth = norm_mix.shape[0]
    xp = x_prompt.reshape(bsz * seq, d)
    xs = x_sample.reshape(dbsz, d)
    ssm_p, conv_p, kp_l, vp_l = [], [], [], []
    ssm_s, conv_s, ks_l, vs_l = [], [], [], []
    for i in range(depth):
        j = i // 2
        nw = norm_mix[i].reshape(1, d)
        if i % 2 == 0:
            mw = _mamba_weights(mamba_w_in[j], mamba_conv_w[j], mamba_conv_b[j], mamba_dt_bias[j],
                                mamba_a_log[j], mamba_d[j], mamba_norm_w[j], mamba_w_out[j])
            xp, cp, sp = _mamba_prompt(xp, nw, *mw, bsz=bsz, seq=seq, ts=2 * SSD_CHUNK)
            xs, cs_, ss_ = _mamba_sample(xs, nw, mw, state_conv[j], state_ssm[j])
            ssm_p.append(sp)
            conv_p.append(cp)
            ssm_s.append(ss_)
            conv_s.append(cs_)
        else:
            wqkv = attn_w_qkv[j].astype(BF16)
            bqkv = attn_b_qkv[j].reshape(1, -1)
            wo = attn_w_o[j].astype(BF16)
            bo = attn_b_o[j].reshape(1, d)
            xp, kp, vp = _attn_prompt(xp, attn_sinks[j], nw, wqkv, bqkv, wo, bo, bsz=bsz, seq=seq,
                                      tq=WINDOW)
            xs, ks_, vs_ = _attn_sample(xs, nw, wqkv, bqkv, attn_sinks[j], wo, bo, cache_k_win[j], cache_v_win[j])
            kp_l.append(kp.reshape(bsz, WINDOW, N_KV_HEADS, HEAD_DIM))
            vp_l.append(vp.reshape(bsz, WINDOW, N_KV_HEADS, HEAD_DIM))
            ks_l.append(ks_)
            vs_l.append(vs_)
        last = i == depth - 1
        moe_w = (norm_ffn[i], router_w_group[i], router_b_group[i], router_w_expert[i], router_b_expert[i],
                 expert_w_gate, expert_w_up, expert_w_down, norm_final)
        xs_tile = jnp.pad(xs, ((0, MOE_TILE - dbsz), (0, 0)))
        xp, xs_tile = _moe(xp, xs_tile, *moe_w, layer=i, final_norm=last)
        xs = xs_tile[:dbsz]
    return (xp.reshape(bsz, seq, d), xs.reshape(dbsz, dseq, d),
            jnp.stack(ssm_p), jnp.stack(conv_p), jnp.stack(kp_l), jnp.stack(vp_l),
            jnp.stack(ssm_s), jnp.stack(conv_s), jnp.stack(ks_l), jnp.stack(vs_l))
```
